```python
import math
import jax, jax.numpy as jnp
from jax import lax
import numpy as np

D_MODEL = 1024
BATCH = 2
SEQ = 8192
DEPTH = 4

GRID_W = 64
CTX_LEN = 256
LRU_WIDTH = 256
LRU_BLOCKS = 4
LRU_BLOCK = LRU_WIDTH // LRU_BLOCKS
CONV_W = 4
RG_C = 8.0
GQA_HEADS = 8
GQA_KV_HEADS = 2
GQA_HEAD_DIM = 64
DIFF_HEADS = 4
DIFF_QK_DIM = 32
DIFF_V_DIM = 64
MIX_WIDTH = LRU_WIDTH + GQA_HEADS * GQA_HEAD_DIM + DIFF_HEADS * DIFF_V_DIM
SPLIT_SIZES = (LRU_WIDTH, LRU_WIDTH, GQA_HEADS * GQA_HEAD_DIM, GQA_KV_HEADS * GQA_HEAD_DIM,
               GQA_KV_HEADS * GQA_HEAD_DIM, DIFF_HEADS * 2 * DIFF_QK_DIM, DIFF_HEADS * 2 * DIFF_QK_DIM,
               DIFF_HEADS * DIFF_V_DIM)
IN_WIDTH = sum(SPLIT_SIZES)
N_EXPERTS = 16
N_GROUPS = 4
EXPERTS_PER_GROUP = N_EXPERTS // N_GROUPS
TOP_K = 2
EXPERT_FF = 512
MOE_BLOCK = 128
Q_BLOCK = 128
ROPE_THETA = 10000.0
EPS = 1e-6
DEEPNORM_ALPHA = (2.0 * DEPTH) ** 0.25
DEEPNORM_BETA = (8.0 * DEPTH) ** -0.25

kernel_name = 'hybrid_diffusion_prefix_trunk'


def layer_norm(x, g, b):
    x32 = x.astype(jnp.float32)
    mu = jnp.mean(x32, axis=-1, keepdims=True)
    var = jnp.mean(jnp.square(x32 - mu), axis=-1, keepdims=True)
    return ((x32 - mu) * lax.rsqrt(var + EPS) * g + b).astype(x.dtype)


def rms_norm(x, g):
    x32 = x.astype(jnp.float32)
    return (x32 * lax.rsqrt(jnp.mean(x32 * x32, axis=-1, keepdims=True) + EPS) * g).astype(x.dtype)


def axial_rope_tables(n_tokens, dim):
    rows = n_tokens // GRID_W
    row = jnp.repeat(jnp.arange(rows, dtype=jnp.float32), GRID_W)
    col = jnp.tile(jnp.arange(GRID_W, dtype=jnp.float32), rows)
    n_freq = dim // 4
    inv = ROPE_THETA ** (-jnp.arange(n_freq, dtype=jnp.float32) / n_freq)
    ang = jnp.concatenate([row[:, None] * inv, col[:, None] * inv], axis=-1)
    return jnp.cos(ang), jnp.sin(ang)


def apply_rope(x, rope):
    cos, sin = rope
    xp = x.reshape(x.shape[:-1] + (x.shape[-1] // 2, 2)).astype(jnp.float32)
    x0, x1 = xp[..., 0], xp[..., 1]
    out = jnp.stack([x0 * cos - x1 * sin, x0 * sin + x1 * cos], axis=-1)
    return out.reshape(x.shape).astype(x.dtype)


def split_heads(z, n_heads, dim):
    b, s, _ = z.shape
    return z.reshape(b, s, n_heads, dim).transpose(0, 2, 1, 3)


def merge_heads(o):
    b, h, s, d = o.shape
    return o.transpose(0, 2, 1, 3).reshape(b, s, h * d)


def centred_dwconv(x, w, b):
    s = x.shape[1]
    left = (CONV_W - 1) // 2
    xp = jnp.pad(x, ((0, 0), (left, CONV_W - 1 - left), (0, 0)))
    y = b
    for k in range(CONV_W):
        y = y + xp[:, k:k + s] * w[k]
    return y


def linear_scan(a, b, h0, reverse):
    if h0 is not None:
        edge = -1 if reverse else 0
        b = b.at[:, edge].add(a[:, edge] * h0)
    def combine(l, r):
        return l[0] * r[0], r[0] * l[1] + r[1]
    _, h = lax.associative_scan(combine, (a, b), axis=1, reverse=reverse)
    return h


def rglru_scan(u, w_a, b_a, w_x, b_x, lam, h0, reverse):
    bsz, s, w = u.shape
    ub = u.reshape(bsz, s, LRU_BLOCKS, LRU_BLOCK)
    r = jax.nn.sigmoid(jnp.einsum('bsnc,ncd->bsnd', ub, w_a).reshape(bsz, s, w) + b_a)
    i = jax.nn.sigmoid(jnp.einsum('bsnc,ncd->bsnd', ub, w_x).reshape(bsz, s, w) + b_x)
    log_a = (-RG_C * r.astype(jnp.float32)) * jax.nn.softplus(-lam.astype(jnp.float32))
    a = jnp.exp(log_a)
    mult = jnp.sqrt(-jnp.expm1(2.0 * log_a))
    return linear_scan(a, mult * (i * u).astype(jnp.float32), h0, reverse)


def gqa_heads(zq, zk, zv, q_g, k_g, rope):
    q = rms_norm(split_heads(zq, GQA_HEADS, GQA_HEAD_DIM), q_g)
    k = rms_norm(split_heads(zk, GQA_KV_HEADS, GQA_HEAD_DIM), k_g)
    v = split_heads(zv, GQA_KV_HEADS, GQA_HEAD_DIM)
    if rope is not None:
        q = apply_rope(q, rope)
        k = apply_rope(k, rope)
    return q, k, v


def gqa_attend(q, k, v):
    bsz, h, sq, dh = q.shape
    kvh = k.shape[1]
    g = h // kvh
    nblk = sq // Q_BLOCK
    qb = jnp.moveaxis(q.reshape(bsz, kvh, g, nblk, Q_BLOCK, dh), 3, 0)
    scale = dh ** -0.5
    def body(qblk):
        s = jnp.einsum('bkgqd,bksd->bkgqs', qblk, k).astype(jnp.float32) * scale
        p = jax.nn.softmax(s, axis=-1).astype(v.dtype)
        return jnp.einsum('bkgqs,bksd->bkgqd', p, v)
    o = lax.map(body, qb)
    return jnp.moveaxis(o, 0, 3).reshape(bsz, h, sq, dh)


def diff_heads(zq, zk, zv, rope):
    bsz, s, _ = zq.shape
    q = zq.reshape(bsz, s, DIFF_HEADS, 2, DIFF_QK_DIM).transpose(0, 2, 3, 1, 4)
    k = zk.reshape(bsz, s, DIFF_HEADS, 2, DIFF_QK_DIM).transpose(0, 2, 3, 1, 4)
    v = split_heads(zv, DIFF_HEADS, DIFF_V_DIM)
    if rope is not None:
        q = apply_rope(q, rope)
        k = apply_rope(k, rope)
    return q, k, v


def diff_attend(q, k, v, lam):
    bsz, h, _, sq, dq = q.shape
    dv = v.shape[-1]
    nblk = sq // Q_BLOCK
    qb = jnp.moveaxis(q.reshape(bsz, h, 2, nblk, Q_BLOCK, dq), 3, 0)
    scale = dq ** -0.5
    def body(qblk):
        s = jnp.einsum('bhmqd,bhmsd->bhmqs', qblk, k).astype(jnp.float32) * scale
        p = jax.nn.softmax(s, axis=-1)
        w = (p[:, :, 0] - lam * p[:, :, 1]).astype(v.dtype)
        return jnp.einsum('bhqs,bhsd->bhqd', w, v)
    o = lax.map(body, qb)
    return jnp.moveaxis(o, 0, 2).reshape(bsz, h, sq, dv)


def token_mixer(h_lat, h_ctx, w_in, conv_w, conv_b, rg_wa, rg_ba, rg_wx, rg_bx, rg_lam,
                q_norm_g, k_norm_g, diff_lambda, diff_subln_g, w_out, rope_g, rope_d, lam_init, need_ctx):
    points = np.cumsum(SPLIT_SIZES)[:-1].tolist()
    ax_l, ag_l, gq_l, gk_l, gv_l, dq_l, dk_l, dv_l = jnp.split(h_lat @ w_in, points, axis=-1)
    ax_c, ag_c, gq_c, gk_c, gv_c, dq_c, dk_c, dv_c = jnp.split(h_ctx @ w_in, points, axis=-1)

    u_l = centred_dwconv(ax_l, conv_w, conv_b)
    u_c = centred_dwconv(ax_c, conv_w, conv_b)
    h_l = jnp.zeros(u_l.shape, jnp.float32)
    h_c = jnp.zeros(u_c.shape, jnp.float32)
    for d, rev in enumerate((False, True)):
        hc = rglru_scan(u_c, rg_wa[d], rg_ba[d], rg_wx[d], rg_bx[d], rg_lam[d], None, rev)
        h0 = hc[:, 0] if rev else hc[:, -1]
        h_l = h_l + rglru_scan(u_l, rg_wa[d], rg_ba[d], rg_wx[d], rg_bx[d], rg_lam[d], h0, rev)
        if need_ctx:
            h_c = h_c + hc
    ya_l = jax.nn.gelu(ag_l) * h_l.astype(ag_l.dtype)

    q_l, k_l, v_l = gqa_heads(gq_l, gk_l, gv_l, q_norm_g, k_norm_g, rope_g)
    q_c, k_c, v_c = gqa_heads(gq_c, gk_c, gv_c, q_norm_g, k_norm_g, None)
    yb_l = merge_heads(gqa_attend(q_l, jnp.concatenate([k_c, k_l], axis=2), jnp.concatenate([v_c, v_l], axis=2)))

    dl = diff_lambda.astype(jnp.float32)
    lam = jnp.exp(jnp.sum(dl[0] * dl[1])) - jnp.exp(jnp.sum(dl[2] * dl[3])) + lam_init
    dq_h, dk_h, dv_h = diff_heads(dq_l, dk_l, dv_l, rope_d)
    cq_h, ck_h, cv_h = diff_heads(dq_c, dk_c, dv_c, None)
    oc_l = diff_attend(dq_h, jnp.concatenate([ck_h, dk_h], axis=3), jnp.concatenate([cv_h, dv_h], axis=2), lam)
    yc_l = merge_heads(rms_norm(oc_l, diff_subln_g) * (1.0 - lam_init))

    out_l = jnp.concatenate([ya_l, yb_l, yc_l], axis=-1) @ w_out
    if not need_ctx:
        return out_l, None
    ya_c = jax.nn.gelu(ag_c) * h_c.astype(ag_c.dtype)
    yb_c = merge_heads(gqa_attend(q_c, k_c, v_c))
    yc_c = merge_heads(rms_norm(diff_attend(cq_h, ck_h, cv_h, lam), diff_subln_g) * (1.0 - lam_init))
    out_c = jnp.concatenate([ya_c, yb_c, yc_c], axis=-1) @ w_out
    return out_l, out_c


def route(h, router_w, router_b):
    t = h.shape[0]
    probs = jax.nn.softmax((h @ router_w).astype(jnp.float32), axis=-1)
    sel = (probs + router_b.astype(jnp.float32)).reshape(t, N_GROUPS, EXPERTS_PER_GROUP)
    grp_score = jnp.sum(lax.top_k(sel, TOP_K)[0], axis=-1)
    g_idx = jnp.argmax(grp_score, axis=-1)
    in_grp = jnp.take_along_axis(sel, g_idx[:, None, None], axis=1)[:, 0]
    _, local = lax.top_k(in_grp, TOP_K)
    experts = g_idx[:, None] * EXPERTS_PER_GROUP + local
    w = jnp.take_along_axis(probs, experts, axis=1)
    return experts, w / jnp.sum(w, axis=-1, keepdims=True)


def moe_ffn(h, router_w, router_b, w_gate, w_up, w_down):
    t, d = h.shape
    experts, gates = route(h, router_w, router_b)
    n_assign = t * TOP_K
    flat_e = experts.reshape(n_assign)
    flat_tok = jnp.repeat(jnp.arange(t, dtype=jnp.int32), TOP_K)
    flat_g = gates.reshape(n_assign)
    order = jnp.argsort(flat_e)
    sorted_e = flat_e[order]
    counts = jnp.bincount(flat_e, length=N_EXPERTS)
    padded = (counts + MOE_BLOCK - 1) // MOE_BLOCK * MOE_BLOCK
    start = jnp.cumsum(counts) - counts
    pend = jnp.cumsum(padded)
    pstart = pend - padded
    dest = pstart[sorted_e] + jnp.arange(n_assign, dtype=jnp.int32) - start[sorted_e]
    n_rows = ((n_assign + MOE_BLOCK - 1) // MOE_BLOCK + N_EXPERTS) * MOE_BLOCK
    n_blk = n_rows // MOE_BLOCK
    row_tok = jnp.zeros((n_rows,), jnp.int32).at[dest].set(flat_tok[order])
    row_gate = jnp.zeros((n_rows,), h.dtype).at[dest].set(flat_g[order].astype(h.dtype))
    blk_expert = jnp.clip(jnp.searchsorted(pend, jnp.arange(n_blk) * MOE_BLOCK, side='right'), 0, N_EXPERTS - 1)
    xs = h[row_tok].reshape(n_blk, MOE_BLOCK, d)
    def expert_block(args):
        xb, e = args
        return (jax.nn.silu(xb @ w_gate[e]) * (xb @ w_up[e])) @ w_down[e]
    ys = lax.map(expert_block, (xs, blk_expert)).reshape(n_rows, d)
    return jnp.zeros_like(h).at[row_tok].add(ys * row_gate[:, None])


def setup_inputs(seed: int = 0) -> dict:
    key = jax.random.key(seed)
    ks = jax.random.split(key, 32)
    f32 = jnp.float32
    D = D_MODEL
    def nrm(k, shape, scale):
        return jax.random.normal(k, shape, f32) * scale
    a0 = jax.random.uniform(ks[14], (DEPTH, 2, LRU_WIDTH), f32, 0.9, 0.999)
    s0 = a0 ** (1.0 / RG_C)
    return {
        'x': nrm(ks[0], (BATCH, SEQ, D), 1.0),
        'c': nrm(ks[1], (BATCH, D), 1.0),
        'ctx': nrm(ks[2], (BATCH, CTX_LEN, D), 1.0),
        'c_ctx': nrm(ks[3], (D,), 1.0),
        'ada_w': nrm(ks[4], (DEPTH, D, 6 * D), D ** -0.5),
        'ada_b': nrm(ks[5], (DEPTH, 6 * D), 0.02),
        'w_in': nrm(ks[6], (DEPTH, D, IN_WIDTH), D ** -0.5),
        'conv_w': nrm(ks[7], (DEPTH, CONV_W, LRU_WIDTH), CONV_W ** -0.5),
        'conv_b': nrm(ks[8], (DEPTH, LRU_WIDTH), 0.02),
        'rg_wa': nrm(ks[9], (DEPTH, 2, LRU_BLOCKS, LRU_BLOCK, LRU_BLOCK), LRU_BLOCK ** -0.5),
        'rg_ba': nrm(ks[10], (DEPTH, 2, LRU_WIDTH), 0.02),
        'rg_wx': nrm(ks[11], (DEPTH, 2, LRU_BLOCKS, LRU_BLOCK, LRU_BLOCK), LRU_BLOCK ** -0.5),
        'rg_bx': nrm(ks[12], (DEPTH, 2, LRU_WIDTH), 0.02),
        'rg_lam': jnp.log(s0) - jnp.log1p(-s0),
        'q_norm_g': 1.0 + nrm(ks[15], (DEPTH, GQA_HEAD_DIM), 0.02),
        'k_norm_g': 1.0 + nrm(ks[16], (DEPTH, GQA_HEAD_DIM), 0.02),
        'diff_lambda': nrm(ks[17], (DEPTH, 4, DIFF_QK_DIM), 0.1),
        'diff_subln_g': 1.0 + nrm(ks[18], (DEPTH, DIFF_V_DIM), 0.02),
        'w_out': nrm(ks[19], (DEPTH, MIX_WIDTH, D), MIX_WIDTH ** -0.5 * DEEPNORM_BETA),
        'ln1_g': 1.0 + nrm(ks[20], (DEPTH, D), 0.02),
        'ln1_b': nrm(ks[21], (DEPTH, D), 0.02),
        'ln2_g': 1.0 + nrm(ks[22], (DEPTH, D), 0.02),
        'ln2_b': nrm(ks[23], (DEPTH, D), 0.02),
        'router_w': nrm(ks[24], (D, N_EXPERTS), D ** -0.5),
        'router_b': nrm(ks[25], (N_EXPERTS,), 0.01),
        'exp_w_gate': nrm(ks[26], (DEPTH, N_EXPERTS, D, EXPERT_FF), D ** -0.5),
        'exp_w_up': nrm(ks[27], (DEPTH, N_EXPERTS, D, EXPERT_FF), D ** -0.5),
        'exp_w_down': nrm(ks[28], (DEPTH, N_EXPERTS, EXPERT_FF, D), EXPERT_FF ** -0.5 * DEEPNORM_BETA),
    }


def reference(x, c, ctx, c_ctx, ada_w, ada_b, w_in, conv_w, conv_b, rg_wa, rg_ba, rg_wx, rg_bx, rg_lam,
              q_norm_g, k_norm_g, diff_lambda, diff_subln_g, w_out, ln1_g, ln1_b, ln2_g, ln2_b,
              router_w, router_b, exp_w_gate, exp_w_up, exp_w_down):
    bsz, s, d = x.shape
    rope_g = axial_rope_tables(s, GQA_HEAD_DIM)
    rope_d = axial_rope_tables(s, DIFF_QK_DIM)
    silu_c = jax.nn.silu(c)
    silu_cc = jax.nn.silu(c_ctx)
    for li in range(DEPTH):
        need_ctx = li < DEPTH - 1
        lam_init = 0.8 - 0.6 * math.exp(-0.3 * li)
        mod_l = (silu_c @ ada_w[li] + ada_b[li])[:, None, :]
        mod_c = silu_cc @ ada_w[li] + ada_b[li]
        sh1, sc1, g1, sh2, sc2, g2 = jnp.split(mod_l, 6, axis=-1)
        csh1, csc1, cg1, csh2, csc2, cg2 = jnp.split(mod_c, 6, axis=-1)

        y_l, y_c = token_mixer(x * (1.0 + sc1) + sh1, ctx * (1.0 + csc1) + csh1, w_in[li], conv_w[li], conv_b[li],
                               rg_wa[li], rg_ba[li], rg_wx[li], rg_bx[li], rg_lam[li], q_norm_g[li], k_norm_g[li],
                               diff_lambda[li], diff_subln_g[li], w_out[li], rope_g, rope_d, lam_init, need_ctx)
        x = layer_norm(DEEPNORM_ALPHA * x + g1 * y_l, ln1_g[li], ln1_b[li])
        h_l = x * (1.0 + sc2) + sh2
        if need_ctx:
            ctx = layer_norm(DEEPNORM_ALPHA * ctx + cg1 * y_c, ln1_g[li], ln1_b[li])
            h_c = ctx * (1.0 + csc2) + csh2
            n_lat = bsz * s
            f = moe_ffn(jnp.concatenate([h_l.reshape(n_lat, d), h_c.reshape(-1, d)], axis=0), router_w, router_b,
                        exp_w_gate[li], exp_w_up[li], exp_w_down[li])
            f_l = f[:n_lat].reshape(bsz, s, d)
            ctx = layer_norm(DEEPNORM_ALPHA * ctx + cg2 * f[n_lat:].reshape(ctx.shape), ln2_g[li], ln2_b[li])
        else:
            f_l = moe_ffn(h_l.reshape(bsz * s, d), router_w, router_b, exp_w_gate[li], exp_w_up[li],
                          exp_w_down[li]).reshape(bsz, s, d)
        x = layer_norm(DEEPNORM_ALPHA * x + g2 * f_l, ln2_g[li], ln2_b[li])
    return x
```

```python
import functools
import math

import numpy as np
import jax
import jax.numpy as jnp
from jax import lax
from jax.experimental import pallas as pl
from jax.experimental.pallas import tpu as pltpu

F32 = jnp.float32
BF16 = jnp.bfloat16
I32 = jnp.int32

D_MODEL = 1024
GRID_W = 64
LRU_WIDTH = 256
LRU_BLOCKS = 4
CONV_W = 4
RG_C = 8.0
GQA_HEADS = 8
GQA_KV_HEADS = 2
GQA_HEAD_DIM = 64
DIFF_HEADS = 4
DIFF_QK_DIM = 32
DIFF_V_DIM = 64
IN_WIDTH = 2048
N_EXPERTS = 16
N_GROUPS = 4
EXPERTS_PER_GROUP = 4
EXPERT_FF = 512
ROPE_THETA = 10000.0
EPS = 1e-6
N_MOD = 6

COL_AX, COL_AG, COL_GQ, COL_GK, COL_GV, COL_DQ, COL_DK, COL_DV = 0, 256, 512, 1024, 1152, 1280, 1536, 1792

LANES = 128
SUBLANES = 8
VMEM_LIMIT = 56 * 1024 * 1024

TOKEN_TILE = 256
VT_ROWS = 80
PAIRS_LO = (0, 0, 0, 1, 1, 2)
PAIRS_HI = (1, 2, 3, 2, 3, 3)
N_CLASSES = N_GROUPS * len(PAIRS_LO)
MOE_ROWS = 128

NT_DIMS = (((1,), (1,)), ((), ()))


def _params(semantics):
    return pltpu.CompilerParams(dimension_semantics=semantics, vmem_limit_bytes=VMEM_LIMIT)


def _ada_kernel(c_ref, w_ref, b_ref, o_ref):
    c = c_ref[...]
    o_ref[0] = jnp.dot(c * jax.nn.sigmoid(c), w_ref[0], preferred_element_type=F32) + b_ref[0]


def _ada_modulation(c_rows, ada_w, ada_b):
    depth, d, n = ada_w.shape
    tn = 1536
    return pl.pallas_call(
        _ada_kernel,
        out_shape=jax.ShapeDtypeStruct((depth, SUBLANES, n), F32),
        grid=(depth, n // tn),
        in_specs=[
            pl.BlockSpec((SUBLANES, d), lambda l, j: (0, 0)),
            pl.BlockSpec((1, d, tn), lambda l, j: (l, 0, j)),
            pl.BlockSpec((1, 1, tn), lambda l, j: (l, 0, j)),
        ],
        out_specs=pl.BlockSpec((1, SUBLANES, tn), lambda l, j: (l, 0, j)),
        compiler_params=_params(("arbitrary", "arbitrary")),
        name="ada_modulation",
    )(c_rows, ada_w, ada_b.reshape(depth, 1, n))


def _rotate_pairs(v, cos, sin_next, sin_prev):
    return v * cos + pltpu.roll(v, LANES - 1, 1) * sin_next + pltpu.roll(v, 1, 1) * sin_prev


def _in_proj_kernel(x_ref, mod_ref, w_ref, gain_ref, seg_ref, rope_g_ref, rope_d_ref,
                    zl_ref, q_ref, k_ref, vt_ref, dq_ref, dk_ref, dvt_ref):
    tm = x_ref.shape[1]
    x = x_ref[0]
    shift = mod_ref[0, 0:1, :]
    scale = mod_ref[0, 1:2, :]
    h = (x * (1.0 + scale) + shift).astype(BF16)
    z = jnp.dot(h, w_ref[...], preferred_element_type=F32)
    zl_ref[0] = z[:, COL_AX:COL_GQ]

    ones_rows = jnp.where(lax.broadcasted_iota(I32, (VT_ROWS - GQA_HEAD_DIM, tm), 0) == 0, 1.0, 0.0).astype(BF16)

    seg = seg_ref[...]
    cos, s_next, s_prev = rope_g_ref[0], rope_g_ref[1], rope_g_ref[2]
    for g in range(5):
        v = z[:, COL_GQ + g * LANES:COL_GQ + (g + 1) * LANES]
        sq = v * v
        sq_hi = sq.astype(BF16)
        sq_lo = (sq - sq_hi.astype(F32)).astype(BF16)
        ssq = jnp.dot(sq_hi, seg, preferred_element_type=F32) + jnp.dot(sq_lo, seg, preferred_element_type=F32)
        v = v * lax.rsqrt(ssq * (1.0 / GQA_HEAD_DIM) + EPS) * gain_ref[g:g + 1, :]
        v = _rotate_pairs(v, cos, s_next, s_prev)
        if g < 4:
            v = (v * GQA_HEAD_DIM ** -0.5).astype(BF16)
            q_ref[0, 2 * g] = v[:, :GQA_HEAD_DIM]
            q_ref[0, 2 * g + 1] = v[:, GQA_HEAD_DIM:]
        else:
            v = v.astype(BF16)
            k_ref[0, 0] = v[:, :GQA_HEAD_DIM]
            k_ref[0, 1] = v[:, GQA_HEAD_DIM:]
    vt = z[:, COL_GV:COL_DQ].T.astype(BF16)
    for hd in range(GQA_KV_HEADS):
        vt_ref[0, hd, 0, 0:GQA_HEAD_DIM, :] = vt[hd * GQA_HEAD_DIM:(hd + 1) * GQA_HEAD_DIM]
        vt_ref[0, hd, 0, GQA_HEAD_DIM:VT_ROWS, :] = ones_rows

    cos, s_next, s_prev = rope_d_ref[0], rope_d_ref[1], rope_d_ref[2]
    for g in range(4):
        v = _rotate_pairs(z[:, COL_DQ + g * LANES:COL_DQ + (g + 1) * LANES], cos, s_next, s_prev)
        if g < 2:
            v = v * DIFF_QK_DIM ** -0.5
        v = v.astype(BF16)
        dst = dq_ref if g < 2 else dk_ref
        for j in range(4):
            dst[0, (g % 2) * 4 + j] = v[:, j * DIFF_QK_DIM:(j + 1) * DIFF_QK_DIM]
    dvt = z[:, COL_DV:IN_WIDTH].T.astype(BF16)
    for hd in range(DIFF_HEADS):
        dvt_ref[0, hd, 0, 0:DIFF_V_DIM, :] = dvt[hd * DIFF_V_DIM:(hd + 1) * DIFF_V_DIM]
        dvt_ref[0, hd, 0, DIFF_V_DIM:VT_ROWS, :] = ones_rows


def _in_proj(xc, mod, w_in, gains, seg, rope_g, rope_d, n_ctx_tiles):
    bsz, sk, d = xc.shape
    tm = TOKEN_TILE
    nt = sk // tm
    n_mod_rows = mod.shape[0]

    def mod_map(b, i):
        return (jnp.where(i < n_ctx_tiles, bsz, b), 0, 0)

    assert n_mod_rows > bsz
    return pl.pallas_call(
        _in_proj_kernel,
        out_shape=(
            jax.ShapeDtypeStruct((bsz, sk, COL_GQ), F32),
            jax.ShapeDtypeStruct((bsz, GQA_HEADS, sk, GQA_HEAD_DIM), BF16),
            jax.ShapeDtypeStruct((bsz, GQA_KV_HEADS, sk, GQA_HEAD_DIM), BF16),
            jax.ShapeDtypeStruct((bsz, GQA_KV_HEADS, nt, VT_ROWS, tm), BF16),
            jax.ShapeDtypeStruct((bsz, 2 * DIFF_HEADS, sk, DIFF_QK_DIM), BF16),
            jax.ShapeDtypeStruct((bsz, 2 * DIFF_HEADS, sk, DIFF_QK_DIM), BF16),
            jax.ShapeDtypeStruct((bsz, DIFF_HEADS, nt, VT_ROWS, tm), BF16),
        ),
        grid=(bsz, nt),
        in_specs=[
            pl.BlockSpec((1, tm, d), lambda b, i: (b, i, 0)),
            pl.BlockSpec((1, N_MOD, d), mod_map),
            pl.BlockSpec((d, IN_WIDTH), lambda b, i: (0, 0)),
            pl.BlockSpec((SUBLANES, LANES), lambda b, i: (0, 0)),
            pl.BlockSpec((LANES, LANES), lambda b, i: (0, 0)),
            pl.BlockSpec((3, tm, LANES), lambda b, i: (0, i, 0)),
            pl.BlockSpec((3, tm, LANES), lambda b, i: (0, i, 0)),
        ],
        out_specs=(
            pl.BlockSpec((1, tm, COL_GQ), lambda b, i: (b, i, 0)),
            pl.BlockSpec((1, GQA_HEADS, tm, GQA_HEAD_DIM), lambda b, i: (b, 0, i, 0)),
            pl.BlockSpec((1, GQA_KV_HEADS, tm, GQA_HEAD_DIM), lambda b, i: (b, 0, i, 0)),
            pl.BlockSpec((1, GQA_KV_HEADS, 1, VT_ROWS, tm), lambda b, i: (b, 0, i, 0, 0)),
            pl.BlockSpec((1, 2 * DIFF_HEADS, tm, DIFF_QK_DIM), lambda b, i: (b, 0, i, 0)),
            pl.BlockSpec((1, 2 * DIFF_HEADS, tm, DIFF_QK_DIM), lambda b, i: (b, 0, i, 0)),
            pl.BlockSpec((1, DIFF_HEADS, 1, VT_ROWS, tm), lambda b, i: (b, 0, i, 0, 0)),
        ),
        compiler_params=_params(("arbitrary", "arbitrary")),
        name="in_proj",
    )(xc, mod, w_in, gains, seg, rope_g, rope_d)


def _expm1(x):
    u = jnp.exp(x)
    return jnp.where(u == 1.0, x, (u - 1.0) * x / jnp.log(u))


def _lru_kernel(ax_ref, prev_ref, next_ref, cw_ref, cb_ref, w_ref, b_ref, lam_ref, h_ref, carry_ref, *, nt):
    tm = ax_ref.shape[1]
    d = pl.program_id(1)
    j = pl.program_id(2)
    blk = jnp.where(d == 0, j, jnp.where(j == 0, 0, nt - j))

    @pl.when(j == 0)
    def _():
        carry_ref[...] = jnp.zeros_like(carry_ref)

    x = ax_ref[0]
    no_left = jnp.logical_or(blk == 0, blk == 1)
    no_right = jnp.logical_or(blk == 0, blk == nt - 1)
    left = jnp.where(no_left, 0.0, prev_ref[0, SUBLANES - 1:SUBLANES, :])
    right0 = jnp.where(no_right, 0.0, next_ref[0, 0:1, :])
    right1 = jnp.where(no_right, 0.0, next_ref[0, 1:2, :])
    row = lax.broadcasted_iota(I32, (tm, 1), 0)
    x_m1 = jnp.where(row == 0, left, pltpu.roll(x, 1, 0))
    x_p1 = jnp.where(row == tm - 1, right0, pltpu.roll(x, tm - 1, 0))
    x_p2 = jnp.where(row == tm - 2, right0, jnp.where(row == tm - 1, right1, pltpu.roll(x, tm - 2, 0)))
    u = cb_ref[...] + x_m1 * cw_ref[0:1, :] + x * cw_ref[1:2, :] + x_p1 * cw_ref[2:3, :] + x_p2 * cw_ref[3:4, :]

    g = jnp.dot(u.astype(BF16), w_ref[0], preferred_element_type=F32) + b_ref[0]
    r = jax.nn.sigmoid(g[:, :LRU_WIDTH])
    gate_in = jax.nn.sigmoid(g[:, LRU_WIDTH:])
    neg_lam = -lam_ref[0]
    softplus = jnp.maximum(neg_lam, 0.0) + jnp.log1p(jnp.exp(-jnp.abs(neg_lam)))
    log_a = (-RG_C * r) * softplus
    a = jnp.exp(log_a)
    b = jnp.sqrt(-_expm1(2.0 * log_a)) * (gate_in * u)

    def scan(reverse):
        aa, bb = a, b
        s = 1
        while s < tm:
            shift = tm - s if reverse else s
            keep = (row < tm - s) if reverse else (row >= s)
            a_sh = jnp.where(keep, pltpu.roll(aa, shift, 0), 1.0)
            b_sh = jnp.where(keep, pltpu.roll(bb, shift, 0), 0.0)
            bb = aa * b_sh + bb
            aa = aa * a_sh
            s *= 2
        h = aa * carry_ref[...] + bb
        h_ref[0, 0] = h
        carry_ref[...] = h[0:1, :] if reverse else h[tm - 1:tm, :]

    @pl.when(d == 0)
    def _():
        scan(False)

    @pl.when(d == 1)
    def _():
        scan(True)


def _lru(zl, conv_w, conv_b, w_gates, b_gates, lam):
    bsz, sk, _ = zl.shape
    tm = TOKEN_TILE
    nt = sk // tm
    per = tm // SUBLANES

    def blk_of(d, j):
        return jnp.where(d == 0, j, jnp.where(j == 0, 0, nt - j))

    return pl.pallas_call(
        functools.partial(_lru_kernel, nt=nt),
        out_shape=jax.ShapeDtypeStruct((2, bsz, sk, LRU_WIDTH), F32),
        grid=(bsz, 2, nt),
        in_specs=[
            pl.BlockSpec((1, tm, LRU_WIDTH), lambda b, d, j: (b, blk_of(d, j), 0)),
            pl.BlockSpec((1, SUBLANES, LRU_WIDTH), lambda b, d, j: (b, jnp.maximum(blk_of(d, j) * per - 1, 0), 0)),
            pl.BlockSpec((1, SUBLANES, LRU_WIDTH),
                         lambda b, d, j: (b, jnp.minimum((blk_of(d, j) + 1) * per, nt * per - 1), 0)),
            pl.BlockSpec((CONV_W, LRU_WIDTH), lambda b, d, j: (0, 0)),
            pl.BlockSpec((1, LRU_WIDTH), lambda b, d, j: (0, 0)),
            pl.BlockSpec((1, LRU_WIDTH, 2 * LRU_WIDTH), lambda b, d, j: (d, 0, 0)),
            pl.BlockSpec((1, 1, 2 * LRU_WIDTH), lambda b, d, j: (d, 0, 0)),
            pl.BlockSpec((1, 1, LRU_WIDTH), lambda b, d, j: (d, 0, 0)),
        ],
        out_specs=pl.BlockSpec((1, 1, tm, LRU_WIDTH), lambda b, d, j: (d, b, blk_of(d, j), 0)),
        scratch_shapes=[pltpu.VMEM((1, LRU_WIDTH), F32)],
        compiler_params=_params(("arbitrary", "arbitrary", "arbitrary")),
        name="rg_lru",
    )(zl, zl, zl, conv_w, conv_b, w_gates, b_gates, lam)


def _attn_kernel(q_ref, k_ref, vt_ref, aux_ref, o_ref, m_ref, acc_ref, *, k_heads, v_heads, nt, n_ctx_tiles, diff):
    tq = q_ref.shape[2]
    tk = tq
    dv = GQA_HEAD_DIM
    qi = pl.program_id(2)
    n_k = jnp.where(qi < n_ctx_tiles, n_ctx_tiles, nt)

    m_ref[...] = jnp.full_like(m_ref, -jnp.inf)
    acc_ref[...] = jnp.zeros_like(acc_ref)

    def k_step(kt, carry):
        start = pl.multiple_of(kt * tk, tk)
        for j in range(4):
            k = k_ref[0, k_heads[j], pl.ds(start, tk), :]
            s = lax.dot_general(k, q_ref[0, j], NT_DIMS, preferred_element_type=F32)
            m_old = m_ref[j]
            m_new = jnp.maximum(m_old, jnp.max(s, axis=0, keepdims=True))
            p = jnp.exp(s - m_new).astype(BF16)
            pv = jnp.dot(vt_ref[0, v_heads[j], kt], p, preferred_element_type=F32)
            acc_ref[j] = acc_ref[j] * jnp.exp(m_old - m_new) + pv
            m_ref[j] = m_new
        return carry

    lax.fori_loop(0, n_k, k_step, 0)

    outs = []
    for j in range(4):
        acc = acc_ref[j]
        outs.append(acc[0:dv] / acc[dv:dv + 1])
    if diff:
        dl = aux_ref[0:4, :]
        lam_init = aux_ref[4:5, 0:1]
        lam = (jnp.exp(jnp.sum(dl[0:1] * dl[1:2], axis=1, keepdims=True))
               - jnp.exp(jnp.sum(dl[2:3] * dl[3:4], axis=1, keepdims=True)) + lam_init)
        heads = []
        for hd in range(2):
            o = outs[2 * hd] - lam * outs[2 * hd + 1]
            heads.append(o * lax.rsqrt(jnp.mean(o * o, axis=0, keepdims=True) + EPS))
        o_ref[0] = ((jnp.concatenate(heads, axis=0).T * aux_ref[5:6, :]) * (1.0 - lam_init)).astype(o_ref.dtype)
    else:
        for pair in range(2):
            o_ref[0, :, pair * 2 * dv:(pair + 1) * 2 * dv] = (
                jnp.concatenate(outs[2 * pair:2 * pair + 2], axis=0).T.astype(o_ref.dtype))


def _attention(q, k, vt, aux, *, diff, n_ctx_tiles):
    bsz, n_q, sk, dh = q.shape
    n_k_heads = k.shape[1]
    n_v_heads = vt.shape[1]
    tq = TOKEN_TILE
    nt = sk // tq
    groups = n_q // 4
    kpg = n_k_heads // groups
    vpg = n_v_heads // groups
    k_heads = tuple(j * kpg // 4 for j in range(4))
    v_heads = tuple(j * vpg // 4 for j in range(4))
    width = 2 * DIFF_V_DIM if diff else 4 * GQA_HEAD_DIM
    return pl.pallas_call(
        functools.partial(_attn_kernel, k_heads=k_heads, v_heads=v_heads, nt=nt, n_ctx_tiles=n_ctx_tiles, diff=diff),
        out_shape=jax.ShapeDtypeStruct((bsz, sk, groups * width), BF16),
        grid=(bsz, groups, nt),
        in_specs=[
            pl.BlockSpec((1, 4, tq, dh), lambda b, g, i: (b, g, i, 0)),
            pl.BlockSpec((1, kpg, sk, dh), lambda b, g, i: (b, g, 0, 0)),
            pl.BlockSpec((1, vpg, nt, VT_ROWS, tq), lambda b, g, i: (b, g, 0, 0, 0)),
            pl.BlockSpec((SUBLANES, LANES), lambda b, g, i: (0, 0)),
        ],
        out_specs=pl.BlockSpec((1, tq, width), lambda b, g, i: (b, i, g)),
        scratch_shapes=[pltpu.VMEM((4, 1, tq), F32), pltpu.VMEM((4, VT_ROWS, tq), F32)],
        compiler_params=_params(("arbitrary", "arbitrary", "arbitrary")),
        name="diff_attention" if diff else "gqa_attention",
    )(q, k, vt, aux)


def _layer_norm(r, g, b):
    mu = jnp.mean(r, axis=-1, keepdims=True)
    c = r - mu
    var = jnp.mean(c * c, axis=-1, keepdims=True)
    return c * lax.rsqrt(var + EPS) * g + b


def _route(logits, bias):
    mx = jnp.max(logits, axis=0, keepdims=True)
    e = jnp.exp(logits - mx)
    probs = e / jnp.sum(e, axis=0, keepdims=True)
    sel = probs + bias
    epg = EXPERTS_PER_GROUP
    in_top2, scores = [], []
    for g in range(N_GROUPS):
        v = [sel[g * epg + i:g * epg + i + 1] for i in range(epg)]
        masks = []
        for i in range(epg):
            rank = jnp.zeros_like(v[i])
            for j in range(epg):
                if j != i:
                    beats = (v[j] > v[i]) if j > i else (v[j] >= v[i])
                    rank = rank + beats.astype(F32)
            masks.append(rank < 2.0)
        in_top2.append(masks)
        scores.append(sum(jnp.where(masks[i], v[i], 0.0) for i in range(epg)))
    chosen = []
    for g in range(N_GROUPS):
        c = None
        for j in range(N_GROUPS):
            if j != g:
                t = (scores[j] < scores[g]) if j < g else (scores[j] <= scores[g])
                c = t if c is None else jnp.logical_and(c, t)
        chosen.append(c)
    picked, weight = [], []
    for i in range(epg):
        m = None
        w = jnp.zeros_like(scores[0])
        for g in range(N_GROUPS):
            t = jnp.logical_and(chosen[g], in_top2[g][i])
            m = t if m is None else jnp.logical_or(m, t)
            w = w + jnp.where(t, probs[g * epg + i:g * epg + i + 1], 0.0)
        picked.append(m)
        weight.append(w)
    total = weight[0] + weight[1] + weight[2] + weight[3]
    gate = [w / total for w in weight]
    lo = jnp.where(picked[0], 0.0, jnp.where(picked[1], 1.0, 2.0))
    hi = jnp.where(picked[3], 3.0, jnp.where(picked[2], 2.0, 1.0))
    gate_lo = jnp.where(picked[0], gate[0], jnp.where(picked[1], gate[1], gate[2]))
    gate_hi = jnp.where(picked[3], gate[3], jnp.where(picked[2], gate[2], gate[1]))
    group = sum(jnp.where(chosen[g], float(g), 0.0) for g in range(N_GROUPS))
    base = jnp.where(lo == 0.0, 0.0, jnp.where(lo == 1.0, 3.0, 5.0))
    cls = group * float(len(PAIRS_LO)) + base + hi - lo - 1.0
    return cls.astype(I32), gate_lo, gate_hi


def _out_proj_kernel(hf_ref, hb_ref, ag_ref, yb_ref, yc_ref, x_ref, mod_ref, w_ref, lng_ref, lnb_ref,
                     rw_ref, rb_ref, x1_ref, ri_ref, rf_ref, *, alpha):
    tm = x_ref.shape[1]
    ya = (jax.nn.gelu(ag_ref[0]) * (hf_ref[0, 0] + hb_ref[0, 0])).astype(BF16)
    n_a = LRU_WIDTH
    n_b = n_a + GQA_HEADS * GQA_HEAD_DIM
    y = (jnp.dot(ya, w_ref[0:n_a, :], preferred_element_type=F32)
         + jnp.dot(yb_ref[0], w_ref[n_a:n_b, :], preferred_element_type=F32)
         + jnp.dot(yc_ref[0], w_ref[n_b:, :], preferred_element_type=F32))
    x1 = _layer_norm(alpha * x_ref[0] + mod_ref[0, 2:3, :] * y, lng_ref[...], lnb_ref[...])
    x1_ref[0] = x1

    h2 = x1 * (1.0 + mod_ref[0, 4:5, :]) + mod_ref[0, 3:4, :]
    h_hi = h2.astype(BF16)
    h_lo = (h2 - h_hi.astype(F32)).astype(BF16)
    both = lax.dot_general(rw_ref[...], h_hi, NT_DIMS, preferred_element_type=F32)
    logits = (both[0:N_EXPERTS] + both[N_EXPERTS:]
              + lax.dot_general(rw_ref[0:N_EXPERTS, :], h_lo, NT_DIMS, preferred_element_type=F32))
    cls, gate_lo, gate_hi = _route(logits, rb_ref[...])
    ri_ref[0] = jnp.broadcast_to(cls, (SUBLANES, tm))
    rf_ref[0] = jnp.concatenate([gate_lo, gate_hi, jnp.zeros((SUBLANES - 2, tm), F32)], axis=0)


def _out_proj(h_lru, zl, yb, yc, xc, mod, w_out, ln_g, ln_b, rw_split, rb, alpha, n_ctx_tiles):
    bsz, sk, d = xc.shape
    tm = TOKEN_TILE
    nt = sk // tm

    def mod_map(b, i):
        return (jnp.where(i < n_ctx_tiles, bsz, b), 0, 0)

    return pl.pallas_call(
        functools.partial(_out_proj_kernel, alpha=alpha),
        out_shape=(
            jax.ShapeDtypeStruct((bsz, sk, d), F32),
            jax.ShapeDtypeStruct((bsz, SUBLANES, sk), I32),
            jax.ShapeDtypeStruct((bsz, SUBLANES, sk), F32),
        ),
        grid=(bsz, nt),
        in_specs=[
            pl.BlockSpec((1, 1, tm, LRU_WIDTH), lambda b, i: (0, b, i, 0)),
            pl.BlockSpec((1, 1, tm, LRU_WIDTH), lambda b, i: (1, b, i, 0)),
            pl.BlockSpec((1, tm, LRU_WIDTH), lambda b, i: (b, i, 1)),
            pl.BlockSpec((1, tm, GQA_HEADS * GQA_HEAD_DIM), lambda b, i: (b, i, 0)),
            pl.BlockSpec((1, tm, DIFF_HEADS * DIFF_V_DIM), lambda b, i: (b, i, 0)),
            pl.BlockSpec((1, tm, d), lambda b, i: (b, i, 0)),
            pl.BlockSpec((1, N_MOD, d), mod_map),
            pl.BlockSpec((d, d), lambda b, i: (0, 0)),
            pl.BlockSpec((1, d), lambda b, i: (0, 0)),
            pl.BlockSpec((1, d), lambda b, i: (0, 0)),
            pl.BlockSpec((2 * N_EXPERTS, d), lambda b, i: (0, 0)),
            pl.BlockSpec((N_EXPERTS, 1), lambda b, i: (0, 0)),
        ],
        out_specs=(
            pl.BlockSpec((1, tm, d), lambda b, i: (b, i, 0)),
            pl.BlockSpec((1, SUBLANES, tm), lambda b, i: (b, 0, i)),
            pl.BlockSpec((1, SUBLANES, tm), lambda b, i: (b, 0, i)),
        ),
        compiler_params=_params(("arbitrary", "arbitrary")),
        name="out_proj_router",
    )(h_lru, h_lru, zl, yb, yc, xc, mod, w_out, ln_g, ln_b, rw_split, rb)


def _moe_kernel(tok_ref, ea_ref, eb_ref, nv_ref,
                x_hbm, gates_ref, mid_ref, mod_ref, lng_ref, lnb_ref,
                wga_ref, wua_ref, wda_ref, wgb_ref, wub_ref, wdb_ref,
                o_hbm, xbuf, obuf, gsem, ssem, *, alpha, n_batch):
    rows = xbuf.shape[1]
    i = pl.program_id(0)
    n_steps = pl.num_programs(0)
    slot = i % 2

    def gather_copy(row_src, slot_, r):
        return pltpu.make_async_copy(x_hbm.at[pl.ds(row_src, 1)], xbuf.at[slot_, pl.ds(r, 1)], gsem.at[slot_])

    def scatter_copy(row_dst, slot_, r):
        return pltpu.make_async_copy(obuf.at[slot_, pl.ds(r, 1)], o_hbm.at[pl.ds(row_dst, 1)], ssem.at[slot_])

    def start_gather(step, slot_):
        def body(r, c):
            gather_copy(tok_ref[step * rows + r], slot_, r).start()
            return c
        lax.fori_loop(0, nv_ref[step], body, 0)

    def wait_gather(step, slot_):
        def body(r, c):
            gather_copy(0, slot_, 0).wait()
            return c
        lax.fori_loop(0, nv_ref[step], body, 0)

    def start_scatter(step, slot_):
        def body(r, c):
            scatter_copy(tok_ref[step * rows + r], slot_, r).start()
            return c
        lax.fori_loop(0, nv_ref[step], body, 0)

    def wait_scatter(step, slot_):
        def body(r, c):
            scatter_copy(0, slot_, 0).wait()
            return c
        lax.fori_loop(0, nv_ref[step], body, 0)

    @pl.when(i == 0)
    def _():
        xbuf[...] = jnp.zeros_like(xbuf)
        start_gather(0, 0)

    wait_gather(i, slot)

    @pl.when(i + 1 < n_steps)
    def _():
        start_gather(i + 1, 1 - slot)

    @pl.when(i >= 2)
    def _():
        wait_scatter(i - 2, slot)

    @pl.when(nv_ref[i] > 0)
    def _():
        x = xbuf[slot]
        mid = mid_ref[...]

        def pick(kk):
            v = mod_ref[n_batch, kk:kk + 1, :]
            for b in range(n_batch):
                v = jnp.where(mid == b, mod_ref[b, kk:kk + 1, :], v)
            return v

        h = (x * (1.0 + pick(4)) + pick(3)).astype(BF16)

        def ffn(wg, wu, wd):
            a = jnp.dot(h, wg[0], preferred_element_type=F32)
            u = jnp.dot(h, wu[0], preferred_element_type=F32)
            return jnp.dot(((a * jax.nn.sigmoid(a)) * u).astype(BF16), wd[0], preferred_element_type=F32)

        f = ffn(wga_ref, wua_ref, wda_ref) * gates_ref[:, 0:1] + ffn(wgb_ref, wub_ref, wdb_ref) * gates_ref[:, 1:2]
        obuf[slot] = _layer_norm(alpha * x + pick(5) * f, lng_ref[...], lnb_ref[...])

    start_scatter(i, slot)

    @pl.when(i == n_steps - 1)
    def _():
        @pl.when(i >= 1)
        def _():
            wait_scatter(i - 1, 1 - slot)
        wait_scatter(i, slot)


def _moe(x_flat, row_tok, blk_ea, blk_eb, blk_nv, row_gates, row_mid, mod, ln_g, ln_b, w_gate, w_up, w_down,
         alpha, n_batch):
    t, d = x_flat.shape
    rows = MOE_ROWS
    n_blk = blk_ea.shape[0]
    ff = w_gate.shape[-1]

    def wa(i, tok, ea, eb, nv):
        return (ea[i], 0, 0)

    def wb(i, tok, ea, eb, nv):
        return (eb[i], 0, 0)

    const2 = lambda i, tok, ea, eb, nv: (0, 0)
    grid_spec = pltpu.PrefetchScalarGridSpec(
        num_scalar_prefetch=4,
        grid=(n_blk,),
        in_specs=[
            pl.BlockSpec(memory_space=pl.ANY),
            pl.BlockSpec((rows, 2), lambda i, tok, ea, eb, nv: (i, 0)),
            pl.BlockSpec((rows, 1), lambda i, tok, ea, eb, nv: (i, 0)),
            pl.BlockSpec(mod.shape, lambda i, tok, ea, eb, nv: (0, 0, 0)),
            pl.BlockSpec((1, d), const2),
            pl.BlockSpec((1, d), const2),
            pl.BlockSpec((1, d, ff), wa),
            pl.BlockSpec((1, d, ff), wa),
            pl.BlockSpec((1, ff, d), wa),
            pl.BlockSpec((1, d, ff), wb),
            pl.BlockSpec((1, d, ff), wb),
            pl.BlockSpec((1, ff, d), wb),
        ],
        out_specs=pl.BlockSpec(memory_space=pl.ANY),
        scratch_shapes=[
            pltpu.VMEM((2, rows, d), F32),
            pltpu.VMEM((2, rows, d), F32),
            pltpu.SemaphoreType.DMA((2,)),
            pltpu.SemaphoreType.DMA((2,)),
        ],
    )
    return pl.pallas_call(
        functools.partial(_moe_kernel, alpha=alpha, n_batch=n_batch),
        out_shape=jax.ShapeDtypeStruct((t, d), F32),
        grid_spec=grid_spec,
        compiler_params=_params(("arbitrary",)),
        name="moe_experts",
    )(row_tok, blk_ea, blk_eb, blk_nv, x_flat, row_gates, row_mid, mod, ln_g, ln_b,
      w_gate, w_up, w_down, w_gate, w_up, w_down)


def _dispatch(cls, n_tok, sk, n_ctx, n_batch):
    rows = MOE_ROWS
    n_blk = n_tok // rows + N_CLASSES
    order = jnp.argsort(cls, stable=True).astype(I32)
    sorted_cls = cls[order]
    classes = jnp.arange(N_CLASSES, dtype=I32)
    start = jnp.searchsorted(sorted_cls, classes, side="left").astype(I32)
    counts = jnp.searchsorted(sorted_cls, classes, side="right").astype(I32) - start
    padded = (counts + rows - 1) // rows * rows
    pend = jnp.cumsum(padded)
    pstart = pend - padded
    n_used = pend[-1] // rows
    blk_ids = jnp.minimum(jnp.arange(n_blk, dtype=I32), jnp.maximum(n_used - 1, 0))
    blk_cls = jnp.clip(jnp.searchsorted(pend, blk_ids * rows, side="right"), 0, N_CLASSES - 1).astype(I32)
    used = jnp.arange(n_blk, dtype=I32) < n_used
    within = blk_ids * rows - pstart[blk_cls]
    blk_nv = jnp.where(used, jnp.clip(counts[blk_cls] - within, 0, rows), 0).astype(I32)
    lo = jnp.asarray(PAIRS_LO, I32)[blk_cls % len(PAIRS_LO)]
    hi = jnp.asarray(PAIRS_HI, I32)[blk_cls % len(PAIRS_LO)]
    grp = blk_cls // len(PAIRS_LO)
    blk_ea = grp * EXPERTS_PER_GROUP + lo
    blk_eb = grp * EXPERTS_PER_GROUP + hi
    r = jnp.arange(n_blk * rows, dtype=I32)
    rb = r // rows
    pos = within[rb] + r % rows
    valid = jnp.logical_and(used[rb], pos < counts[blk_cls[rb]])
    row_tok = jnp.where(valid, order[jnp.clip(start[blk_cls[rb]] + pos, 0, n_tok - 1)], 0).astype(I32)
    row_mid = jnp.where(row_tok % sk < n_ctx, n_batch, row_tok // sk).astype(I32)
    return row_tok, blk_ea, blk_eb, blk_nv, row_mid


def _rope_tables(n_lat, n_ctx, dim):
    rows = n_lat // GRID_W
    row = jnp.repeat(jnp.arange(rows, dtype=F32), GRID_W)
    col = jnp.tile(jnp.arange(GRID_W, dtype=F32), rows)
    n_freq = dim // 4
    inv = ROPE_THETA ** (-jnp.arange(n_freq, dtype=F32) / n_freq)
    ang = jnp.concatenate([row[:, None] * inv, col[:, None] * inv], axis=-1)
    cos = jnp.repeat(jnp.cos(ang), 2, axis=-1)
    sin = jnp.repeat(jnp.sin(ang), 2, axis=-1)
    even = (jnp.arange(dim) % 2 == 0)
    tabs = jnp.stack([cos, jnp.where(even, -sin, 0.0), jnp.where(even, 0.0, sin)])
    ident = jnp.stack([jnp.ones((n_ctx, dim), F32), jnp.zeros((n_ctx, dim), F32), jnp.zeros((n_ctx, dim), F32)])
    return jnp.tile(jnp.concatenate([ident, tabs], axis=1), (1, 1, LANES // dim))


def _block_diag(w):
    n, c, _ = w.shape
    eye = jnp.eye(n, dtype=w.dtype)
    return (eye[:, None, :, None] * w[:, :, None, :]).reshape(n * c, n * c)


def kernel(x, c, ctx, c_ctx, ada_w, ada_b, w_in, conv_w, conv_b, rg_wa, rg_ba, rg_wx, rg_bx, rg_lam, q_norm_g, k_norm_g, diff_lambda, diff_subln_g, w_out, ln1_g, ln1_b, ln2_g, ln2_b, router_w, router_b, exp_w_gate, exp_w_up, exp_w_down):
    bsz, s, d = x.shape
    n_ctx = ctx.shape[1]
    depth = w_in.shape[0]
    sk = n_ctx + s
    assert d == D_MODEL and n_ctx == TOKEN_TILE and s % TOKEN_TILE == 0 and bsz + 1 <= SUBLANES
    n_ctx_tiles = n_ctx // TOKEN_TILE
    n_tok = bsz * sk
    alpha = (2.0 * depth) ** 0.25

    c_rows = jnp.concatenate([c, c_ctx[None, :], jnp.zeros((SUBLANES - bsz - 1, d), F32)], axis=0)
    mods = _ada_modulation(c_rows, ada_w, ada_b).reshape(depth, SUBLANES, N_MOD, d)

    rope_g = _rope_tables(s, n_ctx, GQA_HEAD_DIM)
    rope_d = _rope_tables(s, n_ctx, DIFF_QK_DIM)
    lane = np.arange(LANES)
    seg = jnp.asarray(lane[:, None] // GQA_HEAD_DIM == lane[None, :] // GQA_HEAD_DIM, BF16)
    rw_t = router_w.T
    rw_hi = rw_t.astype(BF16)
    rw_split = jnp.concatenate([rw_hi, (rw_t - rw_hi.astype(F32)).astype(BF16)], axis=0)
    rb = router_b.reshape(N_EXPERTS, 1)
    zero_aux = jnp.zeros((SUBLANES, LANES), F32)

    xc = jnp.concatenate([ctx, x], axis=1)
    for li in range(depth):
        lam_init = 0.8 - 0.6 * math.exp(-0.3 * li)
        mod = mods[li]
        gains = jnp.concatenate([jnp.tile(jnp.tile(q_norm_g[li], 2)[None, :], (4, 1)),
                                 jnp.tile(k_norm_g[li], 2)[None, :], jnp.zeros((3, LANES), F32)], axis=0)
        zl, q, k, vt, dq, dk, dvt = _in_proj(xc, mod, w_in[li].astype(BF16), gains, seg, rope_g, rope_d, n_ctx_tiles)

        w_gates = jnp.stack([jnp.concatenate([_block_diag(rg_wa[li, dd]), _block_diag(rg_wx[li, dd])], axis=1)
                             for dd in range(2)]).astype(BF16)
        b_gates = jnp.concatenate([rg_ba[li], rg_bx[li]], axis=-1)[:, None, :]
        h_lru = _lru(zl, conv_w[li], conv_b[li][None, :], w_gates, b_gates, rg_lam[li][:, None, :])

        yb = _attention(q, k, vt, zero_aux, diff=False, n_ctx_tiles=n_ctx_tiles)
        aux = jnp.concatenate([
            jnp.pad(diff_lambda[li], ((0, 0), (0, LANES - DIFF_QK_DIM))),
            jnp.full((1, LANES), lam_init, F32),
            jnp.tile(diff_subln_g[li], 2)[None, :],
            jnp.zeros((2, LANES), F32)], axis=0)
        yc = _attention(dq, dk, dvt, aux, diff=True, n_ctx_tiles=n_ctx_tiles)

        x1, route_i, route_f = _out_proj(h_lru, zl, yb, yc, xc, mod, w_out[li].astype(BF16), ln1_g[li][None, :],
                                         ln1_b[li][None, :], rw_split, rb, alpha, n_ctx_tiles)

        cls = route_i[:, 0, :].reshape(n_tok)
        row_tok, blk_ea, blk_eb, blk_nv, row_mid = _dispatch(cls, n_tok, sk, n_ctx, bsz)
        gates = jnp.stack([route_f[:, 0, :].reshape(n_tok), route_f[:, 1, :].reshape(n_tok)], axis=-1)
        xc = _moe(x1.reshape(n_tok, d), row_tok, blk_ea, blk_eb, blk_nv, gates[row_tok], row_mid[:, None], mod,
                  ln2_g[li][None, :], ln2_b[li][None, :], exp_w_gate[li].astype(BF16), exp_w_up[li].astype(BF16),
                  exp_w_down[li].astype(BF16), alpha, bsz).reshape(bsz, sk, d)
    return xc[:, n_ctx:, :]
```

```python
import functools
import math

import numpy as np
import jax
import jax.numpy as jnp
from jax import lax
from jax.experimental import pallas as pl
from jax.experimental.pallas import tpu as pltpu

F32 = jnp.float32
BF16 = jnp.bfloat16
I32 = jnp.int32

D_MODEL = 1024
GRID_W = 64
LRU_WIDTH = 256
LRU_BLOCKS = 4
CONV_W = 4
RG_C = 8.0
GQA_HEADS = 8
GQA_KV_HEADS = 2
GQA_HEAD_DIM = 64
DIFF_HEADS = 4
DIFF_QK_DIM = 32
DIFF_V_DIM = 64
IN_WIDTH = 2048
N_EXPERTS = 16
N_GROUPS = 4
EXPERTS_PER_GROUP = 4
EXPERT_FF = 512
ROPE_THETA = 10000.0
EPS = 1e-6
N_MOD = 6
LOG2_E = math.log2(math.e)

COL_AX, COL_AG, COL_GQ, COL_GK, COL_GV, COL_DQ, COL_DK, COL_DV = 0, 256, 512, 1024, 1152, 1280, 1536, 1792

LANES = 128
SUBLANES = 8
VMEM_LIMIT = 56 * 1024 * 1024

TOKEN_TILE = 256
VT_ROWS = 80
PAIRS_LO = (0, 0, 0, 1, 1, 2)
PAIRS_HI = (1, 2, 3, 2, 3, 3)
N_CLASSES = N_GROUPS * len(PAIRS_LO)
MOE_ROWS = 128

NT_DIMS = (((1,), (1,)), ((), ()))


def _params(semantics):
    return pltpu.CompilerParams(dimension_semantics=semantics, vmem_limit_bytes=VMEM_LIMIT)


def _ada_kernel(c_ref, w_ref, b_ref, o_ref):
    c = c_ref[...]
    o_ref[0] = jnp.dot(c * jax.nn.sigmoid(c), w_ref[0], preferred_element_type=F32) + b_ref[0]


def _ada_modulation(c_rows, ada_w, ada_b):
    depth, d, n = ada_w.shape
    tn = 1536
    return pl.pallas_call(
        _ada_kernel,
        out_shape=jax.ShapeDtypeStruct((depth, SUBLANES, n), F32),
        grid=(depth, n // tn),
        in_specs=[
            pl.BlockSpec((SUBLANES, d), lambda l, j: (0, 0)),
            pl.BlockSpec((1, d, tn), lambda l, j: (l, 0, j)),
            pl.BlockSpec((1, 1, tn), lambda l, j: (l, 0, j)),
        ],
        out_specs=pl.BlockSpec((1, SUBLANES, tn), lambda l, j: (l, 0, j)),
        compiler_params=_params(("arbitrary", "arbitrary")),
        name="ada_modulation",
    )(c_rows, ada_w, ada_b.reshape(depth, 1, n))


def _rotate_pairs(v, cos, sin_next, sin_prev):
    return v * cos + pltpu.roll(v, LANES - 1, 1) * sin_next + pltpu.roll(v, 1, 1) * sin_prev


def _in_proj_kernel(x_ref, mod_ref, w_ref, gain_ref, seg_ref, rope_g_ref, rope_d_ref,
                    zl_ref, q_ref, k_ref, vt_ref, dq_ref, dk_ref, dvt_ref):
    tm = x_ref.shape[0]
    x = x_ref[...]
    shift = mod_ref[0, 0:1, :]
    scale = mod_ref[0, 1:2, :]
    h = (x * (1.0 + scale) + shift).astype(BF16)
    z = jnp.dot(h, w_ref[...], preferred_element_type=F32)
    zl_ref[0] = z[:, COL_AX:COL_GQ]

    ones_rows = jnp.where(lax.broadcasted_iota(I32, (VT_ROWS - GQA_HEAD_DIM, tm), 0) == 0, 1.0, 0.0).astype(BF16)

    seg = seg_ref[...]
    cos, s_next, s_prev = rope_g_ref[0], rope_g_ref[1], rope_g_ref[2]
    for g in range(5):
        v = z[:, COL_GQ + g * LANES:COL_GQ + (g + 1) * LANES]
        sq = v * v
        sq_hi = sq.astype(BF16)
        sq_lo = (sq - sq_hi.astype(F32)).astype(BF16)
        ssq = jnp.dot(sq_hi, seg, preferred_element_type=F32) + jnp.dot(sq_lo, seg, preferred_element_type=F32)
        v = v * lax.rsqrt(ssq * (1.0 / GQA_HEAD_DIM) + EPS) * gain_ref[g:g + 1, :]
        v = _rotate_pairs(v, cos, s_next, s_prev)
        if g < 4:
            v = (v * (GQA_HEAD_DIM ** -0.5 * LOG2_E)).astype(BF16)
            q_ref[0, 2 * g] = v[:, :GQA_HEAD_DIM]
            q_ref[0, 2 * g + 1] = v[:, GQA_HEAD_DIM:]
        else:
            v = v.astype(BF16)
            k_ref[0, 0] = v[:, :GQA_HEAD_DIM]
            k_ref[0, 1] = v[:, GQA_HEAD_DIM:]
    vt = z[:, COL_GV:COL_DQ].T.astype(BF16)
    for hd in range(GQA_KV_HEADS):
        vt_ref[0, hd, 0, 0:GQA_HEAD_DIM, :] = vt[hd * GQA_HEAD_DIM:(hd + 1) * GQA_HEAD_DIM]
        vt_ref[0, hd, 0, GQA_HEAD_DIM:VT_ROWS, :] = ones_rows

    cos, s_next, s_prev = rope_d_ref[0], rope_d_ref[1], rope_d_ref[2]
    for g in range(4):
        v = _rotate_pairs(z[:, COL_DQ + g * LANES:COL_DQ + (g + 1) * LANES], cos, s_next, s_prev)
        if g < 2:
            v = v * (DIFF_QK_DIM ** -0.5 * LOG2_E)
        v = v.astype(BF16)
        dst = dq_ref if g < 2 else dk_ref
        for j in range(4):
            dst[0, (g % 2) * 4 + j] = v[:, j * DIFF_QK_DIM:(j + 1) * DIFF_QK_DIM]
    dvt = z[:, COL_DV:IN_WIDTH].T.astype(BF16)
    for hd in range(DIFF_HEADS):
        dvt_ref[0, hd, 0, 0:DIFF_V_DIM, :] = dvt[hd * DIFF_V_DIM:(hd + 1) * DIFF_V_DIM]
        dvt_ref[0, hd, 0, DIFF_V_DIM:VT_ROWS, :] = ones_rows


def _in_proj(xc, bsz, sk, mod, w_in, gains, seg, rope_g, rope_d, n_ctx_tiles):
    d = xc.shape[1]
    tm = TOKEN_TILE
    nt = sk // tm
    n_mod_rows = mod.shape[0]

    def mod_map(b, i):
        return (jnp.where(i < n_ctx_tiles, bsz, b), 0, 0)

    assert n_mod_rows > bsz
    return pl.pallas_call(
        _in_proj_kernel,
        out_shape=(
            jax.ShapeDtypeStruct((bsz, sk, COL_GQ), F32),
            jax.ShapeDtypeStruct((bsz, GQA_HEADS, sk, GQA_HEAD_DIM), BF16),
            jax.ShapeDtypeStruct((bsz, GQA_KV_HEADS, sk, GQA_HEAD_DIM), BF16),
            jax.ShapeDtypeStruct((bsz, GQA_KV_HEADS, nt, VT_ROWS, tm), BF16),
            jax.ShapeDtypeStruct((bsz, 2 * DIFF_HEADS, sk, DIFF_QK_DIM), BF16),
            jax.ShapeDtypeStruct((bsz, 2 * DIFF_HEADS, sk, DIFF_QK_DIM), BF16),
            jax.ShapeDtypeStruct((bsz, DIFF_HEADS, nt, VT_ROWS, tm), BF16),
        ),
        grid=(bsz, nt),
        in_specs=[
            pl.BlockSpec((tm, d), lambda b, i: (b * nt + i, 0)),
            pl.BlockSpec((1, N_MOD, d), mod_map),
            pl.BlockSpec((d, IN_WIDTH), lambda b, i: (0, 0)),
            pl.BlockSpec((SUBLANES, LANES), lambda b, i: (0, 0)),
            pl.BlockSpec((LANES, LANES), lambda b, i: (0, 0)),
            pl.BlockSpec((3, tm, LANES), lambda b, i: (0, i, 0)),
            pl.BlockSpec((3, tm, LANES), lambda b, i: (0, i, 0)),
        ],
        out_specs=(
            pl.BlockSpec((1, tm, COL_GQ), lambda b, i: (b, i, 0)),
            pl.BlockSpec((1, GQA_HEADS, tm, GQA_HEAD_DIM), lambda b, i: (b, 0, i, 0)),
            pl.BlockSpec((1, GQA_KV_HEADS, tm, GQA_HEAD_DIM), lambda b, i: (b, 0, i, 0)),
            pl.BlockSpec((1, GQA_KV_HEADS, 1, VT_ROWS, tm), lambda b, i: (b, 0, i, 0, 0)),
            pl.BlockSpec((1, 2 * DIFF_HEADS, tm, DIFF_QK_DIM), lambda b, i: (b, 0, i, 0)),
            pl.BlockSpec((1, 2 * DIFF_HEADS, tm, DIFF_QK_DIM), lambda b, i: (b, 0, i, 0)),
            pl.BlockSpec((1, DIFF_HEADS, 1, VT_ROWS, tm), lambda b, i: (b, 0, i, 0, 0)),
        ),
        compiler_params=_params(("arbitrary", "arbitrary")),
        name="in_proj",
    )(xc, mod, w_in, gains, seg, rope_g, rope_d)


def _expm1(x):
    u = jnp.exp(x)
    return jnp.where(u == 1.0, x, (u - 1.0) * x / jnp.log(u))


def _lru_kernel(ax_ref, prev_ref, next_ref, cw_ref, cb_ref, w_ref, b_ref, lam_ref, h_ref, carry_ref, *, nt):
    tm = ax_ref.shape[1]
    d = pl.program_id(1)
    j = pl.program_id(2)
    blk = jnp.where(d == 0, j, jnp.where(j == 0, 0, nt - j))

    @pl.when(j == 0)
    def _():
        carry_ref[...] = jnp.zeros_like(carry_ref)

    x = ax_ref[0]
    no_left = jnp.logical_or(blk == 0, blk == 1)
    no_right = jnp.logical_or(blk == 0, blk == nt - 1)
    left = jnp.where(no_left, 0.0, prev_ref[0, SUBLANES - 1:SUBLANES, :])
    right0 = jnp.where(no_right, 0.0, next_ref[0, 0:1, :])
    right1 = jnp.where(no_right, 0.0, next_ref[0, 1:2, :])
    row = lax.broadcasted_iota(I32, (tm, 1), 0)
    x_m1 = jnp.where(row == 0, left, pltpu.roll(x, 1, 0))
    x_p1 = jnp.where(row == tm - 1, right0, pltpu.roll(x, tm - 1, 0))
    x_p2 = jnp.where(row == tm - 2, right0, jnp.where(row == tm - 1, right1, pltpu.roll(x, tm - 2, 0)))
    u = cb_ref[...] + x_m1 * cw_ref[0:1, :] + x * cw_ref[1:2, :] + x_p1 * cw_ref[2:3, :] + x_p2 * cw_ref[3:4, :]

    g = jnp.dot(u.astype(BF16), w_ref[0], preferred_element_type=F32) + b_ref[0]
    r = jax.nn.sigmoid(g[:, :LRU_WIDTH])
    gate_in = jax.nn.sigmoid(g[:, LRU_WIDTH:])
    neg_lam = -lam_ref[0]
    softplus = jnp.maximum(neg_lam, 0.0) + jnp.log1p(jnp.exp(-jnp.abs(neg_lam)))
    log_a = (-RG_C * r) * softplus
    a = jnp.exp(log_a)
    b = jnp.sqrt(-_expm1(2.0 * log_a)) * (gate_in * u)

    def scan(reverse):
        aa, bb = a, b
        s = 1
        while s < tm:
            shift = tm - s if reverse else s
            keep = (row < tm - s) if reverse else (row >= s)
            a_sh = jnp.where(keep, pltpu.roll(aa, shift, 0), 1.0)
            b_sh = jnp.where(keep, pltpu.roll(bb, shift, 0), 0.0)
            bb = aa * b_sh + bb
            aa = aa * a_sh
            s *= 2
        h = aa * carry_ref[...] + bb
        h_ref[0, 0] = h
        carry_ref[...] = h[0:1, :] if reverse else h[tm - 1:tm, :]

    @pl.when(d == 0)
    def _():
        scan(False)

    @pl.when(d == 1)
    def _():
        scan(True)


def _lru(zl, conv_w, conv_b, w_gates, b_gates, lam):
    bsz, sk, _ = zl.shape
    tm = TOKEN_TILE
    nt = sk // tm
    per = tm // SUBLANES

    def blk_of(d, j):
        return jnp.where(d == 0, j, jnp.where(j == 0, 0, nt - j))

    return pl.pallas_call(
        functools.partial(_lru_kernel, nt=nt),
        out_shape=jax.ShapeDtypeStruct((2, bsz, sk, LRU_WIDTH), F32),
        grid=(bsz, 2, nt),
        in_specs=[
            pl.BlockSpec((1, tm, LRU_WIDTH), lambda b, d, j: (b, blk_of(d, j), 0)),
            pl.BlockSpec((1, SUBLANES, LRU_WIDTH), lambda b, d, j: (b, jnp.maximum(blk_of(d, j) * per - 1, 0), 0)),
            pl.BlockSpec((1, SUBLANES, LRU_WIDTH),
                         lambda b, d, j: (b, jnp.minimum((blk_of(d, j) + 1) * per, nt * per - 1), 0)),
            pl.BlockSpec((CONV_W, LRU_WIDTH), lambda b, d, j: (0, 0)),
            pl.BlockSpec((1, LRU_WIDTH), lambda b, d, j: (0, 0)),
            pl.BlockSpec((1, LRU_WIDTH, 2 * LRU_WIDTH), lambda b, d, j: (d, 0, 0)),
            pl.BlockSpec((1, 1, 2 * LRU_WIDTH), lambda b, d, j: (d, 0, 0)),
            pl.BlockSpec((1, 1, LRU_WIDTH), lambda b, d, j: (d, 0, 0)),
        ],
        out_specs=pl.BlockSpec((1, 1, tm, LRU_WIDTH), lambda b, d, j: (d, b, blk_of(d, j), 0)),
        scratch_shapes=[pltpu.VMEM((1, LRU_WIDTH), F32)],
        compiler_params=_params(("arbitrary", "arbitrary", "arbitrary")),
        name="rg_lru",
    )(zl, zl, zl, conv_w, conv_b, w_gates, b_gates, lam)


def _attn_kernel(q_ref, k_ref, vt_ref, aux_ref, o_ref, s_ref, m_ref, acc_ref, *, k_heads, v_heads, nt, n_ctx_tiles, diff):
    tq = q_ref.shape[2]
    tk = tq
    dv = GQA_HEAD_DIM
    qi = pl.program_id(2)
    n_k = jnp.where(qi < n_ctx_tiles, n_ctx_tiles, nt)

    def scores(kt):
        start = pl.multiple_of(kt * tk, tk)
        return jnp.concatenate(
            [lax.dot_general(k_ref[0, k_heads[j], pl.ds(start, tk), :], q_ref[0, j], NT_DIMS,
                             preferred_element_type=F32) for j in range(4)], axis=1)

    def weighted_values(kt, p):
        return jnp.concatenate(
            [jnp.dot(vt_ref[0, v_heads[j], kt], p[:, j * tq:(j + 1) * tq], preferred_element_type=F32)
             for j in range(4)], axis=1)

    def col_max(s, m):
        return jnp.maximum(m, jnp.max(s, axis=0, keepdims=True))

    m_ref[...] = jnp.full_like(m_ref, -jnp.inf)
    acc_ref[...] = jnp.zeros_like(acc_ref)
    s_ref[...] = scores(0)

    def pair_step(i, carry):
        kt = 2 * i
        s0 = s_ref[...]
        s1 = scores(kt + 1)
        m0 = m_ref[...]
        m1 = col_max(s0, m0)
        p0 = jnp.exp2(s0 - m1).astype(BF16)
        s_ref[...] = scores(kt + 2)
        m2 = col_max(s1, m1)
        p1 = jnp.exp2(s1 - m2).astype(BF16)
        pv0 = weighted_values(kt, p0)
        pv1 = weighted_values(kt + 1, p1)
        acc_ref[...] = acc_ref[...] * jnp.exp2(m0 - m2) + pv0 * jnp.exp2(m1 - m2) + pv1
        m_ref[...] = m2
        return carry

    lax.fori_loop(0, (n_k - 1) // 2, pair_step, 0)

    s0 = s_ref[...]
    m0 = m_ref[...]
    m1 = col_max(s0, m0)
    p0 = jnp.exp2(s0 - m1).astype(BF16)
    acc_all = acc_ref[...] * jnp.exp2(m0 - m1) + weighted_values(n_k - 1, p0)

    outs = []
    for j in range(4):
        acc = acc_all[:, j * tq:(j + 1) * tq]
        outs.append(acc[0:dv] / acc[dv:dv + 1])
    if diff:
        dl = aux_ref[0:4, :]
        lam_init = aux_ref[4:5, 0:1]
        lam = (jnp.exp(jnp.sum(dl[0:1] * dl[1:2], axis=1, keepdims=True))
               - jnp.exp(jnp.sum(dl[2:3] * dl[3:4], axis=1, keepdims=True)) + lam_init)
        heads = []
        for hd in range(2):
            o = outs[2 * hd] - lam * outs[2 * hd + 1]
            heads.append(o * lax.rsqrt(jnp.mean(o * o, axis=0, keepdims=True) + EPS))
        o_ref[0] = ((jnp.concatenate(heads, axis=0).T * aux_ref[5:6, :]) * (1.0 - lam_init)).astype(o_ref.dtype)
    else:
        for pair in range(2):
            o_ref[0, :, pair * 2 * dv:(pair + 1) * 2 * dv] = (
                jnp.concatenate(outs[2 * pair:2 * pair + 2], axis=0).T.astype(o_ref.dtype))


def _attention(q, k, vt, aux, *, diff, n_ctx_tiles):
    bsz, n_q, sk, dh = q.shape
    n_k_heads = k.shape[1]
    n_v_heads = vt.shape[1]
    tq = TOKEN_TILE
    nt = sk // tq
    assert nt % 2 == 1 and n_ctx_tiles % 2 == 1
    groups = n_q // 4
    kpg = n_k_heads // groups
    vpg = n_v_heads // groups
    k_heads = tuple(j * kpg // 4 for j in range(4))
    v_heads = tuple(j * vpg // 4 for j in range(4))
    width = 2 * DIFF_V_DIM if diff else 4 * GQA_HEAD_DIM
    return pl.pallas_call(
        functools.partial(_attn_kernel, k_heads=k_heads, v_heads=v_heads, nt=nt, n_ctx_tiles=n_ctx_tiles, diff=diff),
        out_shape=jax.ShapeDtypeStruct((bsz, sk, groups * width), BF16),
        grid=(bsz, groups, nt),
        in_specs=[
            pl.BlockSpec((1, 4, tq, dh), lambda b, g, i: (b, g, i, 0)),
            pl.BlockSpec((1, kpg, sk, dh), lambda b, g, i: (b, g, 0, 0)),
            pl.BlockSpec((1, vpg, nt, VT_ROWS, tq), lambda b, g, i: (b, g, 0, 0, 0)),
            pl.BlockSpec((SUBLANES, LANES), lambda b, g, i: (0, 0)),
        ],
        out_specs=pl.BlockSpec((1, tq, width), lambda b, g, i: (b, i, g)),
        scratch_shapes=[pltpu.VMEM((tq, 4 * tq), F32), pltpu.VMEM((1, 4 * tq), F32),
                        pltpu.VMEM((VT_ROWS, 4 * tq), F32)],
        compiler_params=_params(("arbitrary", "arbitrary", "arbitrary")),
        name="diff_attention" if diff else "gqa_attention",
    )(q, k, vt, aux)


def _layer_norm(r, g, b):
    mu = jnp.mean(r, axis=-1, keepdims=True)
    c = r - mu
    var = jnp.mean(c * c, axis=-1, keepdims=True)
    return c * lax.rsqrt(var + EPS) * g + b


def _route(logits, bias):
    mx = jnp.max(logits, axis=0, keepdims=True)
    e = jnp.exp(logits - mx)
    probs = e / jnp.sum(e, axis=0, keepdims=True)
    sel = probs + bias
    epg = EXPERTS_PER_GROUP
    in_top2, scores = [], []
    for g in range(N_GROUPS):
        v = [sel[g * epg + i:g * epg + i + 1] for i in range(epg)]
        masks = []
        for i in range(epg):
            rank = jnp.zeros_like(v[i])
            for j in range(epg):
                if j != i:
                    beats = (v[j] > v[i]) if j > i else (v[j] >= v[i])
                    rank = rank + beats.astype(F32)
            masks.append(rank < 2.0)
        in_top2.append(masks)
        scores.append(sum(jnp.where(masks[i], v[i], 0.0) for i in range(epg)))
    chosen = []
    for g in range(N_GROUPS):
        c = None
        for j in range(N_GROUPS):
            if j != g:
                t = (scores[j] < scores[g]) if j < g else (scores[j] <= scores[g])
                c = t if c is None else jnp.logical_and(c, t)
        chosen.append(c)
    picked, weight = [], []
    for i in range(epg):
        m = None
        w = jnp.zeros_like(scores[0])
        for g in range(N_GROUPS):
            t = jnp.logical_and(chosen[g], in_top2[g][i])
            m = t if m is None else jnp.logical_or(m, t)
            w = w + jnp.where(t, probs[g * epg + i:g * epg + i + 1], 0.0)
        picked.append(m)
        weight.append(w)
    total = weight[0] + weight[1] + weight[2] + weight[3]
    gate = [w / total for w in weight]
    lo = jnp.where(picked[0], 0.0, jnp.where(picked[1], 1.0, 2.0))
    hi = jnp.where(picked[3], 3.0, jnp.where(picked[2], 2.0, 1.0))
    gate_lo = jnp.where(picked[0], gate[0], jnp.where(picked[1], gate[1], gate[2]))
    gate_hi = jnp.where(picked[3], gate[3], jnp.where(picked[2], gate[2], gate[1]))
    group = sum(jnp.where(chosen[g], float(g), 0.0) for g in range(N_GROUPS))
    base = jnp.where(lo == 0.0, 0.0, jnp.where(lo == 1.0, 3.0, 5.0))
    cls = group * float(len(PAIRS_LO)) + base + hi - lo - 1.0
    return cls.astype(I32), gate_lo, gate_hi


def _out_proj_kernel(hf_ref, hb_ref, ag_ref, yb_ref, yc_ref, x_ref, mod_ref, w_ref, lng_ref, lnb_ref,
                     rw_ref, rb_ref, x1_ref, ri_ref, rf_ref, *, alpha):
    tm = x_ref.shape[0]
    ya = (jax.nn.gelu(ag_ref[0]) * (hf_ref[0, 0] + hb_ref[0, 0])).astype(BF16)
    n_a = LRU_WIDTH
    n_b = n_a + GQA_HEADS * GQA_HEAD_DIM
    y = (jnp.dot(ya, w_ref[0:n_a, :], preferred_element_type=F32)
         + jnp.dot(yb_ref[0], w_ref[n_a:n_b, :], preferred_element_type=F32)
         + jnp.dot(yc_ref[0], w_ref[n_b:, :], preferred_element_type=F32))
    x1 = _layer_norm(alpha * x_ref[...] + mod_ref[0, 2:3, :] * y, lng_ref[...], lnb_ref[...])
    x1_ref[...] = x1

    h2 = x1 * (1.0 + mod_ref[0, 4:5, :]) + mod_ref[0, 3:4, :]
    h_hi = h2.astype(BF16)
    h_lo = (h2 - h_hi.astype(F32)).astype(BF16)
    both = lax.dot_general(rw_ref[...], h_hi, NT_DIMS, preferred_element_type=F32)
    logits = (both[0:N_EXPERTS] + both[N_EXPERTS:]
              + lax.dot_general(rw_ref[0:N_EXPERTS, :], h_lo, NT_DIMS, preferred_element_type=F32))
    cls, gate_lo, gate_hi = _route(logits, rb_ref[...])
    ri_ref[0] = jnp.broadcast_to(cls, (SUBLANES, tm))
    rf_ref[0] = jnp.concatenate([gate_lo, gate_hi, jnp.zeros((SUBLANES - 2, tm), F32)], axis=0)


def _out_proj(h_lru, zl, yb, yc, xc, mod, w_out, ln_g, ln_b, rw_split, rb, alpha, n_ctx_tiles):
    bsz, sk, _ = zl.shape
    d = xc.shape[1]
    tm = TOKEN_TILE
    nt = sk // tm

    def mod_map(b, i):
        return (jnp.where(i < n_ctx_tiles, bsz, b), 0, 0)

    return pl.pallas_call(
        functools.partial(_out_proj_kernel, alpha=alpha),
        out_shape=(
            jax.ShapeDtypeStruct((bsz * sk, d), F32),
            jax.ShapeDtypeStruct((bsz, SUBLANES, sk), I32),
            jax.ShapeDtypeStruct((bsz, SUBLANES, sk), F32),
        ),
        grid=(bsz, nt),
        in_specs=[
            pl.BlockSpec((1, 1, tm, LRU_WIDTH), lambda b, i: (0, b, i, 0)),
            pl.BlockSpec((1, 1, tm, LRU_WIDTH), lambda b, i: (1, b, i, 0)),
            pl.BlockSpec((1, tm, LRU_WIDTH), lambda b, i: (b, i, 1)),
            pl.BlockSpec((1, tm, GQA_HEADS * GQA_HEAD_DIM), lambda b, i: (b, i, 0)),
            pl.BlockSpec((1, tm, DIFF_HEADS * DIFF_V_DIM), lambda b, i: (b, i, 0)),
            pl.BlockSpec((tm, d), lambda b, i: (b * nt + i, 0)),
            pl.BlockSpec((1, N_MOD, d), mod_map),
            pl.BlockSpec((d, d), lambda b, i: (0, 0)),
            pl.BlockSpec((1, d), lambda b, i: (0, 0)),
            pl.BlockSpec((1, d), lambda b, i: (0, 0)),
            pl.BlockSpec((2 * N_EXPERTS, d), lambda b, i: (0, 0)),
            pl.BlockSpec((N_EXPERTS, 1), lambda b, i: (0, 0)),
        ],
        out_specs=(
            pl.BlockSpec((tm, d), lambda b, i: (b * nt + i, 0)),
            pl.BlockSpec((1, SUBLANES, tm), lambda b, i: (b, 0, i)),
            pl.BlockSpec((1, SUBLANES, tm), lambda b, i: (b, 0, i)),
        ),
        compiler_params=_params(("arbitrary", "arbitrary")),
        name="out_proj_router",
    )(h_lru, h_lru, zl, yb, yc, xc, mod, w_out, ln_g, ln_b, rw_split, rb)


def _moe_kernel(tok_ref, ea_ref, eb_ref, nv_ref, nu_ref,
                x_hbm, gates_ref, mid_ref, mod_ref, lng_ref, lnb_ref,
                wga_ref, wua_ref, wda_ref, wgb_ref, wub_ref, wdb_ref,
                o_hbm, xbuf, obuf, gsem, ssem, *, alpha, n_batch, n_tok):
    rows = xbuf.shape[1]
    i = pl.program_id(0)
    n_steps = pl.num_programs(0)
    n_used = nu_ref[0]
    slot = i % 2

    def start_gather(step, slot_):
        for r in range(rows):
            pltpu.make_async_copy(x_hbm.at[pl.ds(tok_ref[step * rows + r], 1)], xbuf.at[slot_, pl.ds(r, 1)],
                                  gsem.at[slot_]).start()

    def wait_gather(slot_):
        pltpu.make_async_copy(x_hbm.at[pl.ds(0, rows)], xbuf.at[slot_], gsem.at[slot_]).wait()

    def start_scatter(step, slot_):
        n_valid = nv_ref[step]
        for r in range(rows):
            dst = jnp.where(r < n_valid, tok_ref[step * rows + r], n_tok + slot_ * rows + r)
            pltpu.make_async_copy(obuf.at[slot_, pl.ds(r, 1)], o_hbm.at[pl.ds(dst, 1)], ssem.at[slot_]).start()

    def wait_scatter(slot_):
        pltpu.make_async_copy(obuf.at[slot_], o_hbm.at[pl.ds(0, rows)], ssem.at[slot_]).wait()

    @pl.when(i == 0)
    def _():
        start_gather(0, 0)

    @pl.when(i < n_used)
    def _():
        wait_gather(slot)

        @pl.when(i + 1 < n_used)
        def _():
            start_gather(i + 1, 1 - slot)

        @pl.when(i >= 2)
        def _():
            wait_scatter(slot)

        x = xbuf[slot]
        mid = mid_ref[...]

        def pick(kk):
            v = mod_ref[n_batch, kk:kk + 1, :]
            for b in range(n_batch):
                v = jnp.where(mid == b, mod_ref[b, kk:kk + 1, :], v)
            return v

        h = x * (1.0 + pick(4)) + pick(3)

        def ffn(wg, wu, wd):
            a = jnp.dot(h, wg[0], preferred_element_type=F32)
            u = jnp.dot(h, wu[0], preferred_element_type=F32)
            return jnp.dot((a * jax.nn.sigmoid(a)) * u, wd[0], preferred_element_type=F32)

        f = ffn(wga_ref, wua_ref, wda_ref) * gates_ref[:, 0:1] + ffn(wgb_ref, wub_ref, wdb_ref) * gates_ref[:, 1:2]
        obuf[slot] = _layer_norm(alpha * x + pick(5) * f, lng_ref[...], lnb_ref[...])
        start_scatter(i, slot)

    @pl.when(i == n_steps - 1)
    def _():
        wait_scatter(0)

        @pl.when(n_used >= 2)
        def _():
            wait_scatter(1)


def _moe(x_flat, n_tok, row_tok, blk_ea, blk_eb, blk_nv, n_used, row_gates, row_mid, mod, ln_g, ln_b,
         w_gate, w_up, w_down, layer, alpha, n_batch):
    d = x_flat.shape[1]
    rows = MOE_ROWS
    n_blk = blk_ea.shape[0]
    ff = w_gate.shape[-1]
    first = layer * N_EXPERTS

    def wa(i, tok, ea, eb, nv, nu):
        return (first + ea[i], 0, 0)

    def wb(i, tok, ea, eb, nv, nu):
        return (first + eb[i], 0, 0)

    const2 = lambda i, tok, ea, eb, nv, nu: (0, 0)
    grid_spec = pltpu.PrefetchScalarGridSpec(
        num_scalar_prefetch=5,
        grid=(n_blk,),
        in_specs=[
            pl.BlockSpec(memory_space=pl.ANY),
            pl.BlockSpec((rows, 2), lambda i, tok, ea, eb, nv, nu: (i, 0)),
            pl.BlockSpec((rows, 1), lambda i, tok, ea, eb, nv, nu: (i, 0)),
            pl.BlockSpec(mod.shape, lambda i, tok, ea, eb, nv, nu: (0, 0, 0)),
            pl.BlockSpec((1, d), const2),
            pl.BlockSpec((1, d), const2),
            pl.BlockSpec((1, d, ff), wa),
            pl.BlockSpec((1, d, ff), wa),
            pl.BlockSpec((1, ff, d), wa),
            pl.BlockSpec((1, d, ff), wb),
            pl.BlockSpec((1, d, ff), wb),
            pl.BlockSpec((1, ff, d), wb),
        ],
        out_specs=pl.BlockSpec(memory_space=pl.ANY),
        scratch_shapes=[
            pltpu.VMEM((2, rows, d), F32),
            pltpu.VMEM((2, rows, d), F32),
            pltpu.SemaphoreType.DMA((2,)),
            pltpu.SemaphoreType.DMA((2,)),
        ],
    )
    return pl.pallas_call(
        functools.partial(_moe_kernel, alpha=alpha, n_batch=n_batch, n_tok=n_tok),
        out_shape=jax.ShapeDtypeStruct((n_tok + 2 * rows, d), F32),
        grid_spec=grid_spec,
        compiler_params=_params(("arbitrary",)),
        name="moe_experts",
    )(row_tok, blk_ea, blk_eb, blk_nv, n_used, x_flat, row_gates, row_mid, mod, ln_g, ln_b,
      w_gate, w_up, w_down, w_gate, w_up, w_down)


def _dispatch(cls, n_tok, sk, n_ctx, n_batch):
    rows = MOE_ROWS
    n_blk = n_tok // rows + N_CLASSES
    order = jnp.argsort(cls, stable=True).astype(I32)
    sorted_cls = cls[order]
    classes = jnp.arange(N_CLASSES, dtype=I32)
    start = jnp.searchsorted(sorted_cls, classes, side="left").astype(I32)
    counts = jnp.searchsorted(sorted_cls, classes, side="right").astype(I32) - start
    padded = (counts + rows - 1) // rows * rows
    pend = jnp.cumsum(padded)
    pstart = pend - padded
    n_used = pend[-1] // rows
    blk_ids = jnp.minimum(jnp.arange(n_blk, dtype=I32), jnp.maximum(n_used - 1, 0))
    blk_cls = jnp.clip(jnp.searchsorted(pend, blk_ids * rows, side="right"), 0, N_CLASSES - 1).astype(I32)
    used = jnp.arange(n_blk, dtype=I32) < n_used
    within = blk_ids * rows - pstart[blk_cls]
    blk_nv = jnp.where(used, jnp.clip(counts[blk_cls] - within, 0, rows), 0).astype(I32)
    lo = jnp.asarray(PAIRS_LO, I32)[blk_cls % len(PAIRS_LO)]
    hi = jnp.asarray(PAIRS_HI, I32)[blk_cls % len(PAIRS_LO)]
    grp = blk_cls // len(PAIRS_LO)
    blk_ea = grp * EXPERTS_PER_GROUP + lo
    blk_eb = grp * EXPERTS_PER_GROUP + hi
    r = jnp.arange(n_blk * rows, dtype=I32)
    rb = r // rows
    pos = within[rb] + r % rows
    valid = jnp.logical_and(used[rb], pos < counts[blk_cls[rb]])
    row_tok = jnp.where(valid, order[jnp.clip(start[blk_cls[rb]] + pos, 0, n_tok - 1)], 0).astype(I32)
    row_mid = jnp.where(row_tok % sk < n_ctx, n_batch, row_tok // sk).astype(I32)
    return row_tok, blk_ea, blk_eb, blk_nv, n_used.reshape(1).astype(I32), row_mid


def _rope_tables(n_lat, n_ctx, dim):
    rows = n_lat // GRID_W
    row = jnp.repeat(jnp.arange(rows, dtype=F32), GRID_W)
    col = jnp.tile(jnp.arange(GRID_W, dtype=F32), rows)
    n_freq = dim // 4
    inv = ROPE_THETA ** (-jnp.arange(n_freq, dtype=F32) / n_freq)
    ang = jnp.concatenate([row[:, None] * inv, col[:, None] * inv], axis=-1)
    cos = jnp.repeat(jnp.cos(ang), 2, axis=-1)
    sin = jnp.repeat(jnp.sin(ang), 2, axis=-1)
    even = (jnp.arange(dim) % 2 == 0)
    tabs = jnp.stack([cos, jnp.where(even, -sin, 0.0), jnp.where(even, 0.0, sin)])
    ident = jnp.stack([jnp.ones((n_ctx, dim), F32), jnp.zeros((n_ctx, dim), F32), jnp.zeros((n_ctx, dim), F32)])
    return jnp.tile(jnp.concatenate([ident, tabs], axis=1), (1, 1, LANES // dim))


def _block_diag(w):
    n, c, _ = w.shape
    eye = jnp.eye(n, dtype=w.dtype)
    return (eye[:, None, :, None] * w[:, :, None, :]).reshape(n * c, n * c)


def kernel(x, c, ctx, c_ctx, ada_w, ada_b, w_in, conv_w, conv_b, rg_wa, rg_ba, rg_wx, rg_bx, rg_lam, q_norm_g, k_norm_g, diff_lambda, diff_subln_g, w_out, ln1_g, ln1_b, ln2_g, ln2_b, router_w, router_b, exp_w_gate, exp_w_up, exp_w_down):
    bsz, s, d = x.shape
    n_ctx = ctx.shape[1]
    depth = w_in.shape[0]
    sk = n_ctx + s
    assert d == D_MODEL and n_ctx == TOKEN_TILE and s % TOKEN_TILE == 0 and bsz + 1 <= SUBLANES
    n_ctx_tiles = n_ctx // TOKEN_TILE
    n_tok = bsz * sk
    alpha = (2.0 * depth) ** 0.25

    c_rows = jnp.concatenate([c, c_ctx[None, :], jnp.zeros((SUBLANES - bsz - 1, d), F32)], axis=0)
    mods = _ada_modulation(c_rows, ada_w, ada_b).reshape(depth, SUBLANES, N_MOD, d)

    rope_g = _rope_tables(s, n_ctx, GQA_HEAD_DIM)
    rope_d = _rope_tables(s, n_ctx, DIFF_QK_DIM)
    lane = np.arange(LANES)
    seg = jnp.asarray(lane[:, None] // GQA_HEAD_DIM == lane[None, :] // GQA_HEAD_DIM, BF16)
    rw_t = router_w.T
    rw_hi = rw_t.astype(BF16)
    rw_split = jnp.concatenate([rw_hi, (rw_t - rw_hi.astype(F32)).astype(BF16)], axis=0)
    rb = router_b.reshape(N_EXPERTS, 1)
    zero_aux = jnp.zeros((SUBLANES, LANES), F32)

    xc = jnp.concatenate([jnp.concatenate([ctx, x], axis=1).reshape(n_tok, d),
                          jnp.zeros((2 * MOE_ROWS, d), F32)], axis=0)
    w_gate_all = exp_w_gate.reshape(depth * N_EXPERTS, d, EXPERT_FF)
    w_up_all = exp_w_up.reshape(depth * N_EXPERTS, d, EXPERT_FF)
    w_down_all = exp_w_down.reshape(depth * N_EXPERTS, EXPERT_FF, d)
    for li in range(depth):
        lam_init = 0.8 - 0.6 * math.exp(-0.3 * li)
        mod = mods[li]
        gains = jnp.concatenate([jnp.tile(jnp.tile(q_norm_g[li], 2)[None, :], (4, 1)),
                                 jnp.tile(k_norm_g[li], 2)[None, :], jnp.zeros((3, LANES), F32)], axis=0)
        zl, q, k, vt, dq, dk, dvt = _in_proj(xc, bsz, sk, mod, w_in[li].astype(BF16), gains, seg, rope_g, rope_d,
                                             n_ctx_tiles)

        w_gates = jnp.stack([jnp.concatenate([_block_diag(rg_wa[li, dd]), _block_diag(rg_wx[li, dd])], axis=1)
                             for dd in range(2)]).astype(BF16)
        b_gates = jnp.concatenate([rg_ba[li], rg_bx[li]], axis=-1)[:, None, :]
        h_lru = _lru(zl, conv_w[li], conv_b[li][None, :], w_gates, b_gates, rg_lam[li][:, None, :])

        yb = _attention(q, k, vt, zero_aux, diff=False, n_ctx_tiles=n_ctx_tiles)
        aux = jnp.concatenate([
            jnp.pad(diff_lambda[li], ((0, 0), (0, LANES - DIFF_QK_DIM))),
            jnp.full((1, LANES), lam_init, F32),
            jnp.tile(diff_subln_g[li], 2)[None, :],
            jnp.zeros((2, LANES), F32)], axis=0)
        yc = _attention(dq, dk, dvt, aux, diff=True, n_ctx_tiles=n_ctx_tiles)

        x1, route_i, route_f = _out_proj(h_lru, zl, yb, yc, xc, mod, w_out[li].astype(BF16), ln1_g[li][None, :],
                                         ln1_b[li][None, :], rw_split, rb, alpha, n_ctx_tiles)

        cls = route_i[:, 0, :].reshape(n_tok)
        row_tok, blk_ea, blk_eb, blk_nv, n_used, row_mid = _dispatch(cls, n_tok, sk, n_ctx, bsz)
        gates = jnp.stack([route_f[:, 0, :].reshape(n_tok), route_f[:, 1, :].reshape(n_tok)], axis=-1)
        xc = _moe(x1, n_tok, row_tok, blk_ea, blk_eb, blk_nv, n_used, gates[row_tok], row_mid[:, None], mod,
                  ln2_g[li][None, :], ln2_b[li][None, :], w_gate_all, w_up_all, w_down_all, li, alpha, bsz)
    return xc[:n_tok].reshape(bsz, sk, d)[:, n_ctx:, :]
```

```python
import functools
import math

import numpy as np
import jax
import jax.numpy as jnp
from jax import lax
from jax.experimental import pallas as pl
from jax.experimental.pallas import tpu as pltpu

F32 = jnp.float32
BF16 = jnp.bfloat16
I32 = jnp.int32

D_MODEL = 1024
GRID_W = 64
LRU_WIDTH = 256
LRU_BLOCKS = 4
CONV_W = 4
RG_C = 8.0
GQA_HEADS = 8
GQA_KV_HEADS = 2
GQA_HEAD_DIM = 64
DIFF_HEADS = 4
DIFF_QK_DIM = 32
DIFF_V_DIM = 64
IN_WIDTH = 2048
N_EXPERTS = 16
N_GROUPS = 4
EXPERTS_PER_GROUP = 4
EXPERT_FF = 512
ROPE_THETA = 10000.0
EPS = 1e-6
N_MOD = 6
LOG2_E = math.log2(math.e)

COL_AX, COL_AG, COL_GQ, COL_GK, COL_GV, COL_DQ, COL_DK, COL_DV = 0, 256, 512, 1024, 1152, 1280, 1536, 1792

LANES = 128
SUBLANES = 8
VMEM_LIMIT = 56 * 1024 * 1024

TOKEN_TILE = 256
VT_ROWS = 80
PAIRS_LO = (0, 0, 0, 1, 1, 2)
PAIRS_HI = (1, 2, 3, 2, 3, 3)
N_CLASSES = N_GROUPS * len(PAIRS_LO)
CLASS_ROWS = 32
MOE_ROWS = 128
KEY_TILES_PER_TRIP = 4

NT_DIMS = (((1,), (1,)), ((), ()))


def _params(semantics):
    return pltpu.CompilerParams(dimension_semantics=semantics, vmem_limit_bytes=VMEM_LIMIT)


def _ada_kernel(c_ref, w_ref, b_ref, o_ref):
    c = c_ref[...]
    o_ref[0] = jnp.dot(c * jax.nn.sigmoid(c), w_ref[0], preferred_element_type=F32) + b_ref[0]


def _ada_modulation(c_rows, ada_w, ada_b):
    depth, d, n = ada_w.shape
    tn = 1536
    return pl.pallas_call(
        _ada_kernel,
        out_shape=jax.ShapeDtypeStruct((depth, SUBLANES, n), F32),
        grid=(depth, n // tn),
        in_specs=[
            pl.BlockSpec((SUBLANES, d), lambda l, j: (0, 0)),
            pl.BlockSpec((1, d, tn), lambda l, j: (l, 0, j)),
            pl.BlockSpec((1, 1, tn), lambda l, j: (l, 0, j)),
        ],
        out_specs=pl.BlockSpec((1, SUBLANES, tn), lambda l, j: (l, 0, j)),
        compiler_params=_params(("arbitrary", "arbitrary")),
        name="ada_modulation",
    )(c_rows, ada_w, ada_b.reshape(depth, 1, n))


def _rotate_pairs(v, cos, sin_next, sin_prev):
    return v * cos + pltpu.roll(v, LANES - 1, 1) * sin_next + pltpu.roll(v, 1, 1) * sin_prev


def _in_proj_kernel(x_ref, mod_ref, w_ref, gain_ref, seg_ref, rope_g_ref, rope_d_ref,
                    zl_ref, q_ref, k_ref, vt_ref, dq_ref, dk_ref, dvt_ref):
    tm = x_ref.shape[0]
    x = x_ref[...]
    shift = mod_ref[0, 0:1, :]
    scale = mod_ref[0, 1:2, :]
    h = (x * (1.0 + scale) + shift).astype(BF16)
    z = jnp.dot(h, w_ref[...], preferred_element_type=F32)
    zl_ref[0] = z[:, COL_AX:COL_GQ]

    ones_rows = jnp.where(lax.broadcasted_iota(I32, (VT_ROWS - GQA_HEAD_DIM, tm), 0) == 0, 1.0, 0.0).astype(BF16)

    seg = seg_ref[...]
    cos, s_next, s_prev = rope_g_ref[0], rope_g_ref[1], rope_g_ref[2]
    for g in range(5):
        v = z[:, COL_GQ + g * LANES:COL_GQ + (g + 1) * LANES]
        sq = v * v
        sq_hi = sq.astype(BF16)
        sq_lo = (sq - sq_hi.astype(F32)).astype(BF16)
        ssq = jnp.dot(sq_hi, seg, preferred_element_type=F32) + jnp.dot(sq_lo, seg, preferred_element_type=F32)
        v = v * lax.rsqrt(ssq * (1.0 / GQA_HEAD_DIM) + EPS) * gain_ref[g:g + 1, :]
        v = _rotate_pairs(v, cos, s_next, s_prev)
        if g < 4:
            v = (v * (GQA_HEAD_DIM ** -0.5 * LOG2_E)).astype(BF16)
            q_ref[0, 2 * g] = v[:, :GQA_HEAD_DIM]
            q_ref[0, 2 * g + 1] = v[:, GQA_HEAD_DIM:]
        else:
            v = v.astype(BF16)
            k_ref[0, 0] = v[:, :GQA_HEAD_DIM]
            k_ref[0, 1] = v[:, GQA_HEAD_DIM:]
    vt = z[:, COL_GV:COL_DQ].T.astype(BF16)
    for hd in range(GQA_KV_HEADS):
        vt_ref[0, hd, 0, 0:GQA_HEAD_DIM, :] = vt[hd * GQA_HEAD_DIM:(hd + 1) * GQA_HEAD_DIM]
        vt_ref[0, hd, 0, GQA_HEAD_DIM:VT_ROWS, :] = ones_rows

    cos, s_next, s_prev = rope_d_ref[0], rope_d_ref[1], rope_d_ref[2]
    for g in range(4):
        v = _rotate_pairs(z[:, COL_DQ + g * LANES:COL_DQ + (g + 1) * LANES], cos, s_next, s_prev)
        if g < 2:
            v = v * (DIFF_QK_DIM ** -0.5 * LOG2_E)
        v = v.astype(BF16)
        dst = dq_ref if g < 2 else dk_ref
        for j in range(4):
            dst[0, (g % 2) * 4 + j] = v[:, j * DIFF_QK_DIM:(j + 1) * DIFF_QK_DIM]
    dvt = z[:, COL_DV:IN_WIDTH].T.astype(BF16)
    for hd in range(DIFF_HEADS):
        dvt_ref[0, hd, 0, 0:DIFF_V_DIM, :] = dvt[hd * DIFF_V_DIM:(hd + 1) * DIFF_V_DIM]
        dvt_ref[0, hd, 0, DIFF_V_DIM:VT_ROWS, :] = ones_rows


def _in_proj(xc, bsz, sk, mod, w_in, gains, seg, rope_g, rope_d, n_ctx_tiles):
    d = xc.shape[1]
    tm = TOKEN_TILE
    nt = sk // tm
    n_mod_rows = mod.shape[0]

    def mod_map(b, i):
        return (jnp.where(i < n_ctx_tiles, bsz, b), 0, 0)

    assert n_mod_rows > bsz
    return pl.pallas_call(
        _in_proj_kernel,
        out_shape=(
            jax.ShapeDtypeStruct((bsz, sk, COL_GQ), F32),
            jax.ShapeDtypeStruct((bsz, GQA_HEADS, sk, GQA_HEAD_DIM), BF16),
            jax.ShapeDtypeStruct((bsz, GQA_KV_HEADS, sk, GQA_HEAD_DIM), BF16),
            jax.ShapeDtypeStruct((bsz, GQA_KV_HEADS, nt, VT_ROWS, tm), BF16),
            jax.ShapeDtypeStruct((bsz, 2 * DIFF_HEADS, sk, DIFF_QK_DIM), BF16),
            jax.ShapeDtypeStruct((bsz, 2 * DIFF_HEADS, sk, DIFF_QK_DIM), BF16),
            jax.ShapeDtypeStruct((bsz, DIFF_HEADS, nt, VT_ROWS, tm), BF16),
        ),
        grid=(bsz, nt),
        in_specs=[
            pl.BlockSpec((tm, d), lambda b, i: (b * nt + i, 0)),
            pl.BlockSpec((1, N_MOD, d), mod_map),
            pl.BlockSpec((d, IN_WIDTH), lambda b, i: (0, 0)),
            pl.BlockSpec((SUBLANES, LANES), lambda b, i: (0, 0)),
            pl.BlockSpec((LANES, LANES), lambda b, i: (0, 0)),
            pl.BlockSpec((3, tm, LANES), lambda b, i: (0, i, 0)),
            pl.BlockSpec((3, tm, LANES), lambda b, i: (0, i, 0)),
        ],
        out_specs=(
            pl.BlockSpec((1, tm, COL_GQ), lambda b, i: (b, i, 0)),
            pl.BlockSpec((1, GQA_HEADS, tm, GQA_HEAD_DIM), lambda b, i: (b, 0, i, 0)),
            pl.BlockSpec((1, GQA_KV_HEADS, tm, GQA_HEAD_DIM), lambda b, i: (b, 0, i, 0)),
            pl.BlockSpec((1, GQA_KV_HEADS, 1, VT_ROWS, tm), lambda b, i: (b, 0, i, 0, 0)),
            pl.BlockSpec((1, 2 * DIFF_HEADS, tm, DIFF_QK_DIM), lambda b, i: (b, 0, i, 0)),
            pl.BlockSpec((1, 2 * DIFF_HEADS, tm, DIFF_QK_DIM), lambda b, i: (b, 0, i, 0)),
            pl.BlockSpec((1, DIFF_HEADS, 1, VT_ROWS, tm), lambda b, i: (b, 0, i, 0, 0)),
        ),
        compiler_params=_params(("arbitrary", "arbitrary")),
        name="in_proj",
    )(xc, mod, w_in, gains, seg, rope_g, rope_d)


def _expm1(x):
    u = jnp.exp(x)
    return jnp.where(u == 1.0, x, (u - 1.0) * x / jnp.log(u))


def _lru_kernel(ax_ref, prev_ref, next_ref, cw_ref, cb_ref, w_ref, b_ref, lam_ref, h_ref, carry_ref, *, nt):
    tm = ax_ref.shape[1]
    d = pl.program_id(1)
    j = pl.program_id(2)
    blk = jnp.where(d == 0, j, jnp.where(j == 0, 0, nt - j))

    @pl.when(j == 0)
    def _():
        carry_ref[...] = jnp.zeros_like(carry_ref)

    x = ax_ref[0]
    no_left = jnp.logical_or(blk == 0, blk == 1)
    no_right = jnp.logical_or(blk == 0, blk == nt - 1)
    left = jnp.where(no_left, 0.0, prev_ref[0, SUBLANES - 1:SUBLANES, :])
    right0 = jnp.where(no_right, 0.0, next_ref[0, 0:1, :])
    right1 = jnp.where(no_right, 0.0, next_ref[0, 1:2, :])
    row = lax.broadcasted_iota(I32, (tm, 1), 0)
    x_m1 = jnp.where(row == 0, left, pltpu.roll(x, 1, 0))
    x_p1 = jnp.where(row == tm - 1, right0, pltpu.roll(x, tm - 1, 0))
    x_p2 = jnp.where(row == tm - 2, right0, jnp.where(row == tm - 1, right1, pltpu.roll(x, tm - 2, 0)))
    u = cb_ref[...] + x_m1 * cw_ref[0:1, :] + x * cw_ref[1:2, :] + x_p1 * cw_ref[2:3, :] + x_p2 * cw_ref[3:4, :]

    g = jnp.dot(u.astype(BF16), w_ref[0], preferred_element_type=F32) + b_ref[0]
    r = jax.nn.sigmoid(g[:, :LRU_WIDTH])
    gate_in = jax.nn.sigmoid(g[:, LRU_WIDTH:])
    neg_lam = -lam_ref[0]
    softplus = jnp.maximum(neg_lam, 0.0) + jnp.log1p(jnp.exp(-jnp.abs(neg_lam)))
    log_a = (-RG_C * r) * softplus
    a = jnp.exp(log_a)
    b = jnp.sqrt(-_expm1(2.0 * log_a)) * (gate_in * u)

    def scan(reverse):
        aa, bb = a, b
        s = 1
        while s < tm:
            shift = tm - s if reverse else s
            keep = (row < tm - s) if reverse else (row >= s)
            a_sh = jnp.where(keep, pltpu.roll(aa, shift, 0), 1.0)
            b_sh = jnp.where(keep, pltpu.roll(bb, shift, 0), 0.0)
            bb = aa * b_sh + bb
            aa = aa * a_sh
            s *= 2
        h = aa * carry_ref[...] + bb
        h_ref[0, 0] = h
        carry_ref[...] = h[0:1, :] if reverse else h[tm - 1:tm, :]

    @pl.when(d == 0)
    def _():
        scan(False)

    @pl.when(d == 1)
    def _():
        scan(True)


def _lru(zl, conv_w, conv_b, w_gates, b_gates, lam):
    bsz, sk, _ = zl.shape
    tm = TOKEN_TILE
    nt = sk // tm
    per = tm // SUBLANES

    def blk_of(d, j):
        return jnp.where(d == 0, j, jnp.where(j == 0, 0, nt - j))

    return pl.pallas_call(
        functools.partial(_lru_kernel, nt=nt),
        out_shape=jax.ShapeDtypeStruct((2, bsz, sk, LRU_WIDTH), F32),
        grid=(bsz, 2, nt),
        in_specs=[
            pl.BlockSpec((1, tm, LRU_WIDTH), lambda b, d, j: (b, blk_of(d, j), 0)),
            pl.BlockSpec((1, SUBLANES, LRU_WIDTH), lambda b, d, j: (b, jnp.maximum(blk_of(d, j) * per - 1, 0), 0)),
            pl.BlockSpec((1, SUBLANES, LRU_WIDTH),
                         lambda b, d, j: (b, jnp.minimum((blk_of(d, j) + 1) * per, nt * per - 1), 0)),
            pl.BlockSpec((CONV_W, LRU_WIDTH), lambda b, d, j: (0, 0)),
            pl.BlockSpec((1, LRU_WIDTH), lambda b, d, j: (0, 0)),
            pl.BlockSpec((1, LRU_WIDTH, 2 * LRU_WIDTH), lambda b, d, j: (d, 0, 0)),
            pl.BlockSpec((1, 1, 2 * LRU_WIDTH), lambda b, d, j: (d, 0, 0)),
            pl.BlockSpec((1, 1, LRU_WIDTH), lambda b, d, j: (d, 0, 0)),
        ],
        out_specs=pl.BlockSpec((1, 1, tm, LRU_WIDTH), lambda b, d, j: (d, b, blk_of(d, j), 0)),
        scratch_shapes=[pltpu.VMEM((1, LRU_WIDTH), F32)],
        compiler_params=_params(("arbitrary", "arbitrary", "arbitrary")),
        name="rg_lru",
    )(zl, zl, zl, conv_w, conv_b, w_gates, b_gates, lam)


def _attn_kernel(q_ref, k_ref, vt_ref, aux_ref, o_ref, s_ref, m_ref, acc_ref, *, k_heads, v_heads, nt, n_ctx_tiles, diff):
    tq = q_ref.shape[2]
    tk = tq
    dv = GQA_HEAD_DIM
    qi = pl.program_id(2)
    n_k = jnp.where(qi < n_ctx_tiles, n_ctx_tiles, nt)

    def scores(kt):
        start = pl.multiple_of(kt * tk, tk)
        return jnp.concatenate(
            [lax.dot_general(k_ref[0, k_heads[j], pl.ds(start, tk), :], q_ref[0, j], NT_DIMS,
                             preferred_element_type=F32) for j in range(4)], axis=1)

    def weighted_values(kt, p):
        return jnp.concatenate(
            [jnp.dot(vt_ref[0, v_heads[j], kt], p[:, j * tq:(j + 1) * tq], preferred_element_type=F32)
             for j in range(4)], axis=1)

    def col_max(s, m):
        return jnp.maximum(m, jnp.max(s, axis=0, keepdims=True))

    m_ref[...] = jnp.full_like(m_ref, -jnp.inf)
    acc_ref[...] = jnp.zeros_like(acc_ref)
    s_ref[...] = scores(0)

    def trip(i, carry):
        kt = KEY_TILES_PER_TRIP * i
        s_cur = s_ref[...]
        maxes = [m_ref[...]]
        probs = []
        for u in range(KEY_TILES_PER_TRIP):
            s_next = scores(kt + u + 1)
            m_new = col_max(s_cur, maxes[-1])
            probs.append(jnp.exp2(s_cur - m_new).astype(BF16))
            maxes.append(m_new)
            if u == KEY_TILES_PER_TRIP - 1:
                s_ref[...] = s_next
            else:
                s_cur = s_next
        acc = acc_ref[...] * jnp.exp2(maxes[0] - maxes[-1])
        for u in range(KEY_TILES_PER_TRIP):
            pv = weighted_values(kt + u, probs[u])
            acc = acc + (pv if u == KEY_TILES_PER_TRIP - 1 else pv * jnp.exp2(maxes[u + 1] - maxes[-1]))
        acc_ref[...] = acc
        m_ref[...] = maxes[-1]
        return carry

    lax.fori_loop(0, (n_k - 1) // KEY_TILES_PER_TRIP, trip, 0)

    s0 = s_ref[...]
    m0 = m_ref[...]
    m1 = col_max(s0, m0)
    p0 = jnp.exp2(s0 - m1).astype(BF16)
    acc_all = acc_ref[...] * jnp.exp2(m0 - m1) + weighted_values(n_k - 1, p0)

    outs = []
    for j in range(4):
        acc = acc_all[:, j * tq:(j + 1) * tq]
        outs.append(acc[0:dv] / acc[dv:dv + 1])
    if diff:
        dl = aux_ref[0:4, :]
        lam_init = aux_ref[4:5, 0:1]
        lam = (jnp.exp(jnp.sum(dl[0:1] * dl[1:2], axis=1, keepdims=True))
               - jnp.exp(jnp.sum(dl[2:3] * dl[3:4], axis=1, keepdims=True)) + lam_init)
        heads = []
        for hd in range(2):
            o = outs[2 * hd] - lam * outs[2 * hd + 1]
            heads.append(o * lax.rsqrt(jnp.mean(o * o, axis=0, keepdims=True) + EPS))
        o_ref[0] = ((jnp.concatenate(heads, axis=0).T * aux_ref[5:6, :]) * (1.0 - lam_init)).astype(o_ref.dtype)
    else:
        for pair in range(2):
            o_ref[0, :, pair * 2 * dv:(pair + 1) * 2 * dv] = (
                jnp.concatenate(outs[2 * pair:2 * pair + 2], axis=0).T.astype(o_ref.dtype))


def _attention(q, k, vt, aux, *, diff, n_ctx_tiles):
    bsz, n_q, sk, dh = q.shape
    n_k_heads = k.shape[1]
    n_v_heads = vt.shape[1]
    tq = TOKEN_TILE
    nt = sk // tq
    assert (nt - 1) % KEY_TILES_PER_TRIP == 0 and (n_ctx_tiles - 1) % KEY_TILES_PER_TRIP == 0
    groups = n_q // 4
    kpg = n_k_heads // groups
    vpg = n_v_heads // groups
    k_heads = tuple(j * kpg // 4 for j in range(4))
    v_heads = tuple(j * vpg // 4 for j in range(4))
    width = 2 * DIFF_V_DIM if diff else 4 * GQA_HEAD_DIM
    return pl.pallas_call(
        functools.partial(_attn_kernel, k_heads=k_heads, v_heads=v_heads, nt=nt, n_ctx_tiles=n_ctx_tiles, diff=diff),
        out_shape=jax.ShapeDtypeStruct((bsz, sk, groups * width), BF16),
        grid=(bsz, groups, nt),
        in_specs=[
            pl.BlockSpec((1, 4, tq, dh), lambda b, g, i: (b, g, i, 0)),
            pl.BlockSpec((1, kpg, sk, dh), lambda b, g, i: (b, g, 0, 0)),
            pl.BlockSpec((1, vpg, nt, VT_ROWS, tq), lambda b, g, i: (b, g, 0, 0, 0)),
            pl.BlockSpec((SUBLANES, LANES), lambda b, g, i: (0, 0)),
        ],
        out_specs=pl.BlockSpec((1, tq, width), lambda b, g, i: (b, i, g)),
        scratch_shapes=[pltpu.VMEM((tq, 4 * tq), F32), pltpu.VMEM((1, 4 * tq), F32),
                        pltpu.VMEM((VT_ROWS, 4 * tq), F32)],
        compiler_params=_params(("arbitrary", "arbitrary", "arbitrary")),
        name="diff_attention" if diff else "gqa_attention",
    )(q, k, vt, aux)


def _layer_norm(r, g, b):
    mu = jnp.mean(r, axis=-1, keepdims=True)
    c = r - mu
    var = jnp.mean(c * c, axis=-1, keepdims=True)
    return c * lax.rsqrt(var + EPS) * g + b


def _route(logits, bias):
    mx = jnp.max(logits, axis=0, keepdims=True)
    e = jnp.exp(logits - mx)
    probs = e / jnp.sum(e, axis=0, keepdims=True)
    sel = probs + bias
    epg = EXPERTS_PER_GROUP
    in_top2, scores = [], []
    for g in range(N_GROUPS):
        v = [sel[g * epg + i:g * epg + i + 1] for i in range(epg)]
        masks = []
        for i in range(epg):
            rank = jnp.zeros_like(v[i])
            for j in range(epg):
                if j != i:
                    beats = (v[j] > v[i]) if j > i else (v[j] >= v[i])
                    rank = rank + beats.astype(F32)
            masks.append(rank < 2.0)
        in_top2.append(masks)
        scores.append(sum(jnp.where(masks[i], v[i], 0.0) for i in range(epg)))
    chosen = []
    for g in range(N_GROUPS):
        c = None
        for j in range(N_GROUPS):
            if j != g:
                t = (scores[j] < scores[g]) if j < g else (scores[j] <= scores[g])
                c = t if c is None else jnp.logical_and(c, t)
        chosen.append(c)
    picked, weight = [], []
    for i in range(epg):
        m = None
        w = jnp.zeros_like(scores[0])
        for g in range(N_GROUPS):
            t = jnp.logical_and(chosen[g], in_top2[g][i])
            m = t if m is None else jnp.logical_or(m, t)
            w = w + jnp.where(t, probs[g * epg + i:g * epg + i + 1], 0.0)
        picked.append(m)
        weight.append(w)
    total = weight[0] + weight[1] + weight[2] + weight[3]
    gate = [w / total for w in weight]
    lo = jnp.where(picked[0], 0.0, jnp.where(picked[1], 1.0, 2.0))
    hi = jnp.where(picked[3], 3.0, jnp.where(picked[2], 2.0, 1.0))
    gate_lo = jnp.where(picked[0], gate[0], jnp.where(picked[1], gate[1], gate[2]))
    gate_hi = jnp.where(picked[3], gate[3], jnp.where(picked[2], gate[2], gate[1]))
    group = sum(jnp.where(chosen[g], float(g), 0.0) for g in range(N_GROUPS))
    base = jnp.where(lo == 0.0, 0.0, jnp.where(lo == 1.0, 3.0, 5.0))
    cls = group * float(len(PAIRS_LO)) + base + hi - lo - 1.0
    return cls.astype(I32), gate_lo, gate_hi


def _out_proj_kernel(hf_ref, hb_ref, ag_ref, yb_ref, yc_ref, x_ref, mod_ref, w_ref, lng_ref, lnb_ref,
                     rw_ref, rb_ref, x1_ref, ri_ref, cnt_ref, run_ref, *, alpha, n_ctx_tiles, n_batch):
    tm = x_ref.shape[0]
    ya = (jax.nn.gelu(ag_ref[0]) * (hf_ref[0, 0] + hb_ref[0, 0])).astype(BF16)
    n_a = LRU_WIDTH
    n_b = n_a + GQA_HEADS * GQA_HEAD_DIM
    y = (jnp.dot(ya, w_ref[0:n_a, :], preferred_element_type=F32)
         + jnp.dot(yb_ref[0], w_ref[n_a:n_b, :], preferred_element_type=F32)
         + jnp.dot(yc_ref[0], w_ref[n_b:, :], preferred_element_type=F32))
    x1 = _layer_norm(alpha * x_ref[...] + mod_ref[0, 2:3, :] * y, lng_ref[...], lnb_ref[...])
    d = x1.shape[1]
    x1_ref[:, 0:d] = x1

    h2 = x1 * (1.0 + mod_ref[0, 4:5, :]) + mod_ref[0, 3:4, :]
    h_hi = h2.astype(BF16)
    h_lo = (h2 - h_hi.astype(F32)).astype(BF16)
    both = lax.dot_general(rw_ref[...], h_hi, NT_DIMS, preferred_element_type=F32)
    logits = (both[0:N_EXPERTS] + both[N_EXPERTS:]
              + lax.dot_general(rw_ref[0:N_EXPERTS, :], h_lo, NT_DIMS, preferred_element_type=F32))
    cls, gate_lo, gate_hi = _route(logits, rb_ref[...])

    b = pl.program_id(0)
    i = pl.program_id(1)
    mod_row = jnp.where(i < n_ctx_tiles, n_batch, b).astype(F32)
    extra = jnp.concatenate([gate_lo, gate_hi, jnp.full((1, tm), mod_row, F32), jnp.zeros((LANES - 3, tm), F32)], axis=0)
    x1_ref[:, d:] = extra.T

    @pl.when(jnp.logical_and(b == 0, i == 0))
    def _():
        run_ref[...] = jnp.zeros_like(run_ref)

    member = lax.broadcasted_iota(I32, (CLASS_ROWS, tm), 0) == cls
    before = lax.broadcasted_iota(I32, (tm, tm), 0) < lax.broadcasted_iota(I32, (tm, tm), 1)
    earlier = jnp.dot(member.astype(BF16), before.astype(BF16), preferred_element_type=F32)
    run = run_ref[...]
    rank = jnp.sum(jnp.where(member, earlier + run, 0.0), axis=0, keepdims=True).astype(I32)
    run = run + jnp.sum(member.astype(F32), axis=1, keepdims=True)
    run_ref[...] = run
    cnt_ref[...] = jnp.broadcast_to(run, cnt_ref.shape).astype(I32)
    ri_ref[0] = jnp.concatenate([cls, rank, jnp.zeros((SUBLANES - 2, tm), I32)], axis=0)


def _out_proj(h_lru, zl, yb, yc, xc, mod, w_out, ln_g, ln_b, rw_split, rb, alpha, n_ctx_tiles):
    bsz, sk, _ = zl.shape
    d = xc.shape[1]
    tm = TOKEN_TILE
    nt = sk // tm

    def mod_map(b, i):
        return (jnp.where(i < n_ctx_tiles, bsz, b), 0, 0)

    return pl.pallas_call(
        functools.partial(_out_proj_kernel, alpha=alpha, n_ctx_tiles=n_ctx_tiles, n_batch=bsz),
        out_shape=(
            jax.ShapeDtypeStruct((bsz * sk, d + LANES), F32),
            jax.ShapeDtypeStruct((bsz, SUBLANES, sk), I32),
            jax.ShapeDtypeStruct((CLASS_ROWS, LANES), I32),
        ),
        grid=(bsz, nt),
        in_specs=[
            pl.BlockSpec((1, 1, tm, LRU_WIDTH), lambda b, i: (0, b, i, 0)),
            pl.BlockSpec((1, 1, tm, LRU_WIDTH), lambda b, i: (1, b, i, 0)),
            pl.BlockSpec((1, tm, LRU_WIDTH), lambda b, i: (b, i, 1)),
            pl.BlockSpec((1, tm, GQA_HEADS * GQA_HEAD_DIM), lambda b, i: (b, i, 0)),
            pl.BlockSpec((1, tm, DIFF_HEADS * DIFF_V_DIM), lambda b, i: (b, i, 0)),
            pl.BlockSpec((tm, d), lambda b, i: (b * nt + i, 0)),
            pl.BlockSpec((1, N_MOD, d), mod_map),
            pl.BlockSpec((d, d), lambda b, i: (0, 0)),
            pl.BlockSpec((1, d), lambda b, i: (0, 0)),
            pl.BlockSpec((1, d), lambda b, i: (0, 0)),
            pl.BlockSpec((2 * N_EXPERTS, d), lambda b, i: (0, 0)),
            pl.BlockSpec((N_EXPERTS, 1), lambda b, i: (0, 0)),
        ],
        out_specs=(
            pl.BlockSpec((tm, d + LANES), lambda b, i: (b * nt + i, 0)),
            pl.BlockSpec((1, SUBLANES, tm), lambda b, i: (b, 0, i)),
            pl.BlockSpec((CLASS_ROWS, LANES), lambda b, i: (0, 0)),
        ),
        scratch_shapes=[pltpu.VMEM((CLASS_ROWS, 1), F32)],
        compiler_params=_params(("arbitrary", "arbitrary")),
        name="out_proj_router",
    )(h_lru, h_lru, zl, yb, yc, xc, mod, w_out, ln_g, ln_b, rw_split, rb)


def _moe_kernel(cls_ref, rank_ref, pstart_ref, ea_ref, eb_ref, nv_ref, nu_ref,
                x_hbm, mod_ref, lng_ref, lnb_ref,
                wga_ref, wua_ref, wda_ref, wgb_ref, wub_ref, wdb_ref,
                o_hbm, tok_ref, xbuf, obuf, gsem, ssem, *, alpha, n_batch, n_tok):
    rows = xbuf.shape[1]
    d = obuf.shape[2]
    i = pl.program_id(0)
    n_steps = pl.num_programs(0)
    n_used = nu_ref[0]
    slot = i % 2

    @pl.when(i == 0)
    def _():
        def clear(r, c):
            tok_ref[r] = 0
            return c
        lax.fori_loop(0, tok_ref.shape[0], clear, 0, unroll=8)

        def place(t, c):
            tok_ref[pstart_ref[cls_ref[t]] + rank_ref[t]] = t
            return c
        lax.fori_loop(0, n_tok, place, 0, unroll=8)

        obuf[...] = jnp.zeros_like(obuf)
        for s in range(2):
            spare = pltpu.make_async_copy(obuf.at[s], o_hbm.at[pl.ds(n_tok + s * rows, rows)], ssem.at[s])
            spare.start()
            spare.wait()

    def start_gather(step, slot_):
        for r in range(rows):
            pltpu.make_async_copy(x_hbm.at[pl.ds(tok_ref[step * rows + r], 1)], xbuf.at[slot_, pl.ds(r, 1)],
                                  gsem.at[slot_]).start()

    def wait_gather(slot_):
        pltpu.make_async_copy(x_hbm.at[pl.ds(0, rows)], xbuf.at[slot_], gsem.at[slot_]).wait()

    def start_scatter(step, slot_):
        n_valid = nv_ref[step]
        for r in range(rows):
            dst = jnp.where(r < n_valid, tok_ref[step * rows + r], n_tok + slot_ * rows + r)
            pltpu.make_async_copy(obuf.at[slot_, pl.ds(r, 1)], o_hbm.at[pl.ds(dst, 1)], ssem.at[slot_]).start()

    def wait_scatter(slot_):
        pltpu.make_async_copy(obuf.at[slot_], o_hbm.at[pl.ds(0, rows)], ssem.at[slot_]).wait()

    @pl.when(i == 0)
    def _():
        start_gather(0, 0)

    @pl.when(i < n_used)
    def _():
        wait_gather(slot)

        @pl.when(i + 1 < n_used)
        def _():
            start_gather(i + 1, 1 - slot)

        @pl.when(i >= 2)
        def _():
            wait_scatter(slot)

        x = xbuf[slot, :, 0:d]
        gate_a = xbuf[slot, :, d:d + 1]
        gate_b = xbuf[slot, :, d + 1:d + 2]
        mid = xbuf[slot, :, d + 2:d + 3]

        def pick(kk):
            v = mod_ref[n_batch, kk:kk + 1, :]
            for b in range(n_batch):
                v = jnp.where(mid == float(b), mod_ref[b, kk:kk + 1, :], v)
            return v

        h = x * (1.0 + pick(4)) + pick(3)

        def ffn(wg, wu, wd):
            a = jnp.dot(h, wg[0], preferred_element_type=F32)
            u = jnp.dot(h, wu[0], preferred_element_type=F32)
            return jnp.dot((a * jax.nn.sigmoid(a)) * u, wd[0], preferred_element_type=F32)

        f = ffn(wga_ref, wua_ref, wda_ref) * gate_a + ffn(wgb_ref, wub_ref, wdb_ref) * gate_b
        obuf[slot] = _layer_norm(alpha * x + pick(5) * f, lng_ref[...], lnb_ref[...])
        start_scatter(i, slot)

    @pl.when(i == n_steps - 1)
    def _():
        wait_scatter(0)

        @pl.when(n_used >= 2)
        def _():
            wait_scatter(1)


def _moe(x_wide, n_tok, cls, rank, pstart, blk_ea, blk_eb, blk_nv, n_used, mod, ln_g, ln_b,
         w_gate, w_up, w_down, layer, alpha, n_batch):
    d = x_wide.shape[1] - LANES
    rows = MOE_ROWS
    n_blk = blk_ea.shape[0]
    ff = w_gate.shape[-1]
    first = layer * N_EXPERTS

    def wa(i, cl, rk, ps, ea, eb, nv, nu):
        return (first + ea[i], 0, 0)

    def wb(i, cl, rk, ps, ea, eb, nv, nu):
        return (first + eb[i], 0, 0)

    const2 = lambda i, cl, rk, ps, ea, eb, nv, nu: (0, 0)
    grid_spec = pltpu.PrefetchScalarGridSpec(
        num_scalar_prefetch=7,
        grid=(n_blk,),
        in_specs=[
            pl.BlockSpec(memory_space=pl.ANY),
            pl.BlockSpec(mod.shape, lambda i, cl, rk, ps, ea, eb, nv, nu: (0, 0, 0)),
            pl.BlockSpec((1, d), const2),
            pl.BlockSpec((1, d), const2),
            pl.BlockSpec((1, d, ff), wa),
            pl.BlockSpec((1, d, ff), wa),
            pl.BlockSpec((1, ff, d), wa),
            pl.BlockSpec((1, d, ff), wb),
            pl.BlockSpec((1, d, ff), wb),
            pl.BlockSpec((1, ff, d), wb),
        ],
        out_specs=pl.BlockSpec(memory_space=pl.ANY),
        scratch_shapes=[
            pltpu.SMEM((n_blk * rows,), I32),
            pltpu.VMEM((2, rows, d + LANES), F32),
            pltpu.VMEM((2, rows, d), F32),
            pltpu.SemaphoreType.DMA((2,)),
            pltpu.SemaphoreType.DMA((2,)),
        ],
    )
    return pl.pallas_call(
        functools.partial(_moe_kernel, alpha=alpha, n_batch=n_batch, n_tok=n_tok),
        out_shape=jax.ShapeDtypeStruct((n_tok + 2 * rows, d), F32),
        grid_spec=grid_spec,
        compiler_params=_params(("arbitrary",)),
        name="moe_experts",
    )(cls, rank, pstart, blk_ea, blk_eb, blk_nv, n_used, x_wide, mod, ln_g, ln_b,
      w_gate, w_up, w_down, w_gate, w_up, w_down)


def _block_tables(counts, n_tok):
    rows = MOE_ROWS
    n_blk = n_tok // rows + N_CLASSES
    n_pairs = len(PAIRS_LO)
    padded = (counts + rows - 1) // rows * rows
    pend = jnp.cumsum(padded)
    pstart = pend - padded
    n_used = pend[-1] // rows
    blk = jnp.arange(n_blk, dtype=I32)
    blk_ids = jnp.minimum(blk, jnp.maximum(n_used - 1, 0))
    blk_cls = jnp.minimum(jnp.sum((blk_ids[:, None] * rows >= pend[None, :]).astype(I32), axis=1), N_CLASSES - 1)
    onehot = blk_cls[:, None] == jnp.arange(N_CLASSES, dtype=I32)[None, :]
    within = blk_ids * rows - jnp.sum(jnp.where(onehot, pstart[None, :], 0), axis=1)
    count_b = jnp.sum(jnp.where(onehot, counts[None, :], 0), axis=1)
    blk_nv = jnp.where(blk < n_used, jnp.clip(count_b - within, 0, rows), 0).astype(I32)
    pair = blk_cls % n_pairs
    lo = sum(jnp.where(pair == p, PAIRS_LO[p], 0) for p in range(n_pairs))
    hi = sum(jnp.where(pair == p, PAIRS_HI[p], 0) for p in range(n_pairs))
    first = blk_cls // n_pairs * EXPERTS_PER_GROUP
    pstart_pad = jnp.concatenate([pstart, jnp.zeros((CLASS_ROWS - N_CLASSES,), I32)]).astype(I32)
    return (pstart_pad, (first + lo).astype(I32), (first + hi).astype(I32), blk_nv,
            n_used.reshape(1).astype(I32))


def _rope_tables(n_lat, n_ctx, dim):
    rows = n_lat // GRID_W
    row = jnp.repeat(jnp.arange(rows, dtype=F32), GRID_W)
    col = jnp.tile(jnp.arange(GRID_W, dtype=F32), rows)
    n_freq = dim // 4
    inv = ROPE_THETA ** (-jnp.arange(n_freq, dtype=F32) / n_freq)
    ang = jnp.concatenate([row[:, None] * inv, col[:, None] * inv], axis=-1)
    cos = jnp.repeat(jnp.cos(ang), 2, axis=-1)
    sin = jnp.repeat(jnp.sin(ang), 2, axis=-1)
    even = (jnp.arange(dim) % 2 == 0)
    tabs = jnp.stack([cos, jnp.where(even, -sin, 0.0), jnp.where(even, 0.0, sin)])
    ident = jnp.stack([jnp.ones((n_ctx, dim), F32), jnp.zeros((n_ctx, dim), F32), jnp.zeros((n_ctx, dim), F32)])
    return jnp.tile(jnp.concatenate([ident, tabs], axis=1), (1, 1, LANES // dim))


def _block_diag(w):
    n, c, _ = w.shape
    eye = jnp.eye(n, dtype=w.dtype)
    return (eye[:, None, :, None] * w[:, :, None, :]).reshape(n * c, n * c)


def kernel(x, c, ctx, c_ctx, ada_w, ada_b, w_in, conv_w, conv_b, rg_wa, rg_ba, rg_wx, rg_bx, rg_lam, q_norm_g, k_norm_g, diff_lambda, diff_subln_g, w_out, ln1_g, ln1_b, ln2_g, ln2_b, router_w, router_b, exp_w_gate, exp_w_up, exp_w_down):
    bsz, s, d = x.shape
    n_ctx = ctx.shape[1]
    depth = w_in.shape[0]
    sk = n_ctx + s
    assert d == D_MODEL and n_ctx == TOKEN_TILE and s % TOKEN_TILE == 0 and bsz + 1 <= SUBLANES
    n_ctx_tiles = n_ctx // TOKEN_TILE
    n_tok = bsz * sk
    alpha = (2.0 * depth) ** 0.25

    c_rows = jnp.concatenate([c, c_ctx[None, :], jnp.zeros((SUBLANES - bsz - 1, d), F32)], axis=0)
    mods = _ada_modulation(c_rows, ada_w, ada_b).reshape(depth, SUBLANES, N_MOD, d)

    rope_g = _rope_tables(s, n_ctx, GQA_HEAD_DIM)
    rope_d = _rope_tables(s, n_ctx, DIFF_QK_DIM)
    lane = np.arange(LANES)
    seg = jnp.asarray(lane[:, None] // GQA_HEAD_DIM == lane[None, :] // GQA_HEAD_DIM, BF16)
    rw_t = router_w.T
    rw_hi = rw_t.astype(BF16)
    rw_split = jnp.concatenate([rw_hi, (rw_t - rw_hi.astype(F32)).astype(BF16)], axis=0)
    rb = router_b.reshape(N_EXPERTS, 1)
    zero_aux = jnp.zeros((SUBLANES, LANES), F32)

    xc = jnp.concatenate([jnp.concatenate([ctx, x], axis=1).reshape(n_tok, d),
                          jnp.zeros((2 * MOE_ROWS, d), F32)], axis=0)
    w_gate_all = exp_w_gate.reshape(depth * N_EXPERTS, d, EXPERT_FF)
    w_up_all = exp_w_up.reshape(depth * N_EXPERTS, d, EXPERT_FF)
    w_down_all = exp_w_down.reshape(depth * N_EXPERTS, EXPERT_FF, d)
    for li in range(depth):
        lam_init = 0.8 - 0.6 * math.exp(-0.3 * li)
        mod = mods[li]
        gains = jnp.concatenate([jnp.tile(jnp.tile(q_norm_g[li], 2)[None, :], (4, 1)),
                                 jnp.tile(k_norm_g[li], 2)[None, :], jnp.zeros((3, LANES), F32)], axis=0)
        zl, q, k, vt, dq, dk, dvt = _in_proj(xc, bsz, sk, mod, w_in[li].astype(BF16), gains, seg, rope_g, rope_d,
                                             n_ctx_tiles)

        w_gates = jnp.stack([jnp.concatenate([_block_diag(rg_wa[li, dd]), _block_diag(rg_wx[li, dd])], axis=1)
                             for dd in range(2)]).astype(BF16)
        b_gates = jnp.concatenate([rg_ba[li], rg_bx[li]], axis=-1)[:, None, :]
        h_lru = _lru(zl, conv_w[li], conv_b[li][None, :], w_gates, b_gates, rg_lam[li][:, None, :])

        yb = _attention(q, k, vt, zero_aux, diff=False, n_ctx_tiles=n_ctx_tiles)
        aux = jnp.concatenate([
            jnp.pad(diff_lambda[li], ((0, 0), (0, LANES - DIFF_QK_DIM))),
            jnp.full((1, LANES), lam_init, F32),
            jnp.tile(diff_subln_g[li], 2)[None, :],
            jnp.zeros((2, LANES), F32)], axis=0)
        yc = _attention(dq, dk, dvt, aux, diff=True, n_ctx_tiles=n_ctx_tiles)

        x1, route_i, counts = _out_proj(h_lru, zl, yb, yc, xc, mod, w_out[li].astype(BF16), ln1_g[li][None, :],
                                        ln1_b[li][None, :], rw_split, rb, alpha, n_ctx_tiles)

        pstart, blk_ea, blk_eb, blk_nv, n_used = _block_tables(counts[:N_CLASSES, 0], n_tok)
        xc = _moe(x1, n_tok, route_i[:, 0, :].reshape(n_tok), route_i[:, 1, :].reshape(n_tok), pstart,
                  blk_ea, blk_eb, blk_nv, n_used, mod, ln2_g[li][None, :], ln2_b[li][None, :],
                  w_gate_all, w_up_all, w_down_all, li, alpha, bsz)
    return xc[:n_tok].reshape(bsz, sk, d)[:, n_ctx:, :]
```

```python
import functools
import math

import numpy as np
import jax
import jax.numpy as jnp
from jax import lax
from jax.experimental import pallas as pl
from jax.experimental.pallas import tpu as pltpu

F32 = jnp.float32
BF16 = jnp.bfloat16
I32 = jnp.int32

D_MODEL = 1024
GRID_W = 64
LRU_WIDTH = 256
LRU_BLOCKS = 4
CONV_W = 4
RG_C = 8.0
GQA_HEADS = 8
GQA_KV_HEADS = 2
GQA_HEAD_DIM = 64
DIFF_HEADS = 4
DIFF_QK_DIM = 32
DIFF_V_DIM = 64
IN_WIDTH = 2048
N_EXPERTS = 16
N_GROUPS = 4
EXPERTS_PER_GROUP = 4
EXPERT_FF = 512
ROPE_THETA = 10000.0
EPS = 1e-6
N_MOD = 6
LOG2_E = math.log2(math.e)

COL_AX, COL_AG, COL_GQ, COL_GK, COL_GV, COL_DQ, COL_DK, COL_DV = 0, 256, 512, 1024, 1152, 1280, 1536, 1792

LANES = 128
SUBLANES = 8
VMEM_LIMIT = 56 * 1024 * 1024

TOKEN_TILE = 256
VT_ROWS = 80
PAIRS_LO = (0, 0, 0, 1, 1, 2)
PAIRS_HI = (1, 2, 3, 2, 3, 3)
N_CLASSES = N_GROUPS * len(PAIRS_LO)
CLASS_ROWS = 32
MOE_ROWS = 128
KEY_TILES_PER_TRIP = 4
MAX_SHIFT_LAG = 60.0
MAX_SHIFT_RAISE = 120.0

NT_DIMS = (((1,), (1,)), ((), ()))


def _params(semantics):
    return pltpu.CompilerParams(dimension_semantics=semantics, vmem_limit_bytes=VMEM_LIMIT)


def _ada_kernel(c_ref, w_ref, b_ref, o_ref):
    c = c_ref[...]
    o_ref[0] = jnp.dot(c * jax.nn.sigmoid(c), w_ref[0], preferred_element_type=F32) + b_ref[0]


def _ada_modulation(c_rows, ada_w, ada_b):
    depth, d, n = ada_w.shape
    tn = 1536
    return pl.pallas_call(
        _ada_kernel,
        out_shape=jax.ShapeDtypeStruct((depth, SUBLANES, n), F32),
        grid=(depth, n // tn),
        in_specs=[
            pl.BlockSpec((SUBLANES, d), lambda l, j: (0, 0)),
            pl.BlockSpec((1, d, tn), lambda l, j: (l, 0, j)),
            pl.BlockSpec((1, 1, tn), lambda l, j: (l, 0, j)),
        ],
        out_specs=pl.BlockSpec((1, SUBLANES, tn), lambda l, j: (l, 0, j)),
        compiler_params=_params(("arbitrary", "arbitrary")),
        name="ada_modulation",
    )(c_rows, ada_w, ada_b.reshape(depth, 1, n))


def _rotate_pairs(v, cos, sin_next, sin_prev):
    return v * cos + pltpu.roll(v, LANES - 1, 1) * sin_next + pltpu.roll(v, 1, 1) * sin_prev


def _in_proj_kernel(x_ref, mod_ref, w_ref, gain_ref, seg_ref, rope_g_ref, rope_d_ref,
                    zl_ref, q_ref, k_ref, vt_ref, dq_ref, dk_ref, dvt_ref):
    tm = x_ref.shape[0]
    x = x_ref[...]
    shift = mod_ref[0, 0:1, :]
    scale = mod_ref[0, 1:2, :]
    h = (x * (1.0 + scale) + shift).astype(BF16)
    z = jnp.dot(h, w_ref[...], preferred_element_type=F32)
    zl_ref[0] = z[:, COL_AX:COL_GQ]

    ones_rows = jnp.where(lax.broadcasted_iota(I32, (VT_ROWS - GQA_HEAD_DIM, tm), 0) == 0, 1.0, 0.0).astype(BF16)

    seg = seg_ref[...]
    cos, s_next, s_prev = rope_g_ref[0], rope_g_ref[1], rope_g_ref[2]
    for g in range(5):
        v = z[:, COL_GQ + g * LANES:COL_GQ + (g + 1) * LANES]
        sq = v * v
        sq_hi = sq.astype(BF16)
        sq_lo = (sq - sq_hi.astype(F32)).astype(BF16)
        ssq = jnp.dot(sq_hi, seg, preferred_element_type=F32) + jnp.dot(sq_lo, seg, preferred_element_type=F32)
        v = v * lax.rsqrt(ssq * (1.0 / GQA_HEAD_DIM) + EPS) * gain_ref[g:g + 1, :]
        v = _rotate_pairs(v, cos, s_next, s_prev)
        if g < 4:
            v = (v * (GQA_HEAD_DIM ** -0.5 * LOG2_E)).astype(BF16)
            q_ref[0, 2 * g] = v[:, :GQA_HEAD_DIM]
            q_ref[0, 2 * g + 1] = v[:, GQA_HEAD_DIM:]
        else:
            v = v.astype(BF16)
            k_ref[0, 0] = v[:, :GQA_HEAD_DIM]
            k_ref[0, 1] = v[:, GQA_HEAD_DIM:]
    vt = z[:, COL_GV:COL_DQ].T.astype(BF16)
    for hd in range(GQA_KV_HEADS):
        vt_ref[0, hd, 0, 0:GQA_HEAD_DIM, :] = vt[hd * GQA_HEAD_DIM:(hd + 1) * GQA_HEAD_DIM]
        vt_ref[0, hd, 0, GQA_HEAD_DIM:VT_ROWS, :] = ones_rows

    cos, s_next, s_prev = rope_d_ref[0], rope_d_ref[1], rope_d_ref[2]
    for g in range(4):
        v = _rotate_pairs(z[:, COL_DQ + g * LANES:COL_DQ + (g + 1) * LANES], cos, s_next, s_prev)
        if g < 2:
            v = v * (DIFF_QK_DIM ** -0.5 * LOG2_E)
        v = v.astype(BF16)
        dst = dq_ref if g < 2 else dk_ref
        for j in range(4):
            dst[0, (g % 2) * 4 + j] = v[:, j * DIFF_QK_DIM:(j + 1) * DIFF_QK_DIM]
    dvt = z[:, COL_DV:IN_WIDTH].T.astype(BF16)
    for hd in range(DIFF_HEADS):
        dvt_ref[0, hd, 0, 0:DIFF_V_DIM, :] = dvt[hd * DIFF_V_DIM:(hd + 1) * DIFF_V_DIM]
        dvt_ref[0, hd, 0, DIFF_V_DIM:VT_ROWS, :] = ones_rows


def _in_proj(xc, bsz, sk, mod, w_in, gains, seg, rope_g, rope_d, n_ctx_tiles):
    d = xc.shape[1]
    tm = TOKEN_TILE
    nt = sk // tm
    n_mod_rows = mod.shape[0]

    def mod_map(b, i):
        return (jnp.where(i < n_ctx_tiles, bsz, b), 0, 0)

    assert n_mod_rows > bsz
    return pl.pallas_call(
        _in_proj_kernel,
        out_shape=(
            jax.ShapeDtypeStruct((bsz, sk, COL_GQ), F32),
            jax.ShapeDtypeStruct((bsz, GQA_HEADS, sk, GQA_HEAD_DIM), BF16),
            jax.ShapeDtypeStruct((bsz, GQA_KV_HEADS, sk, GQA_HEAD_DIM), BF16),
            jax.ShapeDtypeStruct((bsz, GQA_KV_HEADS, nt, VT_ROWS, tm), BF16),
            jax.ShapeDtypeStruct((bsz, 2 * DIFF_HEADS, sk, DIFF_QK_DIM), BF16),
            jax.ShapeDtypeStruct((bsz, 2 * DIFF_HEADS, sk, DIFF_QK_DIM), BF16),
            jax.ShapeDtypeStruct((bsz, DIFF_HEADS, nt, VT_ROWS, tm), BF16),
        ),
        grid=(bsz, nt),
        in_specs=[
            pl.BlockSpec((tm, d), lambda b, i: (b * nt + i, 0)),
            pl.BlockSpec((1, N_MOD, d), mod_map),
            pl.BlockSpec((d, IN_WIDTH), lambda b, i: (0, 0)),
            pl.BlockSpec((SUBLANES, LANES), lambda b, i: (0, 0)),
            pl.BlockSpec((LANES, LANES), lambda b, i: (0, 0)),
            pl.BlockSpec((3, tm, LANES), lambda b, i: (0, i, 0)),
            pl.BlockSpec((3, tm, LANES), lambda b, i: (0, i, 0)),
        ],
        out_specs=(
            pl.BlockSpec((1, tm, COL_GQ), lambda b, i: (b, i, 0)),
            pl.BlockSpec((1, GQA_HEADS, tm, GQA_HEAD_DIM), lambda b, i: (b, 0, i, 0)),
            pl.BlockSpec((1, GQA_KV_HEADS, tm, GQA_HEAD_DIM), lambda b, i: (b, 0, i, 0)),
            pl.BlockSpec((1, GQA_KV_HEADS, 1, VT_ROWS, tm), lambda b, i: (b, 0, i, 0, 0)),
            pl.BlockSpec((1, 2 * DIFF_HEADS, tm, DIFF_QK_DIM), lambda b, i: (b, 0, i, 0)),
            pl.BlockSpec((1, 2 * DIFF_HEADS, tm, DIFF_QK_DIM), lambda b, i: (b, 0, i, 0)),
            pl.BlockSpec((1, DIFF_HEADS, 1, VT_ROWS, tm), lambda b, i: (b, 0, i, 0, 0)),
        ),
        compiler_params=_params(("arbitrary", "arbitrary")),
        name="in_proj",
    )(xc, mod, w_in, gains, seg, rope_g, rope_d)


def _expm1(x):
    u = jnp.exp(x)
    return jnp.where(u == 1.0, x, (u - 1.0) * x / jnp.log(u))


def _lru_kernel(ax_ref, prev_ref, next_ref, cw_ref, cb_ref, w_ref, b_ref, lam_ref, h_ref, carry_ref, *, nt):
    tm = ax_ref.shape[1]
    d = pl.program_id(1)
    j = pl.program_id(2)
    blk = jnp.where(d == 0, j, jnp.where(j == 0, 0, nt - j))

    @pl.when(j == 0)
    def _():
        carry_ref[...] = jnp.zeros_like(carry_ref)

    x = ax_ref[0]
    no_left = jnp.logical_or(blk == 0, blk == 1)
    no_right = jnp.logical_or(blk == 0, blk == nt - 1)
    left = jnp.where(no_left, 0.0, prev_ref[0, SUBLANES - 1:SUBLANES, :])
    right0 = jnp.where(no_right, 0.0, next_ref[0, 0:1, :])
    right1 = jnp.where(no_right, 0.0, next_ref[0, 1:2, :])
    row = lax.broadcasted_iota(I32, (tm, 1), 0)
    x_m1 = jnp.where(row == 0, left, pltpu.roll(x, 1, 0))
    x_p1 = jnp.where(row == tm - 1, right0, pltpu.roll(x, tm - 1, 0))
    x_p2 = jnp.where(row == tm - 2, right0, jnp.where(row == tm - 1, right1, pltpu.roll(x, tm - 2, 0)))
    u = cb_ref[...] + x_m1 * cw_ref[0:1, :] + x * cw_ref[1:2, :] + x_p1 * cw_ref[2:3, :] + x_p2 * cw_ref[3:4, :]

    g = jnp.dot(u.astype(BF16), w_ref[0], preferred_element_type=F32) + b_ref[0]
    r = jax.nn.sigmoid(g[:, :LRU_WIDTH])
    gate_in = jax.nn.sigmoid(g[:, LRU_WIDTH:])
    neg_lam = -lam_ref[0]
    softplus = jnp.maximum(neg_lam, 0.0) + jnp.log1p(jnp.exp(-jnp.abs(neg_lam)))
    log_a = (-RG_C * r) * softplus
    a = jnp.exp(log_a)
    b = jnp.sqrt(-_expm1(2.0 * log_a)) * (gate_in * u)

    def scan(reverse):
        aa, bb = a, b
        s = 1
        while s < tm:
            shift = tm - s if reverse else s
            keep = (row < tm - s) if reverse else (row >= s)
            a_sh = jnp.where(keep, pltpu.roll(aa, shift, 0), 1.0)
            b_sh = jnp.where(keep, pltpu.roll(bb, shift, 0), 0.0)
            bb = aa * b_sh + bb
            aa = aa * a_sh
            s *= 2
        h = aa * carry_ref[...] + bb
        h_ref[0, 0] = h
        carry_ref[...] = h[0:1, :] if reverse else h[tm - 1:tm, :]

    @pl.when(d == 0)
    def _():
        scan(False)

    @pl.when(d == 1)
    def _():
        scan(True)


def _lru(zl, conv_w, conv_b, w_gates, b_gates, lam):
    bsz, sk, _ = zl.shape
    tm = TOKEN_TILE
    nt = sk // tm
    per = tm // SUBLANES

    def blk_of(d, j):
        return jnp.where(d == 0, j, jnp.where(j == 0, 0, nt - j))

    return pl.pallas_call(
        functools.partial(_lru_kernel, nt=nt),
        out_shape=jax.ShapeDtypeStruct((2, bsz, sk, LRU_WIDTH), F32),
        grid=(bsz, 2, nt),
        in_specs=[
            pl.BlockSpec((1, tm, LRU_WIDTH), lambda b, d, j: (b, blk_of(d, j), 0)),
            pl.BlockSpec((1, SUBLANES, LRU_WIDTH), lambda b, d, j: (b, jnp.maximum(blk_of(d, j) * per - 1, 0), 0)),
            pl.BlockSpec((1, SUBLANES, LRU_WIDTH),
                         lambda b, d, j: (b, jnp.minimum((blk_of(d, j) + 1) * per, nt * per - 1), 0)),
            pl.BlockSpec((CONV_W, LRU_WIDTH), lambda b, d, j: (0, 0)),
            pl.BlockSpec((1, LRU_WIDTH), lambda b, d, j: (0, 0)),
            pl.BlockSpec((1, LRU_WIDTH, 2 * LRU_WIDTH), lambda b, d, j: (d, 0, 0)),
            pl.BlockSpec((1, 1, 2 * LRU_WIDTH), lambda b, d, j: (d, 0, 0)),
            pl.BlockSpec((1, 1, LRU_WIDTH), lambda b, d, j: (d, 0, 0)),
        ],
        out_specs=pl.BlockSpec((1, 1, tm, LRU_WIDTH), lambda b, d, j: (d, b, blk_of(d, j), 0)),
        scratch_shapes=[pltpu.VMEM((1, LRU_WIDTH), F32)],
        compiler_params=_params(("arbitrary", "arbitrary", "arbitrary")),
        name="rg_lru",
    )(zl, zl, zl, conv_w, conv_b, w_gates, b_gates, lam)


def _attn_kernel(q_ref, k_ref, vt_ref, aux_ref, o_ref, s_ref, m_ref, acc_ref, *, k_heads, v_heads, nt, n_ctx_tiles, diff):
    tq = q_ref.shape[2]
    tk = tq
    dv = GQA_HEAD_DIM
    qi = pl.program_id(2)
    n_k = jnp.where(qi < n_ctx_tiles, n_ctx_tiles, nt)

    def scores(kt):
        start = pl.multiple_of(kt * tk, tk)
        return jnp.concatenate(
            [lax.dot_general(k_ref[0, k_heads[j], pl.ds(start, tk), :], q_ref[0, j], NT_DIMS,
                             preferred_element_type=F32) for j in range(4)], axis=1)

    def weighted_values(kt, p):
        return jnp.concatenate(
            [jnp.dot(vt_ref[0, v_heads[j], kt], p[:, j * tq:(j + 1) * tq], preferred_element_type=F32)
             for j in range(4)], axis=1)

    def col_max(s, m):
        return jnp.maximum(m, jnp.max(s, axis=0, keepdims=True))

    m_ref[...] = jnp.full_like(m_ref, -jnp.inf)
    acc_ref[...] = jnp.zeros_like(acc_ref)
    s_ref[...] = scores(0)

    def trip(i):
        kt = KEY_TILES_PER_TRIP * i
        m0 = m_ref[...]
        s_cur = s_ref[...]
        peak = jnp.zeros((1, 4 * tq), BF16)
        acc = acc_ref[...]
        for u in range(KEY_TILES_PER_TRIP):
            s_next = scores(kt + u + 1)
            p = jnp.exp2(s_cur - m0).astype(BF16)
            peak = jnp.maximum(peak, jnp.max(p, axis=0, keepdims=True))
            acc = acc + weighted_values(kt + u, p)
            s_cur = s_next
        peak = peak.astype(F32)
        lagging = jnp.max(peak) > 2.0 ** MAX_SHIFT_LAG

        @pl.when(jnp.logical_not(lagging))
        def _():
            acc_ref[...] = acc
            s_ref[...] = s_cur

        @pl.when(lagging)
        def _():
            m_new = m0 + jnp.clip(jnp.log(peak) * LOG2_E, 0.0, MAX_SHIFT_RAISE)
            acc_ref[...] = acc_ref[...] * jnp.exp2(m0 - m_new)
            m_ref[...] = m_new

        return jnp.where(lagging, i, i + 1)

    m_ref[...] = col_max(s_ref[...], m_ref[...])
    n_trips = (n_k - 1) // KEY_TILES_PER_TRIP
    lax.while_loop(lambda i: i < n_trips, trip, jnp.int32(0))

    s0 = s_ref[...]
    m0 = m_ref[...]
    m1 = col_max(s0, m0)
    p0 = jnp.exp2(s0 - m1).astype(BF16)
    acc_all = acc_ref[...] * jnp.exp2(m0 - m1) + weighted_values(n_k - 1, p0)

    outs = []
    for j in range(4):
        acc = acc_all[:, j * tq:(j + 1) * tq]
        outs.append(acc[0:dv] / acc[dv:dv + 1])
    if diff:
        dl = aux_ref[0:4, :]
        lam_init = aux_ref[4:5, 0:1]
        lam = (jnp.exp(jnp.sum(dl[0:1] * dl[1:2], axis=1, keepdims=True))
               - jnp.exp(jnp.sum(dl[2:3] * dl[3:4], axis=1, keepdims=True)) + lam_init)
        heads = []
        for hd in range(2):
            o = outs[2 * hd] - lam * outs[2 * hd + 1]
            heads.append(o * lax.rsqrt(jnp.mean(o * o, axis=0, keepdims=True) + EPS))
        o_ref[0] = ((jnp.concatenate(heads, axis=0).T * aux_ref[5:6, :]) * (1.0 - lam_init)).astype(o_ref.dtype)
    else:
        for pair in range(2):
            o_ref[0, :, pair * 2 * dv:(pair + 1) * 2 * dv] = (
                jnp.concatenate(outs[2 * pair:2 * pair + 2], axis=0).T.astype(o_ref.dtype))


def _attention(q, k, vt, aux, *, diff, n_ctx_tiles):
    bsz, n_q, sk, dh = q.shape
    n_k_heads = k.shape[1]
    n_v_heads = vt.shape[1]
    tq = TOKEN_TILE
    nt = sk // tq
    assert (nt - 1) % KEY_TILES_PER_TRIP == 0 and (n_ctx_tiles - 1) % KEY_TILES_PER_TRIP == 0
    groups = n_q // 4
    kpg = n_k_heads // groups
    vpg = n_v_heads // groups
    k_heads = tuple(j * kpg // 4 for j in range(4))
    v_heads = tuple(j * vpg // 4 for j in range(4))
    width = 2 * DIFF_V_DIM if diff else 4 * GQA_HEAD_DIM
    return pl.pallas_call(
        functools.partial(_attn_kernel, k_heads=k_heads, v_heads=v_heads, nt=nt, n_ctx_tiles=n_ctx_tiles, diff=diff),
        out_shape=jax.ShapeDtypeStruct((bsz, sk, groups * width), BF16),
        grid=(bsz, groups, nt),
        in_specs=[
            pl.BlockSpec((1, 4, tq, dh), lambda b, g, i: (b, g, i, 0)),
            pl.BlockSpec((1, kpg, sk, dh), lambda b, g, i: (b, g, 0, 0)),
            pl.BlockSpec((1, vpg, nt, VT_ROWS, tq), lambda b, g, i: (b, g, 0, 0, 0)),
            pl.BlockSpec((SUBLANES, LANES), lambda b, g, i: (0, 0)),
        ],
        out_specs=pl.BlockSpec((1, tq, width), lambda b, g, i: (b, i, g)),
        scratch_shapes=[pltpu.VMEM((tq, 4 * tq), F32), pltpu.VMEM((1, 4 * tq), F32),
                        pltpu.VMEM((VT_ROWS, 4 * tq), F32)],
        compiler_params=_params(("arbitrary", "arbitrary", "arbitrary")),
        name="diff_attention" if diff else "gqa_attention",
    )(q, k, vt, aux)


def _layer_norm(r, g, b):
    mu = jnp.mean(r, axis=-1, keepdims=True)
    c = r - mu
    var = jnp.mean(c * c, axis=-1, keepdims=True)
    return c * lax.rsqrt(var + EPS) * g + b


def _route(logits, bias):
    mx = jnp.max(logits, axis=0, keepdims=True)
    e = jnp.exp(logits - mx)
    probs = e / jnp.sum(e, axis=0, keepdims=True)
    sel = probs + bias
    epg = EXPERTS_PER_GROUP
    in_top2, scores = [], []
    for g in range(N_GROUPS):
        v = [sel[g * epg + i:g * epg + i + 1] for i in range(epg)]
        masks = []
        for i in range(epg):
            rank = jnp.zeros_like(v[i])
            for j in range(epg):
                if j != i:
                    beats = (v[j] > v[i]) if j > i else (v[j] >= v[i])
                    rank = rank + beats.astype(F32)
            masks.append(rank < 2.0)
        in_top2.append(masks)
        scores.append(sum(jnp.where(masks[i], v[i], 0.0) for i in range(epg)))
    chosen = []
    for g in range(N_GROUPS):
        c = None
        for j in range(N_GROUPS):
            if j != g:
                t = (scores[j] < scores[g]) if j < g else (scores[j] <= scores[g])
                c = t if c is None else jnp.logical_and(c, t)
        chosen.append(c)
    picked, weight = [], []
    for i in range(epg):
        m = None
        w = jnp.zeros_like(scores[0])
        for g in range(N_GROUPS):
            t = jnp.logical_and(chosen[g], in_top2[g][i])
            m = t if m is None else jnp.logical_or(m, t)
            w = w + jnp.where(t, probs[g * epg + i:g * epg + i + 1], 0.0)
        picked.append(m)
        weight.append(w)
    total = weight[0] + weight[1] + weight[2] + weight[3]
    gate = [w / total for w in weight]
    lo = jnp.where(picked[0], 0.0, jnp.where(picked[1], 1.0, 2.0))
    hi = jnp.where(picked[3], 3.0, jnp.where(picked[2], 2.0, 1.0))
    gate_lo = jnp.where(picked[0], gate[0], jnp.where(picked[1], gate[1], gate[2]))
    gate_hi = jnp.where(picked[3], gate[3], jnp.where(picked[2], gate[2], gate[1]))
    group = sum(jnp.where(chosen[g], float(g), 0.0) for g in range(N_GROUPS))
    base = jnp.where(lo == 0.0, 0.0, jnp.where(lo == 1.0, 3.0, 5.0))
    cls = group * float(len(PAIRS_LO)) + base + hi - lo - 1.0
    return cls.astype(I32), gate_lo, gate_hi


def _out_proj_kernel(hf_ref, hb_ref, ag_ref, yb_ref, yc_ref, x_ref, mod_ref, w_ref, lng_ref, lnb_ref,
                     rw_ref, rb_ref, x1_ref, ri_ref, cnt_ref, run_ref, *, alpha, n_ctx_tiles, n_batch):
    tm = x_ref.shape[0]
    ya = (jax.nn.gelu(ag_ref[0]) * (hf_ref[0, 0] + hb_ref[0, 0])).astype(BF16)
    n_a = LRU_WIDTH
    n_b = n_a + GQA_HEADS * GQA_HEAD_DIM
    y = (jnp.dot(ya, w_ref[0:n_a, :], preferred_element_type=F32)
         + jnp.dot(yb_ref[0], w_ref[n_a:n_b, :], preferred_element_type=F32)
         + jnp.dot(yc_ref[0], w_ref[n_b:, :], preferred_element_type=F32))
    x1 = _layer_norm(alpha * x_ref[...] + mod_ref[0, 2:3, :] * y, lng_ref[...], lnb_ref[...])
    d = x1.shape[1]
    x1_ref[:, 0:d] = x1

    h2 = x1 * (1.0 + mod_ref[0, 4:5, :]) + mod_ref[0, 3:4, :]
    h_hi = h2.astype(BF16)
    h_lo = (h2 - h_hi.astype(F32)).astype(BF16)
    both = lax.dot_general(rw_ref[...], h_hi, NT_DIMS, preferred_element_type=F32)
    logits = (both[0:N_EXPERTS] + both[N_EXPERTS:]
              + lax.dot_general(rw_ref[0:N_EXPERTS, :], h_lo, NT_DIMS, preferred_element_type=F32))
    cls, gate_lo, gate_hi = _route(logits, rb_ref[...])

    b = pl.program_id(0)
    i = pl.program_id(1)
    mod_row = jnp.where(i < n_ctx_tiles, n_batch, b).astype(F32)
    extra = jnp.concatenate([gate_lo, gate_hi, jnp.full((1, tm), mod_row, F32), jnp.zeros((LANES - 3, tm), F32)], axis=0)
    x1_ref[:, d:] = extra.T

    @pl.when(jnp.logical_and(b == 0, i == 0))
    def _():
        run_ref[...] = jnp.zeros_like(run_ref)

    member = lax.broadcasted_iota(I32, (CLASS_ROWS, tm), 0) == cls
    before = lax.broadcasted_iota(I32, (tm, tm), 0) < lax.broadcasted_iota(I32, (tm, tm), 1)
    earlier = jnp.dot(member.astype(BF16), before.astype(BF16), preferred_element_type=F32)
    run = run_ref[...]
    rank = jnp.sum(jnp.where(member, earlier + run, 0.0), axis=0, keepdims=True).astype(I32)
    run = run + jnp.sum(member.astype(F32), axis=1, keepdims=True)
    run_ref[...] = run
    cnt_ref[...] = jnp.broadcast_to(run, cnt_ref.shape).astype(I32)
    ri_ref[0] = jnp.concatenate([cls, rank, jnp.zeros((SUBLANES - 2, tm), I32)], axis=0)


def _out_proj(h_lru, zl, yb, yc, xc, mod, w_out, ln_g, ln_b, rw_split, rb, alpha, n_ctx_tiles):
    bsz, sk, _ = zl.shape
    d = xc.shape[1]
    tm = TOKEN_TILE
    nt = sk // tm

    def mod_map(b, i):
        return (jnp.where(i < n_ctx_tiles, bsz, b), 0, 0)

    return pl.pallas_call(
        functools.partial(_out_proj_kernel, alpha=alpha, n_ctx_tiles=n_ctx_tiles, n_batch=bsz),
        out_shape=(
            jax.ShapeDtypeStruct((bsz * sk, d + LANES), F32),
            jax.ShapeDtypeStruct((bsz, SUBLANES, sk), I32),
            jax.ShapeDtypeStruct((CLASS_ROWS, LANES), I32),
        ),
        grid=(bsz, nt),
        in_specs=[
            pl.BlockSpec((1, 1, tm, LRU_WIDTH), lambda b, i: (0, b, i, 0)),
            pl.BlockSpec((1, 1, tm, LRU_WIDTH), lambda b, i: (1, b, i, 0)),
            pl.BlockSpec((1, tm, LRU_WIDTH), lambda b, i: (b, i, 1)),
            pl.BlockSpec((1, tm, GQA_HEADS * GQA_HEAD_DIM), lambda b, i: (b, i, 0)),
            pl.BlockSpec((1, tm, DIFF_HEADS * DIFF_V_DIM), lambda b, i: (b, i, 0)),
            pl.BlockSpec((tm, d), lambda b, i: (b * nt + i, 0)),
            pl.BlockSpec((1, N_MOD, d), mod_map),
            pl.BlockSpec((d, d), lambda b, i: (0, 0)),
            pl.BlockSpec((1, d), lambda b, i: (0, 0)),
            pl.BlockSpec((1, d), lambda b, i: (0, 0)),
            pl.BlockSpec((2 * N_EXPERTS, d), lambda b, i: (0, 0)),
            pl.BlockSpec((N_EXPERTS, 1), lambda b, i: (0, 0)),
        ],
        out_specs=(
            pl.BlockSpec((tm, d + LANES), lambda b, i: (b * nt + i, 0)),
            pl.BlockSpec((1, SUBLANES, tm), lambda b, i: (b, 0, i)),
            pl.BlockSpec((CLASS_ROWS, LANES), lambda b, i: (0, 0)),
        ),
        scratch_shapes=[pltpu.VMEM((CLASS_ROWS, 1), F32)],
        compiler_params=_params(("arbitrary", "arbitrary")),
        name="out_proj_router",
    )(h_lru, h_lru, zl, yb, yc, xc, mod, w_out, ln_g, ln_b, rw_split, rb)


def _moe_kernel(cls_ref, rank_ref, pstart_ref, ea_ref, eb_ref, nv_ref, nu_ref,
                x_hbm, mod_ref, lng_ref, lnb_ref,
                wga_ref, wua_ref, wda_ref, wgb_ref, wub_ref, wdb_ref,
                o_hbm, tok_ref, xbuf, obuf, gsem, ssem, *, alpha, n_batch, n_tok):
    rows = xbuf.shape[1]
    d = obuf.shape[2]
    i = pl.program_id(0)
    n_steps = pl.num_programs(0)
    n_used = nu_ref[0]
    slot = i % 2

    @pl.when(i == 0)
    def _():
        def clear(r, c):
            tok_ref[r] = 0
            return c
        lax.fori_loop(0, tok_ref.shape[0], clear, 0, unroll=8)

        def place(t, c):
            tok_ref[pstart_ref[cls_ref[t]] + rank_ref[t]] = t
            return c
        lax.fori_loop(0, n_tok, place, 0, unroll=8)

        obuf[...] = jnp.zeros_like(obuf)
        for s in range(2):
            spare = pltpu.make_async_copy(obuf.at[s], o_hbm.at[pl.ds(n_tok + s * rows, rows)], ssem.at[s])
            spare.start()
            spare.wait()

    def start_gather(step, slot_):
        for r in range(rows):
            pltpu.make_async_copy(x_hbm.at[pl.ds(tok_ref[step * rows + r], 1)], xbuf.at[slot_, pl.ds(r, 1)],
                                  gsem.at[slot_]).start()

    def wait_gather(slot_):
        pltpu.make_async_copy(x_hbm.at[pl.ds(0, rows)], xbuf.at[slot_], gsem.at[slot_]).wait()

    def start_scatter(step, slot_):
        n_valid = nv_ref[step]
        for r in range(rows):
            dst = jnp.where(r < n_valid, tok_ref[step * rows + r], n_tok + slot_ * rows + r)
            pltpu.make_async_copy(obuf.at[slot_, pl.ds(r, 1)], o_hbm.at[pl.ds(dst, 1)], ssem.at[slot_]).start()

    def wait_scatter(slot_):
        pltpu.make_async_copy(obuf.at[slot_], o_hbm.at[pl.ds(0, rows)], ssem.at[slot_]).wait()

    @pl.when(i == 0)
    def _():
        start_gather(0, 0)

    @pl.when(i < n_used)
    def _():
        wait_gather(slot)

        @pl.when(i + 1 < n_used)
        def _():
            start_gather(i + 1, 1 - slot)

        @pl.when(i >= 2)
        def _():
            wait_scatter(slot)

        x = xbuf[slot, :, 0:d]
        gate_a = xbuf[slot, :, d:d + 1]
        gate_b = xbuf[slot, :, d + 1:d + 2]
        mid = xbuf[slot, :, d + 2:d + 3]

        def pick(kk):
            v = mod_ref[n_batch, kk:kk + 1, :]
            for b in range(n_batch):
                v = jnp.where(mid == float(b), mod_ref[b, kk:kk + 1, :], v)
            return v

        h = x * (1.0 + pick(4)) + pick(3)

        def ffn(wg, wu, wd):
            a = jnp.dot(h, wg[0], preferred_element_type=F32)
            u = jnp.dot(h, wu[0], preferred_element_type=F32)
            return jnp.dot((a * jax.nn.sigmoid(a)) * u, wd[0], preferred_element_type=F32)

        f = ffn(wga_ref, wua_ref, wda_ref) * gate_a + ffn(wgb_ref, wub_ref, wdb_ref) * gate_b
        obuf[slot] = _layer_norm(alpha * x + pick(5) * f, lng_ref[...], lnb_ref[...])
        start_scatter(i, slot)

    @pl.when(i == n_steps - 1)
    def _():
        wait_scatter(0)

        @pl.when(n_used >= 2)
        def _():
            wait_scatter(1)


def _moe(x_wide, n_tok, cls, rank, pstart, blk_ea, blk_eb, blk_nv, n_used, mod, ln_g, ln_b,
         w_gate, w_up, w_down, layer, alpha, n_batch):
    d = x_wide.shape[1] - LANES
    rows = MOE_ROWS
    n_blk = blk_ea.shape[0]
    ff = w_gate.shape[-1]
    first = layer * N_EXPERTS

    def wa(i, cl, rk, ps, ea, eb, nv, nu):
        return (first + ea[i], 0, 0)

    def wb(i, cl, rk, ps, ea, eb, nv, nu):
        return (first + eb[i], 0, 0)

    const2 = lambda i, cl, rk, ps, ea, eb, nv, nu: (0, 0)
    grid_spec = pltpu.PrefetchScalarGridSpec(
        num_scalar_prefetch=7,
        grid=(n_blk,),
        in_specs=[
            pl.BlockSpec(memory_space=pl.ANY),
            pl.BlockSpec(mod.shape, lambda i, cl, rk, ps, ea, eb, nv, nu: (0, 0, 0)),
            pl.BlockSpec((1, d), const2),
            pl.BlockSpec((1, d), const2),
            pl.BlockSpec((1, d, ff), wa),
            pl.BlockSpec((1, d, ff), wa),
            pl.BlockSpec((1, ff, d), wa),
            pl.BlockSpec((1, d, ff), wb),
            pl.BlockSpec((1, d, ff), wb),
            pl.BlockSpec((1, ff, d), wb),
        ],
        out_specs=pl.BlockSpec(memory_space=pl.ANY),
        scratch_shapes=[
            pltpu.SMEM((n_blk * rows,), I32),
            pltpu.VMEM((2, rows, d + LANES), F32),
            pltpu.VMEM((2, rows, d), F32),
            pltpu.SemaphoreType.DMA((2,)),
            pltpu.SemaphoreType.DMA((2,)),
        ],
    )
    return pl.pallas_call(
        functools.partial(_moe_kernel, alpha=alpha, n_batch=n_batch, n_tok=n_tok),
        out_shape=jax.ShapeDtypeStruct((n_tok + 2 * rows, d), F32),
        grid_spec=grid_spec,
        compiler_params=_params(("arbitrary",)),
        name="moe_experts",
    )(cls, rank, pstart, blk_ea, blk_eb, blk_nv, n_used, x_wide, mod, ln_g, ln_b,
      w_gate, w_up, w_down, w_gate, w_up, w_down)


def _block_tables(counts, n_tok):
    rows = MOE_ROWS
    n_blk = n_tok // rows + N_CLASSES
    n_pairs = len(PAIRS_LO)
    padded = (counts + rows - 1) // rows * rows
    pend = jnp.cumsum(padded)
    pstart = pend - padded
    n_used = pend[-1] // rows
    blk = jnp.arange(n_blk, dtype=I32)
    blk_ids = jnp.minimum(blk, jnp.maximum(n_used - 1, 0))
    blk_cls = jnp.minimum(jnp.sum((blk_ids[:, None] * rows >= pend[None, :]).astype(I32), axis=1), N_CLASSES - 1)
    onehot = blk_cls[:, None] == jnp.arange(N_CLASSES, dtype=I32)[None, :]
    within = blk_ids * rows - jnp.sum(jnp.where(onehot, pstart[None, :], 0), axis=1)
    count_b = jnp.sum(jnp.where(onehot, counts[None, :], 0), axis=1)
    blk_nv = jnp.where(blk < n_used, jnp.clip(count_b - within, 0, rows), 0).astype(I32)
    pair = blk_cls % n_pairs
    lo = sum(jnp.where(pair == p, PAIRS_LO[p], 0) for p in range(n_pairs))
    hi = sum(jnp.where(pair == p, PAIRS_HI[p], 0) for p in range(n_pairs))
    first = blk_cls // n_pairs * EXPERTS_PER_GROUP
    pstart_pad = jnp.concatenate([pstart, jnp.zeros((CLASS_ROWS - N_CLASSES,), I32)]).astype(I32)
    return (pstart_pad, (first + lo).astype(I32), (first + hi).astype(I32), blk_nv,
            n_used.reshape(1).astype(I32))


def _rope_tables(n_lat, n_ctx, dim):
    rows = n_lat // GRID_W
    row = jnp.repeat(jnp.arange(rows, dtype=F32), GRID_W)
    col = jnp.tile(jnp.arange(GRID_W, dtype=F32), rows)
    n_freq = dim // 4
    inv = ROPE_THETA ** (-jnp.arange(n_freq, dtype=F32) / n_freq)
    ang = jnp.concatenate([row[:, None] * inv, col[:, None] * inv], axis=-1)
    cos = jnp.repeat(jnp.cos(ang), 2, axis=-1)
    sin = jnp.repeat(jnp.sin(ang), 2, axis=-1)
    even = (jnp.arange(dim) % 2 == 0)
    tabs = jnp.stack([cos, jnp.where(even, -sin, 0.0), jnp.where(even, 0.0, sin)])
    ident = jnp.stack([jnp.ones((n_ctx, dim), F32), jnp.zeros((n_ctx, dim), F32), jnp.zeros((n_ctx, dim), F32)])
    return jnp.tile(jnp.concatenate([ident, tabs], axis=1), (1, 1, LANES // dim))


def _block_diag(w):
    n, c, _ = w.shape
    eye = jnp.eye(n, dtype=w.dtype)
    return (eye[:, None, :, None] * w[:, :, None, :]).reshape(n * c, n * c)


def kernel(x, c, ctx, c_ctx, ada_w, ada_b, w_in, conv_w, conv_b, rg_wa, rg_ba, rg_wx, rg_bx, rg_lam, q_norm_g, k_norm_g, diff_lambda, diff_subln_g, w_out, ln1_g, ln1_b, ln2_g, ln2_b, router_w, router_b, exp_w_gate, exp_w_up, exp_w_down):
    bsz, s, d = x.shape
    n_ctx = ctx.shape[1]
    depth = w_in.shape[0]
    sk = n_ctx + s
    assert d == D_MODEL and n_ctx == TOKEN_TILE and s % TOKEN_TILE == 0 and bsz + 1 <= SUBLANES
    n_ctx_tiles = n_ctx // TOKEN_TILE
    n_tok = bsz * sk
    alpha = (2.0 * depth) ** 0.25

    c_rows = jnp.concatenate([c, c_ctx[None, :], jnp.zeros((SUBLANES - bsz - 1, d), F32)], axis=0)
    mods = _ada_modulation(c_rows, ada_w, ada_b).reshape(depth, SUBLANES, N_MOD, d)

    rope_g = _rope_tables(s, n_ctx, GQA_HEAD_DIM)
    rope_d = _rope_tables(s, n_ctx, DIFF_QK_DIM)
    lane = np.arange(LANES)
    seg = jnp.asarray(lane[:, None] // GQA_HEAD_DIM == lane[None, :] // GQA_HEAD_DIM, BF16)
    rw_t = router_w.T
    rw_hi = rw_t.astype(BF16)
    rw_split = jnp.concatenate([rw_hi, (rw_t - rw_hi.astype(F32)).astype(BF16)], axis=0)
    rb = router_b.reshape(N_EXPERTS, 1)
    zero_aux = jnp.zeros((SUBLANES, LANES), F32)

    xc = jnp.concatenate([jnp.concatenate([ctx, x], axis=1).reshape(n_tok, d),
                          jnp.zeros((2 * MOE_ROWS, d), F32)], axis=0)
    w_gate_all = exp_w_gate.reshape(depth * N_EXPERTS, d, EXPERT_FF)
    w_up_all = exp_w_up.reshape(depth * N_EXPERTS, d, EXPERT_FF)
    w_down_all = exp_w_down.reshape(depth * N_EXPERTS, EXPERT_FF, d)
    for li in range(depth):
        lam_init = 0.8 - 0.6 * math.exp(-0.3 * li)
        mod = mods[li]
        gains = jnp.concatenate([jnp.tile(jnp.tile(q_norm_g[li], 2)[None, :], (4, 1)),
                                 jnp.tile(k_norm_g[li], 2)[None, :], jnp.zeros((3, LANES), F32)], axis=0)
        zl, q, k, vt, dq, dk, dvt = _in_proj(xc, bsz, sk, mod, w_in[li].astype(BF16), gains, seg, rope_g, rope_d,
                                             n_ctx_tiles)

        w_gates = jnp.stack([jnp.concatenate([_block_diag(rg_wa[li, dd]), _block_diag(rg_wx[li, dd])], axis=1)
                             for dd in range(2)]).astype(BF16)
        b_gates = jnp.concatenate([rg_ba[li], rg_bx[li]], axis=-1)[:, None, :]
        h_lru = _lru(zl, conv_w[li], conv_b[li][None, :], w_gates, b_gates, rg_lam[li][:, None, :])

        yb = _attention(q, k, vt, zero_aux, diff=False, n_ctx_tiles=n_ctx_tiles)
        aux = jnp.concatenate([
            jnp.pad(diff_lambda[li], ((0, 0), (0, LANES - DIFF_QK_DIM))),
            jnp.full((1, LANES), lam_init, F32),
            jnp.tile(diff_subln_g[li], 2)[None, :],
            jnp.zeros((2, LANES), F32)], axis=0)
        yc = _attention(dq, dk, dvt, aux, diff=True, n_ctx_tiles=n_ctx_tiles)

        x1, route_i, counts = _out_proj(h_lru, zl, yb, yc, xc, mod, w_out[li].astype(BF16), ln1_g[li][None, :],
                                        ln1_b[li][None, :], rw_split, rb, alpha, n_ctx_tiles)

        pstart, blk_ea, blk_eb, blk_nv, n_used = _block_tables(counts[:N_CLASSES, 0], n_tok)
        xc = _moe(x1, n_tok, route_i[:, 0, :].reshape(n_tok), route_i[:, 1, :].reshape(n_tok), pstart,
                  blk_ea, blk_eb, blk_nv, n_used, mod, ln2_g[li][None, :], ln2_b[li][None, :],
                  w_gate_all, w_up_all, w_down_all, li, alpha, bsz)
    return xc[:n_tok].reshape(bsz, sk, d)[:, n_ctx:, :]
```

```python
import functools
import math

import numpy as np
import jax
import jax.numpy as jnp
from jax import lax
from jax.experimental import pallas as pl
from jax.experimental.pallas import tpu as pltpu

F32 = jnp.float32
BF16 = jnp.bfloat16
I32 = jnp.int32

D_MODEL = 1024
GRID_W = 64
LRU_WIDTH = 256
LRU_BLOCKS = 4
CONV_W = 4
RG_C = 8.0
GQA_HEADS = 8
GQA_KV_HEADS = 2
GQA_HEAD_DIM = 64
DIFF_HEADS = 4
DIFF_QK_DIM = 32
DIFF_V_DIM = 64
IN_WIDTH = 2048
N_EXPERTS = 16
N_GROUPS = 4
EXPERTS_PER_GROUP = 4
EXPERT_FF = 512
ROPE_THETA = 10000.0
EPS = 1e-6
N_MOD = 6
LOG2_E = math.log2(math.e)

COL_AX, COL_AG, COL_GQ, COL_GK, COL_GV, COL_DQ, COL_DK, COL_DV = 0, 256, 512, 1024, 1152, 1280, 1536, 1792

LANES = 128
SUBLANES = 8
VMEM_LIMIT = 56 * 1024 * 1024

TOKEN_TILE = 256
VT_ROWS = 80
PAIRS_LO = (0, 0, 0, 1, 1, 2)
PAIRS_HI = (1, 2, 3, 2, 3, 3)
N_CLASSES = N_GROUPS * len(PAIRS_LO)
CLASS_ROWS = 32
MOE_ROWS = 128
KEY_TILES_PER_TRIP = 8
MAX_SHIFT_LAG = 60.0
MAX_SHIFT_RAISE = 120.0

NT_DIMS = (((1,), (1,)), ((), ()))


def _params(semantics):
    return pltpu.CompilerParams(dimension_semantics=semantics, vmem_limit_bytes=VMEM_LIMIT)


def _ada_kernel(c_ref, w_ref, b_ref, o_ref):
    c = c_ref[...]
    o_ref[0] = jnp.dot(c * jax.nn.sigmoid(c), w_ref[0], preferred_element_type=F32) + b_ref[0]


def _ada_modulation(c_rows, ada_w, ada_b):
    depth, d, n = ada_w.shape
    tn = 1536
    return pl.pallas_call(
        _ada_kernel,
        out_shape=jax.ShapeDtypeStruct((depth, SUBLANES, n), F32),
        grid=(depth, n // tn),
        in_specs=[
            pl.BlockSpec((SUBLANES, d), lambda l, j: (0, 0)),
            pl.BlockSpec((1, d, tn), lambda l, j: (l, 0, j)),
            pl.BlockSpec((1, 1, tn), lambda l, j: (l, 0, j)),
        ],
        out_specs=pl.BlockSpec((1, SUBLANES, tn), lambda l, j: (l, 0, j)),
        compiler_params=_params(("arbitrary", "arbitrary")),
        name="ada_modulation",
    )(c_rows, ada_w, ada_b.reshape(depth, 1, n))


def _rotate_pairs(v, cos, sin_next, sin_prev):
    return v * cos + pltpu.roll(v, LANES - 1, 1) * sin_next + pltpu.roll(v, 1, 1) * sin_prev


def _in_proj_kernel(x_ref, mod_ref, w_ref, gain_ref, seg_ref, rope_g_ref, rope_d_ref,
                    zl_ref, q_ref, k_ref, vt_ref, dq_ref, dk_ref, dvt_ref):
    tm = x_ref.shape[0]
    x = x_ref[...]
    shift = mod_ref[0, 0:1, :]
    scale = mod_ref[0, 1:2, :]
    h = (x * (1.0 + scale) + shift).astype(BF16)
    z = jnp.dot(h, w_ref[...], preferred_element_type=F32)
    zl_ref[0] = z[:, COL_AX:COL_GQ]

    ones_rows = jnp.where(lax.broadcasted_iota(I32, (VT_ROWS - GQA_HEAD_DIM, tm), 0) == 0, 1.0, 0.0).astype(BF16)

    seg = seg_ref[...]
    cos, s_next, s_prev = rope_g_ref[0], rope_g_ref[1], rope_g_ref[2]
    for g in range(5):
        v = z[:, COL_GQ + g * LANES:COL_GQ + (g + 1) * LANES]
        sq = v * v
        sq_hi = sq.astype(BF16)
        sq_lo = (sq - sq_hi.astype(F32)).astype(BF16)
        ssq = jnp.dot(sq_hi, seg, preferred_element_type=F32) + jnp.dot(sq_lo, seg, preferred_element_type=F32)
        v = v * lax.rsqrt(ssq * (1.0 / GQA_HEAD_DIM) + EPS) * gain_ref[g:g + 1, :]
        v = _rotate_pairs(v, cos, s_next, s_prev)
        if g < 4:
            v = (v * (GQA_HEAD_DIM ** -0.5 * LOG2_E)).astype(BF16)
            q_ref[0, 2 * g] = v[:, :GQA_HEAD_DIM]
            q_ref[0, 2 * g + 1] = v[:, GQA_HEAD_DIM:]
        else:
            v = v.astype(BF16)
            k_ref[0, 0] = v[:, :GQA_HEAD_DIM]
            k_ref[0, 1] = v[:, GQA_HEAD_DIM:]
    vt = z[:, COL_GV:COL_DQ].T.astype(BF16)
    for hd in range(GQA_KV_HEADS):
        vt_ref[0, hd, 0, 0:GQA_HEAD_DIM, :] = vt[hd * GQA_HEAD_DIM:(hd + 1) * GQA_HEAD_DIM]
        vt_ref[0, hd, 0, GQA_HEAD_DIM:VT_ROWS, :] = ones_rows

    cos, s_next, s_prev = rope_d_ref[0], rope_d_ref[1], rope_d_ref[2]
    for g in range(4):
        v = _rotate_pairs(z[:, COL_DQ + g * LANES:COL_DQ + (g + 1) * LANES], cos, s_next, s_prev)
        if g < 2:
            v = v * (DIFF_QK_DIM ** -0.5 * LOG2_E)
        v = v.astype(BF16)
        dst = dq_ref if g < 2 else dk_ref
        for j in range(4):
            dst[0, (g % 2) * 4 + j] = v[:, j * DIFF_QK_DIM:(j + 1) * DIFF_QK_DIM]
    dvt = z[:, COL_DV:IN_WIDTH].T.astype(BF16)
    for hd in range(DIFF_HEADS):
        dvt_ref[0, hd, 0, 0:DIFF_V_DIM, :] = dvt[hd * DIFF_V_DIM:(hd + 1) * DIFF_V_DIM]
        dvt_ref[0, hd, 0, DIFF_V_DIM:VT_ROWS, :] = ones_rows


def _in_proj(xc, bsz, sk, mod, w_in, gains, seg, rope_g, rope_d, n_ctx_tiles):
    d = xc.shape[1]
    tm = TOKEN_TILE
    nt = sk // tm
    n_mod_rows = mod.shape[0]

    def mod_map(b, i):
        return (jnp.where(i < n_ctx_tiles, bsz, b), 0, 0)

    assert n_mod_rows > bsz
    return pl.pallas_call(
        _in_proj_kernel,
        out_shape=(
            jax.ShapeDtypeStruct((bsz, sk, COL_GQ), F32),
            jax.ShapeDtypeStruct((bsz, GQA_HEADS, sk, GQA_HEAD_DIM), BF16),
            jax.ShapeDtypeStruct((bsz, GQA_KV_HEADS, sk, GQA_HEAD_DIM), BF16),
            jax.ShapeDtypeStruct((bsz, GQA_KV_HEADS, nt, VT_ROWS, tm), BF16),
            jax.ShapeDtypeStruct((bsz, 2 * DIFF_HEADS, sk, DIFF_QK_DIM), BF16),
            jax.ShapeDtypeStruct((bsz, 2 * DIFF_HEADS, sk, DIFF_QK_DIM), BF16),
            jax.ShapeDtypeStruct((bsz, DIFF_HEADS, nt, VT_ROWS, tm), BF16),
        ),
        grid=(bsz, nt),
        in_specs=[
            pl.BlockSpec((tm, d), lambda b, i: (b * nt + i, 0)),
            pl.BlockSpec((1, N_MOD, d), mod_map),
            pl.BlockSpec((d, IN_WIDTH), lambda b, i: (0, 0)),
            pl.BlockSpec((SUBLANES, LANES), lambda b, i: (0, 0)),
            pl.BlockSpec((LANES, LANES), lambda b, i: (0, 0)),
            pl.BlockSpec((3, tm, LANES), lambda b, i: (0, i, 0)),
            pl.BlockSpec((3, tm, LANES), lambda b, i: (0, i, 0)),
        ],
        out_specs=(
            pl.BlockSpec((1, tm, COL_GQ), lambda b, i: (b, i, 0)),
            pl.BlockSpec((1, GQA_HEADS, tm, GQA_HEAD_DIM), lambda b, i: (b, 0, i, 0)),
            pl.BlockSpec((1, GQA_KV_HEADS, tm, GQA_HEAD_DIM), lambda b, i: (b, 0, i, 0)),
            pl.BlockSpec((1, GQA_KV_HEADS, 1, VT_ROWS, tm), lambda b, i: (b, 0, i, 0, 0)),
            pl.BlockSpec((1, 2 * DIFF_HEADS, tm, DIFF_QK_DIM), lambda b, i: (b, 0, i, 0)),
            pl.BlockSpec((1, 2 * DIFF_HEADS, tm, DIFF_QK_DIM), lambda b, i: (b, 0, i, 0)),
            pl.BlockSpec((1, DIFF_HEADS, 1, VT_ROWS, tm), lambda b, i: (b, 0, i, 0, 0)),
        ),
        compiler_params=_params(("arbitrary", "arbitrary")),
        name="in_proj",
    )(xc, mod, w_in, gains, seg, rope_g, rope_d)


def _expm1(x):
    u = jnp.exp(x)
    return jnp.where(u == 1.0, x, (u - 1.0) * x / jnp.log(u))


def _lru_kernel(ax_ref, prev_ref, next_ref, cw_ref, cb_ref, w_ref, b_ref, lam_ref, h_ref, carry_ref, *, nt):
    tm = ax_ref.shape[1]
    d = pl.program_id(1)
    j = pl.program_id(2)
    blk = jnp.where(d == 0, j, jnp.where(j == 0, 0, nt - j))

    @pl.when(j == 0)
    def _():
        carry_ref[...] = jnp.zeros_like(carry_ref)

    x = ax_ref[0]
    no_left = jnp.logical_or(blk == 0, blk == 1)
    no_right = jnp.logical_or(blk == 0, blk == nt - 1)
    left = jnp.where(no_left, 0.0, prev_ref[0, SUBLANES - 1:SUBLANES, :])
    right0 = jnp.where(no_right, 0.0, next_ref[0, 0:1, :])
    right1 = jnp.where(no_right, 0.0, next_ref[0, 1:2, :])
    row = lax.broadcasted_iota(I32, (tm, 1), 0)
    x_m1 = jnp.where(row == 0, left, pltpu.roll(x, 1, 0))
    x_p1 = jnp.where(row == tm - 1, right0, pltpu.roll(x, tm - 1, 0))
    x_p2 = jnp.where(row == tm - 2, right0, jnp.where(row == tm - 1, right1, pltpu.roll(x, tm - 2, 0)))
    u = cb_ref[...] + x_m1 * cw_ref[0:1, :] + x * cw_ref[1:2, :] + x_p1 * cw_ref[2:3, :] + x_p2 * cw_ref[3:4, :]

    g = jnp.dot(u.astype(BF16), w_ref[0], preferred_element_type=F32) + b_ref[0]
    r = jax.nn.sigmoid(g[:, :LRU_WIDTH])
    gate_in = jax.nn.sigmoid(g[:, LRU_WIDTH:])
    neg_lam = -lam_ref[0]
    softplus = jnp.maximum(neg_lam, 0.0) + jnp.log1p(jnp.exp(-jnp.abs(neg_lam)))
    log_a = (-RG_C * r) * softplus
    a = jnp.exp(log_a)
    b = jnp.sqrt(-_expm1(2.0 * log_a)) * (gate_in * u)

    def scan(reverse):
        aa, bb = a, b
        s = 1
        while s < tm:
            shift = tm - s if reverse else s
            keep = (row < tm - s) if reverse else (row >= s)
            a_sh = jnp.where(keep, pltpu.roll(aa, shift, 0), 1.0)
            b_sh = jnp.where(keep, pltpu.roll(bb, shift, 0), 0.0)
            bb = aa * b_sh + bb
            aa = aa * a_sh
            s *= 2
        h = aa * carry_ref[...] + bb
        h_ref[0, 0] = h
        carry_ref[...] = h[0:1, :] if reverse else h[tm - 1:tm, :]

    @pl.when(d == 0)
    def _():
        scan(False)

    @pl.when(d == 1)
    def _():
        scan(True)


def _lru(zl, conv_w, conv_b, w_gates, b_gates, lam):
    bsz, sk, _ = zl.shape
    tm = TOKEN_TILE
    nt = sk // tm
    per = tm // SUBLANES

    def blk_of(d, j):
        return jnp.where(d == 0, j, jnp.where(j == 0, 0, nt - j))

    return pl.pallas_call(
        functools.partial(_lru_kernel, nt=nt),
        out_shape=jax.ShapeDtypeStruct((2, bsz, sk, LRU_WIDTH), F32),
        grid=(bsz, 2, nt),
        in_specs=[
            pl.BlockSpec((1, tm, LRU_WIDTH), lambda b, d, j: (b, blk_of(d, j), 0)),
            pl.BlockSpec((1, SUBLANES, LRU_WIDTH), lambda b, d, j: (b, jnp.maximum(blk_of(d, j) * per - 1, 0), 0)),
            pl.BlockSpec((1, SUBLANES, LRU_WIDTH),
                         lambda b, d, j: (b, jnp.minimum((blk_of(d, j) + 1) * per, nt * per - 1), 0)),
            pl.BlockSpec((CONV_W, LRU_WIDTH), lambda b, d, j: (0, 0)),
            pl.BlockSpec((1, LRU_WIDTH), lambda b, d, j: (0, 0)),
            pl.BlockSpec((1, LRU_WIDTH, 2 * LRU_WIDTH), lambda b, d, j: (d, 0, 0)),
            pl.BlockSpec((1, 1, 2 * LRU_WIDTH), lambda b, d, j: (d, 0, 0)),
            pl.BlockSpec((1, 1, LRU_WIDTH), lambda b, d, j: (d, 0, 0)),
        ],
        out_specs=pl.BlockSpec((1, 1, tm, LRU_WIDTH), lambda b, d, j: (d, b, blk_of(d, j), 0)),
        scratch_shapes=[pltpu.VMEM((1, LRU_WIDTH), F32)],
        compiler_params=_params(("arbitrary", "arbitrary", "arbitrary")),
        name="rg_lru",
    )(zl, zl, zl, conv_w, conv_b, w_gates, b_gates, lam)


def _attn_kernel(q_ref, k_ref, vt_ref, aux_ref, o_ref, s_ref, m_ref, acc_ref, *, k_heads, v_heads, nt, n_ctx_tiles, diff):
    tq = q_ref.shape[2]
    tk = tq
    dv = GQA_HEAD_DIM
    qi = pl.program_id(2)
    n_k = jnp.where(qi < n_ctx_tiles, n_ctx_tiles, nt)

    def scores(kt):
        start = pl.multiple_of(kt * tk, tk)
        return jnp.concatenate(
            [lax.dot_general(k_ref[0, k_heads[j], pl.ds(start, tk), :], q_ref[0, j], NT_DIMS,
                             preferred_element_type=F32) for j in range(4)], axis=1)

    def weighted_values(kt, p):
        return jnp.concatenate(
            [jnp.dot(vt_ref[0, v_heads[j], kt], p[:, j * tq:(j + 1) * tq], preferred_element_type=F32)
             for j in range(4)], axis=1)

    def col_max(s, m):
        return jnp.maximum(m, jnp.max(s, axis=0, keepdims=True))

    s_first = scores(0)
    s_ref[0] = s_first
    m_ref[...] = jnp.max(s_first, axis=0, keepdims=True)
    acc_ref[0] = jnp.zeros(acc_ref.shape[1:], F32)

    def trip(carry):
        i, cur = carry
        kt = KEY_TILES_PER_TRIP * i
        m0 = m_ref[...]
        s_cur = s_ref[cur]
        peak = jnp.zeros((1, 4 * tq), BF16)
        acc = acc_ref[cur]
        for u in range(KEY_TILES_PER_TRIP):
            s_next = scores(kt + u + 1)
            p = jnp.exp2(s_cur - m0).astype(BF16)
            peak = jnp.maximum(peak, jnp.max(p, axis=0, keepdims=True))
            acc = acc + weighted_values(kt + u, p)
            s_cur = s_next
        s_ref[1 - cur] = s_cur
        acc_ref[1 - cur] = acc
        peak = peak.astype(F32)
        lagging = jnp.max(peak) > 2.0 ** MAX_SHIFT_LAG

        @pl.when(lagging)
        def _():
            m_new = m0 + jnp.clip(jnp.log(peak) * LOG2_E, 0.0, MAX_SHIFT_RAISE)
            acc_ref[cur] = acc_ref[cur] * jnp.exp2(m0 - m_new)
            m_ref[...] = m_new

        return jnp.where(lagging, i, i + 1), jnp.where(lagging, cur, 1 - cur)

    n_trips = (n_k - 1) // KEY_TILES_PER_TRIP
    _, cur = lax.while_loop(lambda c: c[0] < n_trips, trip, (jnp.int32(0), jnp.int32(0)))

    s0 = s_ref[cur]
    m0 = m_ref[...]
    m1 = col_max(s0, m0)
    p0 = jnp.exp2(s0 - m1).astype(BF16)
    acc_all = acc_ref[cur] * jnp.exp2(m0 - m1) + weighted_values(n_k - 1, p0)

    outs = []
    for j in range(4):
        acc = acc_all[:, j * tq:(j + 1) * tq]
        outs.append(acc[0:dv] / acc[dv:dv + 1])
    if diff:
        dl = aux_ref[0:4, :]
        lam_init = aux_ref[4:5, 0:1]
        lam = (jnp.exp(jnp.sum(dl[0:1] * dl[1:2], axis=1, keepdims=True))
               - jnp.exp(jnp.sum(dl[2:3] * dl[3:4], axis=1, keepdims=True)) + lam_init)
        heads = []
        for hd in range(2):
            o = outs[2 * hd] - lam * outs[2 * hd + 1]
            heads.append(o * lax.rsqrt(jnp.mean(o * o, axis=0, keepdims=True) + EPS))
        o_ref[0] = ((jnp.concatenate(heads, axis=0).T * aux_ref[5:6, :]) * (1.0 - lam_init)).astype(o_ref.dtype)
    else:
        for pair in range(2):
            o_ref[0, :, pair * 2 * dv:(pair + 1) * 2 * dv] = (
                jnp.concatenate(outs[2 * pair:2 * pair + 2], axis=0).T.astype(o_ref.dtype))


def _attention(q, k, vt, aux, *, diff, n_ctx_tiles):
    bsz, n_q, sk, dh = q.shape
    n_k_heads = k.shape[1]
    n_v_heads = vt.shape[1]
    tq = TOKEN_TILE
    nt = sk // tq
    assert (nt - 1) % KEY_TILES_PER_TRIP == 0 and (n_ctx_tiles - 1) % KEY_TILES_PER_TRIP == 0
    groups = n_q // 4
    kpg = n_k_heads // groups
    vpg = n_v_heads // groups
    k_heads = tuple(j * kpg // 4 for j in range(4))
    v_heads = tuple(j * vpg // 4 for j in range(4))
    width = 2 * DIFF_V_DIM if diff else 4 * GQA_HEAD_DIM
    return pl.pallas_call(
        functools.partial(_attn_kernel, k_heads=k_heads, v_heads=v_heads, nt=nt, n_ctx_tiles=n_ctx_tiles, diff=diff),
        out_shape=jax.ShapeDtypeStruct((bsz, sk, groups * width), BF16),
        grid=(bsz, groups, nt),
        in_specs=[
            pl.BlockSpec((1, 4, tq, dh), lambda b, g, i: (b, g, i, 0)),
            pl.BlockSpec((1, kpg, sk, dh), lambda b, g, i: (b, g, 0, 0)),
            pl.BlockSpec((1, vpg, nt, VT_ROWS, tq), lambda b, g, i: (b, g, 0, 0, 0)),
            pl.BlockSpec((SUBLANES, LANES), lambda b, g, i: (0, 0)),
        ],
        out_specs=pl.BlockSpec((1, tq, width), lambda b, g, i: (b, i, g)),
        scratch_shapes=[pltpu.VMEM((2, tq, 4 * tq), F32), pltpu.VMEM((1, 4 * tq), F32),
                        pltpu.VMEM((2, VT_ROWS, 4 * tq), F32)],
        compiler_params=_params(("arbitrary", "arbitrary", "arbitrary")),
        name="diff_attention" if diff else "gqa_attention",
    )(q, k, vt, aux)


def _layer_norm(r, g, b):
    mu = jnp.mean(r, axis=-1, keepdims=True)
    c = r - mu
    var = jnp.mean(c * c, axis=-1, keepdims=True)
    return c * lax.rsqrt(var + EPS) * g + b


def _route(logits, bias):
    mx = jnp.max(logits, axis=0, keepdims=True)
    e = jnp.exp(logits - mx)
    probs = e / jnp.sum(e, axis=0, keepdims=True)
    sel = probs + bias
    epg = EXPERTS_PER_GROUP
    in_top2, scores = [], []
    for g in range(N_GROUPS):
        v = [sel[g * epg + i:g * epg + i + 1] for i in range(epg)]
        masks = []
        for i in range(epg):
            rank = jnp.zeros_like(v[i])
            for j in range(epg):
                if j != i:
                    beats = (v[j] > v[i]) if j > i else (v[j] >= v[i])
                    rank = rank + beats.astype(F32)
            masks.append(rank < 2.0)
        in_top2.append(masks)
        scores.append(sum(jnp.where(masks[i], v[i], 0.0) for i in range(epg)))
    chosen = []
    for g in range(N_GROUPS):
        c = None
        for j in range(N_GROUPS):
            if j != g:
                t = (scores[j] < scores[g]) if j < g else (scores[j] <= scores[g])
                c = t if c is None else jnp.logical_and(c, t)
        chosen.append(c)
    picked, weight = [], []
    for i in range(epg):
        m = None
        w = jnp.zeros_like(scores[0])
        for g in range(N_GROUPS):
            t = jnp.logical_and(chosen[g], in_top2[g][i])
            m = t if m is None else jnp.logical_or(m, t)
            w = w + jnp.where(t, probs[g * epg + i:g * epg + i + 1], 0.0)
        picked.append(m)
        weight.append(w)
    total = weight[0] + weight[1] + weight[2] + weight[3]
    gate = [w / total for w in weight]
    lo = jnp.where(picked[0], 0.0, jnp.where(picked[1], 1.0, 2.0))
    hi = jnp.where(picked[3], 3.0, jnp.where(picked[2], 2.0, 1.0))
    gate_lo = jnp.where(picked[0], gate[0], jnp.where(picked[1], gate[1], gate[2]))
    gate_hi = jnp.where(picked[3], gate[3], jnp.where(picked[2], gate[2], gate[1]))
    group = sum(jnp.where(chosen[g], float(g), 0.0) for g in range(N_GROUPS))
    base = jnp.where(lo == 0.0, 0.0, jnp.where(lo == 1.0, 3.0, 5.0))
    cls = group * float(len(PAIRS_LO)) + base + hi - lo - 1.0
    return cls.astype(I32), gate_lo, gate_hi


def _out_proj_kernel(hf_ref, hb_ref, ag_ref, yb_ref, yc_ref, x_ref, mod_ref, w_ref, lng_ref, lnb_ref,
                     rw_ref, rb_ref, x1_ref, ri_ref, cnt_ref, run_ref, *, alpha, n_ctx_tiles, n_batch):
    tm = x_ref.shape[0]
    ya = (jax.nn.gelu(ag_ref[0]) * (hf_ref[0, 0] + hb_ref[0, 0])).astype(BF16)
    n_a = LRU_WIDTH
    n_b = n_a + GQA_HEADS * GQA_HEAD_DIM
    y = (jnp.dot(ya, w_ref[0:n_a, :], preferred_element_type=F32)
         + jnp.dot(yb_ref[0], w_ref[n_a:n_b, :], preferred_element_type=F32)
         + jnp.dot(yc_ref[0], w_ref[n_b:, :], preferred_element_type=F32))
    x1 = _layer_norm(alpha * x_ref[...] + mod_ref[0, 2:3, :] * y, lng_ref[...], lnb_ref[...])
    d = x1.shape[1]
    x1_ref[:, 0:d] = x1

    h2 = x1 * (1.0 + mod_ref[0, 4:5, :]) + mod_ref[0, 3:4, :]
    h_hi = h2.astype(BF16)
    h_lo = (h2 - h_hi.astype(F32)).astype(BF16)
    both = lax.dot_general(rw_ref[...], h_hi, NT_DIMS, preferred_element_type=F32)
    logits = (both[0:N_EXPERTS] + both[N_EXPERTS:]
              + lax.dot_general(rw_ref[0:N_EXPERTS, :], h_lo, NT_DIMS, preferred_element_type=F32))
    cls, gate_lo, gate_hi = _route(logits, rb_ref[...])

    b = pl.program_id(0)
    i = pl.program_id(1)
    mod_row = jnp.where(i < n_ctx_tiles, n_batch, b).astype(F32)
    extra = jnp.concatenate([gate_lo, gate_hi, jnp.full((1, tm), mod_row, F32), jnp.zeros((LANES - 3, tm), F32)], axis=0)
    x1_ref[:, d:] = extra.T

    @pl.when(jnp.logical_and(b == 0, i == 0))
    def _():
        run_ref[...] = jnp.zeros_like(run_ref)

    member = lax.broadcasted_iota(I32, (CLASS_ROWS, tm), 0) == cls
    before = lax.broadcasted_iota(I32, (tm, tm), 0) < lax.broadcasted_iota(I32, (tm, tm), 1)
    earlier = jnp.dot(member.astype(BF16), before.astype(BF16), preferred_element_type=F32)
    run = run_ref[...]
    rank = jnp.sum(jnp.where(member, earlier + run, 0.0), axis=0, keepdims=True).astype(I32)
    run = run + jnp.sum(member.astype(F32), axis=1, keepdims=True)
    run_ref[...] = run
    cnt_ref[...] = jnp.broadcast_to(run, cnt_ref.shape).astype(I32)
    ri_ref[0] = jnp.concatenate([cls, rank, jnp.zeros((SUBLANES - 2, tm), I32)], axis=0)


def _out_proj(h_lru, zl, yb, yc, xc, mod, w_out, ln_g, ln_b, rw_split, rb, alpha, n_ctx_tiles):
    bsz, sk, _ = zl.shape
    d = xc.shape[1]
    tm = TOKEN_TILE
    nt = sk // tm

    def mod_map(b, i):
        return (jnp.where(i < n_ctx_tiles, bsz, b), 0, 0)

    return pl.pallas_call(
        functools.partial(_out_proj_kernel, alpha=alpha, n_ctx_tiles=n_ctx_tiles, n_batch=bsz),
        out_shape=(
            jax.ShapeDtypeStruct((bsz * sk, d + LANES), F32),
            jax.ShapeDtypeStruct((bsz, SUBLANES, sk), I32),
            jax.ShapeDtypeStruct((CLASS_ROWS, LANES), I32),
        ),
        grid=(bsz, nt),
        in_specs=[
            pl.BlockSpec((1, 1, tm, LRU_WIDTH), lambda b, i: (0, b, i, 0)),
            pl.BlockSpec((1, 1, tm, LRU_WIDTH), lambda b, i: (1, b, i, 0)),
            pl.BlockSpec((1, tm, LRU_WIDTH), lambda b, i: (b, i, 1)),
            pl.BlockSpec((1, tm, GQA_HEADS * GQA_HEAD_DIM), lambda b, i: (b, i, 0)),
            pl.BlockSpec((1, tm, DIFF_HEADS * DIFF_V_DIM), lambda b, i: (b, i, 0)),
            pl.BlockSpec((tm, d), lambda b, i: (b * nt + i, 0)),
            pl.BlockSpec((1, N_MOD, d), mod_map),
            pl.BlockSpec((d, d), lambda b, i: (0, 0)),
            pl.BlockSpec((1, d), lambda b, i: (0, 0)),
            pl.BlockSpec((1, d), lambda b, i: (0, 0)),
            pl.BlockSpec((2 * N_EXPERTS, d), lambda b, i: (0, 0)),
            pl.BlockSpec((N_EXPERTS, 1), lambda b, i: (0, 0)),
        ],
        out_specs=(
            pl.BlockSpec((tm, d + LANES), lambda b, i: (b * nt + i, 0)),
            pl.BlockSpec((1, SUBLANES, tm), lambda b, i: (b, 0, i)),
            pl.BlockSpec((CLASS_ROWS, LANES), lambda b, i: (0, 0)),
        ),
        scratch_shapes=[pltpu.VMEM((CLASS_ROWS, 1), F32)],
        compiler_params=_params(("arbitrary", "arbitrary")),
        name="out_proj_router",
    )(h_lru, h_lru, zl, yb, yc, xc, mod, w_out, ln_g, ln_b, rw_split, rb)


def _moe_kernel(cls_ref, rank_ref, pstart_ref, ea_ref, eb_ref, nv_ref, nu_ref,
                x_hbm, mod_ref, lng_ref, lnb_ref,
                wga_ref, wua_ref, wda_ref, wgb_ref, wub_ref, wdb_ref,
                o_hbm, tok_ref, xbuf, obuf, gsem, ssem, *, alpha, n_batch, n_tok):
    rows = xbuf.shape[1]
    d = obuf.shape[2]
    i = pl.program_id(0)
    n_steps = pl.num_programs(0)
    n_used = nu_ref[0]
    slot = i % 2

    @pl.when(i == 0)
    def _():
        def clear(r, c):
            tok_ref[r] = 0
            return c
        lax.fori_loop(0, tok_ref.shape[0], clear, 0, unroll=8)

        def place(t, c):
            tok_ref[pstart_ref[cls_ref[t]] + rank_ref[t]] = t
            return c
        lax.fori_loop(0, n_tok, place, 0, unroll=8)

        obuf[...] = jnp.zeros_like(obuf)
        for s in range(2):
            spare = pltpu.make_async_copy(obuf.at[s], o_hbm.at[pl.ds(n_tok + s * rows, rows)], ssem.at[s])
            spare.start()
            spare.wait()

    def start_gather(step, slot_):
        for r in range(rows):
            pltpu.make_async_copy(x_hbm.at[pl.ds(tok_ref[step * rows + r], 1)], xbuf.at[slot_, pl.ds(r, 1)],
                                  gsem.at[slot_]).start()

    def wait_gather(slot_):
        pltpu.make_async_copy(x_hbm.at[pl.ds(0, rows)], xbuf.at[slot_], gsem.at[slot_]).wait()

    def start_scatter(step, slot_):
        n_valid = nv_ref[step]
        for r in range(rows):
            dst = jnp.where(r < n_valid, tok_ref[step * rows + r], n_tok + slot_ * rows + r)
            pltpu.make_async_copy(obuf.at[slot_, pl.ds(r, 1)], o_hbm.at[pl.ds(dst, 1)], ssem.at[slot_]).start()

    def wait_scatter(slot_):
        pltpu.make_async_copy(obuf.at[slot_], o_hbm.at[pl.ds(0, rows)], ssem.at[slot_]).wait()

    @pl.when(i == 0)
    def _():
        start_gather(0, 0)

    @pl.when(i < n_used)
    def _():
        wait_gather(slot)

        @pl.when(i + 1 < n_used)
        def _():
            start_gather(i + 1, 1 - slot)

        @pl.when(i >= 2)
        def _():
            wait_scatter(slot)

        x = xbuf[slot, :, 0:d]
        gate_a = xbuf[slot, :, d:d + 1]
        gate_b = xbuf[slot, :, d + 1:d + 2]
        mid = xbuf[slot, :, d + 2:d + 3]

        def pick(kk):
            v = mod_ref[n_batch, kk:kk + 1, :]
            for b in range(n_batch):
                v = jnp.where(mid == float(b), mod_ref[b, kk:kk + 1, :], v)
            return v

        h = x * (1.0 + pick(4)) + pick(3)

        def ffn(wg, wu, wd):
            a = jnp.dot(h, wg[0], preferred_element_type=F32)
            u = jnp.dot(h, wu[0], preferred_element_type=F32)
            return jnp.dot((a * jax.nn.sigmoid(a)) * u, wd[0], preferred_element_type=F32)

        f = ffn(wga_ref, wua_ref, wda_ref) * gate_a + ffn(wgb_ref, wub_ref, wdb_ref) * gate_b
        obuf[slot] = _layer_norm(alpha * x + pick(5) * f, lng_ref[...], lnb_ref[...])
        start_scatter(i, slot)

    @pl.when(i == n_steps - 1)
    def _():
        wait_scatter(0)

        @pl.when(n_used >= 2)
        def _():
            wait_scatter(1)


def _moe(x_wide, n_tok, cls, rank, pstart, blk_ea, blk_eb, blk_nv, n_used, mod, ln_g, ln_b,
         w_gate, w_up, w_down, layer, alpha, n_batch):
    d = x_wide.shape[1] - LANES
    rows = MOE_ROWS
    n_blk = blk_ea.shape[0]
    ff = w_gate.shape[-1]
    first = layer * N_EXPERTS

    def wa(i, cl, rk, ps, ea, eb, nv, nu):
        return (first + ea[i], 0, 0)

    def wb(i, cl, rk, ps, ea, eb, nv, nu):
        return (first + eb[i], 0, 0)

    const2 = lambda i, cl, rk, ps, ea, eb, nv, nu: (0, 0)
    grid_spec = pltpu.PrefetchScalarGridSpec(
        num_scalar_prefetch=7,
        grid=(n_blk,),
        in_specs=[
            pl.BlockSpec(memory_space=pl.ANY),
            pl.BlockSpec(mod.shape, lambda i, cl, rk, ps, ea, eb, nv, nu: (0, 0, 0)),
            pl.BlockSpec((1, d), const2),
            pl.BlockSpec((1, d), const2),
            pl.BlockSpec((1, d, ff), wa),
            pl.BlockSpec((1, d, ff), wa),
            pl.BlockSpec((1, ff, d), wa),
            pl.BlockSpec((1, d, ff), wb),
            pl.BlockSpec((1, d, ff), wb),
            pl.BlockSpec((1, ff, d), wb),
        ],
        out_specs=pl.BlockSpec(memory_space=pl.ANY),
        scratch_shapes=[
            pltpu.SMEM((n_blk * rows,), I32),
            pltpu.VMEM((2, rows, d + LANES), F32),
            pltpu.VMEM((2, rows, d), F32),
            pltpu.SemaphoreType.DMA((2,)),
            pltpu.SemaphoreType.DMA((2,)),
        ],
    )
    return pl.pallas_call(
        functools.partial(_moe_kernel, alpha=alpha, n_batch=n_batch, n_tok=n_tok),
        out_shape=jax.ShapeDtypeStruct((n_tok + 2 * rows, d), F32),
        grid_spec=grid_spec,
        compiler_params=_params(("arbitrary",)),
        name="moe_experts",
    )(cls, rank, pstart, blk_ea, blk_eb, blk_nv, n_used, x_wide, mod, ln_g, ln_b,
      w_gate, w_up, w_down, w_gate, w_up, w_down)


def _block_tables(counts, n_tok):
    rows = MOE_ROWS
    n_blk = n_tok // rows + N_CLASSES
    n_pairs = len(PAIRS_LO)
    padded = (counts + rows - 1) // rows * rows
    pend = jnp.cumsum(padded)
    pstart = pend - padded
    n_used = pend[-1] // rows
    blk = jnp.arange(n_blk, dtype=I32)
    blk_ids = jnp.minimum(blk, jnp.maximum(n_used - 1, 0))
    blk_cls = jnp.minimum(jnp.sum((blk_ids[:, None] * rows >= pend[None, :]).astype(I32), axis=1), N_CLASSES - 1)
    onehot = blk_cls[:, None] == jnp.arange(N_CLASSES, dtype=I32)[None, :]
    within = blk_ids * rows - jnp.sum(jnp.where(onehot, pstart[None, :], 0), axis=1)
    count_b = jnp.sum(jnp.where(onehot, counts[None, :], 0), axis=1)
    blk_nv = jnp.where(blk < n_used, jnp.clip(count_b - within, 0, rows), 0).astype(I32)
    pair = blk_cls % n_pairs
    lo = sum(jnp.where(pair == p, PAIRS_LO[p], 0) for p in range(n_pairs))
    hi = sum(jnp.where(pair == p, PAIRS_HI[p], 0) for p in range(n_pairs))
    first = blk_cls // n_pairs * EXPERTS_PER_GROUP
    pstart_pad = jnp.concatenate([pstart, jnp.zeros((CLASS_ROWS - N_CLASSES,), I32)]).astype(I32)
    return (pstart_pad, (first + lo).astype(I32), (first + hi).astype(I32), blk_nv,
            n_used.reshape(1).astype(I32))


def _rope_tables(n_lat, n_ctx, dim):
    rows = n_lat // GRID_W
    row = jnp.repeat(jnp.arange(rows, dtype=F32), GRID_W)
    col = jnp.tile(jnp.arange(GRID_W, dtype=F32), rows)
    n_freq = dim // 4
    inv = ROPE_THETA ** (-jnp.arange(n_freq, dtype=F32) / n_freq)
    ang = jnp.concatenate([row[:, None] * inv, col[:, None] * inv], axis=-1)
    cos = jnp.repeat(jnp.cos(ang), 2, axis=-1)
    sin = jnp.repeat(jnp.sin(ang), 2, axis=-1)
    even = (jnp.arange(dim) % 2 == 0)
    tabs = jnp.stack([cos, jnp.where(even, -sin, 0.0), jnp.where(even, 0.0, sin)])
    ident = jnp.stack([jnp.ones((n_ctx, dim), F32), jnp.zeros((n_ctx, dim), F32), jnp.zeros((n_ctx, dim), F32)])
    return jnp.tile(jnp.concatenate([ident, tabs], axis=1), (1, 1, LANES // dim))


def _block_diag(w):
    n, c, _ = w.shape
    eye = jnp.eye(n, dtype=w.dtype)
    return (eye[:, None, :, None] * w[:, :, None, :]).reshape(n * c, n * c)


def kernel(x, c, ctx, c_ctx, ada_w, ada_b, w_in, conv_w, conv_b, rg_wa, rg_ba, rg_wx, rg_bx, rg_lam, q_norm_g, k_norm_g, diff_lambda, diff_subln_g, w_out, ln1_g, ln1_b, ln2_g, ln2_b, router_w, router_b, exp_w_gate, exp_w_up, exp_w_down):
    bsz, s, d = x.shape
    n_ctx = ctx.shape[1]
    depth = w_in.shape[0]
    sk = n_ctx + s
    assert d == D_MODEL and n_ctx == TOKEN_TILE and s % TOKEN_TILE == 0 and bsz + 1 <= SUBLANES
    n_ctx_tiles = n_ctx // TOKEN_TILE
    n_tok = bsz * sk
    alpha = (2.0 * depth) ** 0.25

    c_rows = jnp.concatenate([c, c_ctx[None, :], jnp.zeros((SUBLANES - bsz - 1, d), F32)], axis=0)
    mods = _ada_modulation(c_rows, ada_w, ada_b).reshape(depth, SUBLANES, N_MOD, d)

    rope_g = _rope_tables(s, n_ctx, GQA_HEAD_DIM)
    rope_d = _rope_tables(s, n_ctx, DIFF_QK_DIM)
    lane = np.arange(LANES)
    seg = jnp.asarray(lane[:, None] // GQA_HEAD_DIM == lane[None, :] // GQA_HEAD_DIM, BF16)
    rw_t = router_w.T
    rw_hi = rw_t.astype(BF16)
    rw_split = jnp.concatenate([rw_hi, (rw_t - rw_hi.astype(F32)).astype(BF16)], axis=0)
    rb = router_b.reshape(N_EXPERTS, 1)
    zero_aux = jnp.zeros((SUBLANES, LANES), F32)

    xc = jnp.concatenate([jnp.concatenate([ctx, x], axis=1).reshape(n_tok, d),
                          jnp.zeros((2 * MOE_ROWS, d), F32)], axis=0)
    w_gate_all = exp_w_gate.reshape(depth * N_EXPERTS, d, EXPERT_FF)
    w_up_all = exp_w_up.reshape(depth * N_EXPERTS, d, EXPERT_FF)
    w_down_all = exp_w_down.reshape(depth * N_EXPERTS, EXPERT_FF, d)
    for li in range(depth):
        lam_init = 0.8 - 0.6 * math.exp(-0.3 * li)
        mod = mods[li]
        gains = jnp.concatenate([jnp.tile(jnp.tile(q_norm_g[li], 2)[None, :], (4, 1)),
                                 jnp.tile(k_norm_g[li], 2)[None, :], jnp.zeros((3, LANES), F32)], axis=0)
        zl, q, k, vt, dq, dk, dvt = _in_proj(xc, bsz, sk, mod, w_in[li].astype(BF16), gains, seg, rope_g, rope_d,
                                             n_ctx_tiles)

        w_gates = jnp.stack([jnp.concatenate([_block_diag(rg_wa[li, dd]), _block_diag(rg_wx[li, dd])], axis=1)
                             for dd in range(2)]).astype(BF16)
        b_gates = jnp.concatenate([rg_ba[li], rg_bx[li]], axis=-1)[:, None, :]
        h_lru = _lru(zl, conv_w[li], conv_b[li][None, :], w_gates, b_gates, rg_lam[li][:, None, :])

        yb = _attention(q, k, vt, zero_aux, diff=False, n_ctx_tiles=n_ctx_tiles)
        aux = jnp.concatenate([
            jnp.pad(diff_lambda[li], ((0, 0), (0, LANES - DIFF_QK_DIM))),
            jnp.full((1, LANES), lam_init, F32),
            jnp.tile(diff_subln_g[li], 2)[None, :],
            jnp.zeros((2, LANES), F32)], axis=0)
        yc = _attention(dq, dk, dvt, aux, diff=True, n_ctx_tiles=n_ctx_tiles)

        x1, route_i, counts = _out_proj(h_lru, zl, yb, yc, xc, mod, w_out[li].astype(BF16), ln1_g[li][None, :],
                                        ln1_b[li][None, :], rw_split, rb, alpha, n_ctx_tiles)

        pstart, blk_ea, blk_eb, blk_nv, n_used = _block_tables(counts[:N_CLASSES, 0], n_tok)
        xc = _moe(x1, n_tok, route_i[:, 0, :].reshape(n_tok), route_i[:, 1, :].reshape(n_tok), pstart,
                  blk_ea, blk_eb, blk_nv, n_used, mod, ln2_g[li][None, :], ln2_b[li][None, :],
                  w_gate_all, w_up_all, w_down_all, li, alpha, bsz)
    return xc[:n_tok].reshape(bsz, sk, d)[:, n_ctx:, :]
```

```python
import functools
import math

import numpy as np
import jax
import jax.numpy as jnp
from jax import lax
from jax.experimental import pallas as pl
from jax.experimental.pallas import tpu as pltpu

F32 = jnp.float32
BF16 = jnp.bfloat16
I32 = jnp.int32

D_MODEL = 1024
GRID_W = 64
LRU_WIDTH = 256
LRU_BLOCKS = 4
CONV_W = 4
RG_C = 8.0
GQA_HEADS = 8
GQA_KV_HEADS = 2
GQA_HEAD_DIM = 64
DIFF_HEADS = 4
DIFF_QK_DIM = 32
DIFF_V_DIM = 64
IN_WIDTH = 2048
N_EXPERTS = 16
N_GROUPS = 4
EXPERTS_PER_GROUP = 4
EXPERT_FF = 512
ROPE_THETA = 10000.0
EPS = 1e-6
N_MOD = 6
LOG2_E = math.log2(math.e)

COL_AX, COL_AG, COL_GQ, COL_GK, COL_GV, COL_DQ, COL_DK, COL_DV = 0, 256, 512, 1024, 1152, 1280, 1536, 1792

LANES = 128
SUBLANES = 8
VMEM_LIMIT = 56 * 1024 * 1024

TOKEN_TILE = 256
VT_ROWS = 80
PAIRS_LO = (0, 0, 0, 1, 1, 2)
PAIRS_HI = (1, 2, 3, 2, 3, 3)
N_CLASSES = N_GROUPS * len(PAIRS_LO)
CLASS_ROWS = 32
MOE_ROWS = 128
KEY_TILES_PER_TRIP = 8
MAX_SHIFT_LAG = 60.0
MAX_SHIFT_RAISE = 120.0

NT_DIMS = (((1,), (1,)), ((), ()))


def _params(semantics):
    return pltpu.CompilerParams(dimension_semantics=semantics, vmem_limit_bytes=VMEM_LIMIT)


def _ada_kernel(c_ref, w_ref, b_ref, o_ref):
    c = c_ref[...]
    o_ref[0] = jnp.dot(c * jax.nn.sigmoid(c), w_ref[0], preferred_element_type=F32) + b_ref[0]


def _ada_modulation(c_rows, ada_w, ada_b):
    depth, d, n = ada_w.shape
    tn = 1536
    return pl.pallas_call(
        _ada_kernel,
        out_shape=jax.ShapeDtypeStruct((depth, SUBLANES, n), F32),
        grid=(depth, n // tn),
        in_specs=[
            pl.BlockSpec((SUBLANES, d), lambda l, j: (0, 0)),
            pl.BlockSpec((1, d, tn), lambda l, j: (l, 0, j)),
            pl.BlockSpec((1, 1, tn), lambda l, j: (l, 0, j)),
        ],
        out_specs=pl.BlockSpec((1, SUBLANES, tn), lambda l, j: (l, 0, j)),
        compiler_params=_params(("arbitrary", "arbitrary")),
        name="ada_modulation",
    )(c_rows, ada_w, ada_b.reshape(depth, 1, n))


def _rotate_pairs(v, cos, sin_next, sin_prev):
    return v * cos + pltpu.roll(v, LANES - 1, 1) * sin_next + pltpu.roll(v, 1, 1) * sin_prev


def _in_proj_kernel(x_ref, mod_ref, w_ref, gain_ref, seg_ref, rope_g_ref, rope_d_ref,
                    zl_ref, q_ref, k_ref, vt_ref, dq_ref, dk_ref, dvt_ref):
    tm = x_ref.shape[0]
    x = x_ref[...]
    shift = mod_ref[0, 0:1, :]
    scale = mod_ref[0, 1:2, :]
    h = (x * (1.0 + scale) + shift).astype(BF16)
    z = jnp.dot(h, w_ref[...], preferred_element_type=F32)
    zl_ref[0] = z[:, COL_AX:COL_GQ]

    ones_rows = jnp.where(lax.broadcasted_iota(I32, (VT_ROWS - GQA_HEAD_DIM, tm), 0) == 0, 1.0, 0.0).astype(BF16)

    seg = seg_ref[...]
    cos, s_next, s_prev = rope_g_ref[0], rope_g_ref[1], rope_g_ref[2]
    for g in range(5):
        v = z[:, COL_GQ + g * LANES:COL_GQ + (g + 1) * LANES]
        sq = v * v
        sq_hi = sq.astype(BF16)
        sq_lo = (sq - sq_hi.astype(F32)).astype(BF16)
        ssq = jnp.dot(sq_hi, seg, preferred_element_type=F32) + jnp.dot(sq_lo, seg, preferred_element_type=F32)
        v = v * lax.rsqrt(ssq * (1.0 / GQA_HEAD_DIM) + EPS) * gain_ref[g:g + 1, :]
        v = _rotate_pairs(v, cos, s_next, s_prev)
        if g < 4:
            v = (v * (GQA_HEAD_DIM ** -0.5 * LOG2_E)).astype(BF16)
            q_ref[0, 2 * g] = v[:, :GQA_HEAD_DIM]
            q_ref[0, 2 * g + 1] = v[:, GQA_HEAD_DIM:]
        else:
            v = v.astype(BF16)
            k_ref[0, 0] = v[:, :GQA_HEAD_DIM]
            k_ref[0, 1] = v[:, GQA_HEAD_DIM:]
    vt = z[:, COL_GV:COL_DQ].T.astype(BF16)
    for hd in range(GQA_KV_HEADS):
        vt_ref[0, hd, 0, 0:GQA_HEAD_DIM, :] = vt[hd * GQA_HEAD_DIM:(hd + 1) * GQA_HEAD_DIM]
        vt_ref[0, hd, 0, GQA_HEAD_DIM:VT_ROWS, :] = ones_rows

    cos, s_next, s_prev = rope_d_ref[0], rope_d_ref[1], rope_d_ref[2]
    for g in range(4):
        v = _rotate_pairs(z[:, COL_DQ + g * LANES:COL_DQ + (g + 1) * LANES], cos, s_next, s_prev)
        if g < 2:
            v = v * (DIFF_QK_DIM ** -0.5 * LOG2_E)
        v = v.astype(BF16)
        dst = dq_ref if g < 2 else dk_ref
        for j in range(4):
            dst[0, (g % 2) * 4 + j] = v[:, j * DIFF_QK_DIM:(j + 1) * DIFF_QK_DIM]
    dvt = z[:, COL_DV:IN_WIDTH].T.astype(BF16)
    for hd in range(DIFF_HEADS):
        dvt_ref[0, hd, 0, 0:DIFF_V_DIM, :] = dvt[hd * DIFF_V_DIM:(hd + 1) * DIFF_V_DIM]
        dvt_ref[0, hd, 0, DIFF_V_DIM:VT_ROWS, :] = ones_rows


def _in_proj(xc, bsz, sk, mod, w_in, gains, seg, rope_g, rope_d, n_ctx_tiles):
    d = xc.shape[1]
    tm = TOKEN_TILE
    nt = sk // tm
    n_mod_rows = mod.shape[0]

    def mod_map(b, i):
        return (jnp.where(i < n_ctx_tiles, bsz, b), 0, 0)

    assert n_mod_rows > bsz
    return pl.pallas_call(
        _in_proj_kernel,
        out_shape=(
            jax.ShapeDtypeStruct((bsz, sk, COL_GQ), F32),
            jax.ShapeDtypeStruct((bsz, GQA_HEADS, sk, GQA_HEAD_DIM), BF16),
            jax.ShapeDtypeStruct((bsz, GQA_KV_HEADS, sk, GQA_HEAD_DIM), BF16),
            jax.ShapeDtypeStruct((bsz, GQA_KV_HEADS, nt, VT_ROWS, tm), BF16),
            jax.ShapeDtypeStruct((bsz, 2 * DIFF_HEADS, sk, DIFF_QK_DIM), BF16),
            jax.ShapeDtypeStruct((bsz, 2 * DIFF_HEADS, sk, DIFF_QK_DIM), BF16),
            jax.ShapeDtypeStruct((bsz, DIFF_HEADS, nt, VT_ROWS, tm), BF16),
        ),
        grid=(bsz, nt),
        in_specs=[
            pl.BlockSpec((tm, d), lambda b, i: (b * nt + i, 0)),
            pl.BlockSpec((1, N_MOD, d), mod_map),
            pl.BlockSpec((d, IN_WIDTH), lambda b, i: (0, 0)),
            pl.BlockSpec((SUBLANES, LANES), lambda b, i: (0, 0)),
            pl.BlockSpec((LANES, LANES), lambda b, i: (0, 0)),
            pl.BlockSpec((3, tm, LANES), lambda b, i: (0, i, 0)),
            pl.BlockSpec((3, tm, LANES), lambda b, i: (0, i, 0)),
        ],
        out_specs=(
            pl.BlockSpec((1, tm, COL_GQ), lambda b, i: (b, i, 0)),
            pl.BlockSpec((1, GQA_HEADS, tm, GQA_HEAD_DIM), lambda b, i: (b, 0, i, 0)),
            pl.BlockSpec((1, GQA_KV_HEADS, tm, GQA_HEAD_DIM), lambda b, i: (b, 0, i, 0)),
            pl.BlockSpec((1, GQA_KV_HEADS, 1, VT_ROWS, tm), lambda b, i: (b, 0, i, 0, 0)),
            pl.BlockSpec((1, 2 * DIFF_HEADS, tm, DIFF_QK_DIM), lambda b, i: (b, 0, i, 0)),
            pl.BlockSpec((1, 2 * DIFF_HEADS, tm, DIFF_QK_DIM), lambda b, i: (b, 0, i, 0)),
            pl.BlockSpec((1, DIFF_HEADS, 1, VT_ROWS, tm), lambda b, i: (b, 0, i, 0, 0)),
        ),
        compiler_params=_params(("arbitrary", "arbitrary")),
        name="in_proj",
    )(xc, mod, w_in, gains, seg, rope_g, rope_d)


def _expm1(x):
    u = jnp.exp(x)
    return jnp.where(u == 1.0, x, (u - 1.0) * x / jnp.log(u))


def _lru_kernel(ax_ref, prev_ref, next_ref, cw_ref, cb_ref, w_ref, b_ref, lam_ref, h_ref, carry_ref, *, nt):
    tm = ax_ref.shape[1]
    d = pl.program_id(1)
    j = pl.program_id(2)
    blk = jnp.where(d == 0, j, jnp.where(j == 0, 0, nt - j))

    @pl.when(j == 0)
    def _():
        carry_ref[...] = jnp.zeros_like(carry_ref)

    x = ax_ref[0]
    no_left = jnp.logical_or(blk == 0, blk == 1)
    no_right = jnp.logical_or(blk == 0, blk == nt - 1)
    left = jnp.where(no_left, 0.0, prev_ref[0, SUBLANES - 1:SUBLANES, :])
    right0 = jnp.where(no_right, 0.0, next_ref[0, 0:1, :])
    right1 = jnp.where(no_right, 0.0, next_ref[0, 1:2, :])
    row = lax.broadcasted_iota(I32, (tm, 1), 0)
    x_m1 = jnp.where(row == 0, left, pltpu.roll(x, 1, 0))
    x_p1 = jnp.where(row == tm - 1, right0, pltpu.roll(x, tm - 1, 0))
    x_p2 = jnp.where(row == tm - 2, right0, jnp.where(row == tm - 1, right1, pltpu.roll(x, tm - 2, 0)))
    u = cb_ref[...] + x_m1 * cw_ref[0:1, :] + x * cw_ref[1:2, :] + x_p1 * cw_ref[2:3, :] + x_p2 * cw_ref[3:4, :]

    g = jnp.dot(u.astype(BF16), w_ref[0], preferred_element_type=F32) + b_ref[0]
    r = jax.nn.sigmoid(g[:, :LRU_WIDTH])
    gate_in = jax.nn.sigmoid(g[:, LRU_WIDTH:])
    neg_lam = -lam_ref[0]
    softplus = jnp.maximum(neg_lam, 0.0) + jnp.log1p(jnp.exp(-jnp.abs(neg_lam)))
    log_a = (-RG_C * r) * softplus
    a = jnp.exp(log_a)
    b = jnp.sqrt(-_expm1(2.0 * log_a)) * (gate_in * u)

    def scan(reverse):
        aa, bb = a, b
        s = 1
        while s < tm:
            shift = tm - s if reverse else s
            keep = (row < tm - s) if reverse else (row >= s)
            a_sh = jnp.where(keep, pltpu.roll(aa, shift, 0), 1.0)
            b_sh = jnp.where(keep, pltpu.roll(bb, shift, 0), 0.0)
            bb = aa * b_sh + bb
            aa = aa * a_sh
            s *= 2
        h = aa * carry_ref[...] + bb
        h_ref[0, 0] = h
        carry_ref[...] = h[0:1, :] if reverse else h[tm - 1:tm, :]

    @pl.when(d == 0)
    def _():
        scan(False)

    @pl.when(d == 1)
    def _():
        scan(True)


def _lru(zl, conv_w, conv_b, w_gates, b_gates, lam):
    bsz, sk, _ = zl.shape
    tm = TOKEN_TILE
    nt = sk // tm
    per = tm // SUBLANES

    def blk_of(d, j):
        return jnp.where(d == 0, j, jnp.where(j == 0, 0, nt - j))

    return pl.pallas_call(
        functools.partial(_lru_kernel, nt=nt),
        out_shape=jax.ShapeDtypeStruct((2, bsz, sk, LRU_WIDTH), F32),
        grid=(bsz, 2, nt),
        in_specs=[
            pl.BlockSpec((1, tm, LRU_WIDTH), lambda b, d, j: (b, blk_of(d, j), 0)),
            pl.BlockSpec((1, SUBLANES, LRU_WIDTH), lambda b, d, j: (b, jnp.maximum(blk_of(d, j) * per - 1, 0), 0)),
            pl.BlockSpec((1, SUBLANES, LRU_WIDTH),
                         lambda b, d, j: (b, jnp.minimum((blk_of(d, j) + 1) * per, nt * per - 1), 0)),
            pl.BlockSpec((CONV_W, LRU_WIDTH), lambda b, d, j: (0, 0)),
            pl.BlockSpec((1, LRU_WIDTH), lambda b, d, j: (0, 0)),
            pl.BlockSpec((1, LRU_WIDTH, 2 * LRU_WIDTH), lambda b, d, j: (d, 0, 0)),
            pl.BlockSpec((1, 1, 2 * LRU_WIDTH), lambda b, d, j: (d, 0, 0)),
            pl.BlockSpec((1, 1, LRU_WIDTH), lambda b, d, j: (d, 0, 0)),
        ],
        out_specs=pl.BlockSpec((1, 1, tm, LRU_WIDTH), lambda b, d, j: (d, b, blk_of(d, j), 0)),
        scratch_shapes=[pltpu.VMEM((1, LRU_WIDTH), F32)],
        compiler_params=_params(("arbitrary", "arbitrary", "arbitrary")),
        name="rg_lru",
    )(zl, zl, zl, conv_w, conv_b, w_gates, b_gates, lam)


def _attn_kernel(q_ref, k_ref, vt_ref, aux_ref, o_ref, s_ref, m_ref, acc_ref, *, k_heads, v_heads, nt, n_ctx_tiles, diff):
    tq = q_ref.shape[2]
    tk = tq
    dv = GQA_HEAD_DIM
    qi = pl.program_id(2)
    n_k = jnp.where(qi < n_ctx_tiles, n_ctx_tiles, nt)

    def scores(kt):
        start = pl.multiple_of(kt * tk, tk)
        return jnp.concatenate(
            [lax.dot_general(k_ref[0, k_heads[j], pl.ds(start, tk), :], q_ref[0, j], NT_DIMS,
                             preferred_element_type=F32) for j in range(4)], axis=1)

    def weighted_values(kt, p):
        return jnp.concatenate(
            [jnp.dot(vt_ref[0, v_heads[j], kt], p[:, j * tq:(j + 1) * tq], preferred_element_type=F32)
             for j in range(4)], axis=1)

    def col_max(s, m):
        return jnp.maximum(m, jnp.max(s, axis=0, keepdims=True))

    s_first = scores(0)
    s_ref[0] = s_first
    m_ref[...] = jnp.max(s_first, axis=0, keepdims=True)
    acc_ref[0] = jnp.zeros(acc_ref.shape[1:], F32)

    def trip(carry):
        i, cur = carry
        kt = KEY_TILES_PER_TRIP * i
        m0 = m_ref[...]
        s_cur = s_ref[cur]
        peak = jnp.zeros((1, 4 * tq), BF16)
        acc = acc_ref[cur]
        for u in range(KEY_TILES_PER_TRIP):
            s_next = scores(kt + u + 1)
            p = jnp.exp2(s_cur - m0).astype(BF16)
            peak = jnp.maximum(peak, jnp.max(p, axis=0, keepdims=True))
            acc = acc + weighted_values(kt + u, p)
            s_cur = s_next
        s_ref[1 - cur] = s_cur
        acc_ref[1 - cur] = acc
        peak = peak.astype(F32)
        lagging = jnp.max(peak) > 2.0 ** MAX_SHIFT_LAG

        @pl.when(lagging)
        def _():
            m_new = m0 + jnp.clip(jnp.log(peak) * LOG2_E, 0.0, MAX_SHIFT_RAISE)
            acc_ref[cur] = acc_ref[cur] * jnp.exp2(m0 - m_new)
            m_ref[...] = m_new

        return jnp.where(lagging, i, i + 1), jnp.where(lagging, cur, 1 - cur)

    n_trips = (n_k - 1) // KEY_TILES_PER_TRIP
    _, cur = lax.while_loop(lambda c: c[0] < n_trips, trip, (jnp.int32(0), jnp.int32(0)))

    s0 = s_ref[cur]
    m0 = m_ref[...]
    m1 = col_max(s0, m0)
    p0 = jnp.exp2(s0 - m1).astype(BF16)
    acc_all = acc_ref[cur] * jnp.exp2(m0 - m1) + weighted_values(n_k - 1, p0)

    outs = []
    for j in range(4):
        acc = acc_all[:, j * tq:(j + 1) * tq]
        outs.append(acc[0:dv] / acc[dv:dv + 1])
    if diff:
        dl = aux_ref[0:4, :]
        lam_init = aux_ref[4:5, 0:1]
        lam = (jnp.exp(jnp.sum(dl[0:1] * dl[1:2], axis=1, keepdims=True))
               - jnp.exp(jnp.sum(dl[2:3] * dl[3:4], axis=1, keepdims=True)) + lam_init)
        heads = []
        for hd in range(2):
            o = outs[2 * hd] - lam * outs[2 * hd + 1]
            heads.append(o * lax.rsqrt(jnp.mean(o * o, axis=0, keepdims=True) + EPS))
        o_ref[0] = ((jnp.concatenate(heads, axis=0).T * aux_ref[5:6, :]) * (1.0 - lam_init)).astype(o_ref.dtype)
    else:
        for pair in range(2):
            o_ref[0, :, pair * 2 * dv:(pair + 1) * 2 * dv] = (
                jnp.concatenate(outs[2 * pair:2 * pair + 2], axis=0).T.astype(o_ref.dtype))


def _attention(q, k, vt, aux, *, diff, n_ctx_tiles):
    bsz, n_q, sk, dh = q.shape
    n_k_heads = k.shape[1]
    n_v_heads = vt.shape[1]
    tq = TOKEN_TILE
    nt = sk // tq
    assert (nt - 1) % KEY_TILES_PER_TRIP == 0 and (n_ctx_tiles - 1) % KEY_TILES_PER_TRIP == 0
    groups = n_q // 4
    kpg = n_k_heads // groups
    vpg = n_v_heads // groups
    k_heads = tuple(j * kpg // 4 for j in range(4))
    v_heads = tuple(j * vpg // 4 for j in range(4))
    width = 2 * DIFF_V_DIM if diff else 4 * GQA_HEAD_DIM
    return pl.pallas_call(
        functools.partial(_attn_kernel, k_heads=k_heads, v_heads=v_heads, nt=nt, n_ctx_tiles=n_ctx_tiles, diff=diff),
        out_shape=jax.ShapeDtypeStruct((bsz, sk, groups * width), BF16),
        grid=(bsz, groups, nt),
        in_specs=[
            pl.BlockSpec((1, 4, tq, dh), lambda b, g, i: (b, g, i, 0)),
            pl.BlockSpec((1, kpg, sk, dh), lambda b, g, i: (b, g, 0, 0)),
            pl.BlockSpec((1, vpg, nt, VT_ROWS, tq), lambda b, g, i: (b, g, 0, 0, 0)),
            pl.BlockSpec((SUBLANES, LANES), lambda b, g, i: (0, 0)),
        ],
        out_specs=pl.BlockSpec((1, tq, width), lambda b, g, i: (b, i, g)),
        scratch_shapes=[pltpu.VMEM((2, tq, 4 * tq), F32), pltpu.VMEM((1, 4 * tq), F32),
                        pltpu.VMEM((2, VT_ROWS, 4 * tq), F32)],
        compiler_params=_params(("arbitrary", "arbitrary", "arbitrary")),
        name="diff_attention" if diff else "gqa_attention",
    )(q, k, vt, aux)


def _layer_norm(r, g, b):
    mu = jnp.mean(r, axis=-1, keepdims=True)
    c = r - mu
    var = jnp.mean(c * c, axis=-1, keepdims=True)
    return c * lax.rsqrt(var + EPS) * g + b


def _route(logits, bias):
    mx = jnp.max(logits, axis=0, keepdims=True)
    e = jnp.exp(logits - mx)
    probs = e / jnp.sum(e, axis=0, keepdims=True)
    sel = probs + bias
    epg = EXPERTS_PER_GROUP
    in_top2, scores = [], []
    for g in range(N_GROUPS):
        v = [sel[g * epg + i:g * epg + i + 1] for i in range(epg)]
        masks = []
        for i in range(epg):
            rank = jnp.zeros_like(v[i])
            for j in range(epg):
                if j != i:
                    beats = (v[j] > v[i]) if j > i else (v[j] >= v[i])
                    rank = rank + beats.astype(F32)
            masks.append(rank < 2.0)
        in_top2.append(masks)
        scores.append(sum(jnp.where(masks[i], v[i], 0.0) for i in range(epg)))
    chosen = []
    for g in range(N_GROUPS):
        c = None
        for j in range(N_GROUPS):
            if j != g:
                t = (scores[j] < scores[g]) if j < g else (scores[j] <= scores[g])
                c = t if c is None else jnp.logical_and(c, t)
        chosen.append(c)
    picked, weight = [], []
    for i in range(epg):
        m = None
        w = jnp.zeros_like(scores[0])
        for g in range(N_GROUPS):
            t = jnp.logical_and(chosen[g], in_top2[g][i])
            m = t if m is None else jnp.logical_or(m, t)
            w = w + jnp.where(t, probs[g * epg + i:g * epg + i + 1], 0.0)
        picked.append(m)
        weight.append(w)
    total = weight[0] + weight[1] + weight[2] + weight[3]
    gate = [w / total for w in weight]
    lo = jnp.where(picked[0], 0.0, jnp.where(picked[1], 1.0, 2.0))
    hi = jnp.where(picked[3], 3.0, jnp.where(picked[2], 2.0, 1.0))
    gate_lo = jnp.where(picked[0], gate[0], jnp.where(picked[1], gate[1], gate[2]))
    gate_hi = jnp.where(picked[3], gate[3], jnp.where(picked[2], gate[2], gate[1]))
    group = sum(jnp.where(chosen[g], float(g), 0.0) for g in range(N_GROUPS))
    base = jnp.where(lo == 0.0, 0.0, jnp.where(lo == 1.0, 3.0, 5.0))
    cls = group * float(len(PAIRS_LO)) + base + hi - lo - 1.0
    return cls.astype(I32), gate_lo, gate_hi


def _out_proj_kernel(hf_ref, hb_ref, ag_ref, yb_ref, yc_ref, x_ref, mod_ref, w_ref, lng_ref, lnb_ref,
                     rw_ref, rb_ref, x1_ref, ri_ref, cnt_ref, run_ref, *, alpha, n_ctx_tiles, n_batch):
    tm = x_ref.shape[0]
    ya = (jax.nn.gelu(ag_ref[0]) * (hf_ref[0, 0] + hb_ref[0, 0])).astype(BF16)
    n_a = LRU_WIDTH
    n_b = n_a + GQA_HEADS * GQA_HEAD_DIM
    y = (jnp.dot(ya, w_ref[0:n_a, :], preferred_element_type=F32)
         + jnp.dot(yb_ref[0], w_ref[n_a:n_b, :], preferred_element_type=F32)
         + jnp.dot(yc_ref[0], w_ref[n_b:, :], preferred_element_type=F32))
    x1 = _layer_norm(alpha * x_ref[...] + mod_ref[0, 2:3, :] * y, lng_ref[...], lnb_ref[...])
    d = x1.shape[1]
    x1_ref[:, 0:d] = x1

    h2 = x1 * (1.0 + mod_ref[0, 4:5, :]) + mod_ref[0, 3:4, :]
    h_hi = h2.astype(BF16)
    h_lo = (h2 - h_hi.astype(F32)).astype(BF16)
    both = lax.dot_general(rw_ref[...], h_hi, NT_DIMS, preferred_element_type=F32)
    logits = (both[0:N_EXPERTS] + both[N_EXPERTS:]
              + lax.dot_general(rw_ref[0:N_EXPERTS, :], h_lo, NT_DIMS, preferred_element_type=F32))
    cls, gate_lo, gate_hi = _route(logits, rb_ref[...])

    b = pl.program_id(0)
    i = pl.program_id(1)
    mod_row = jnp.where(i < n_ctx_tiles, n_batch, b).astype(F32)
    extra = jnp.concatenate([gate_lo, gate_hi, jnp.full((1, tm), mod_row, F32), jnp.zeros((LANES - 3, tm), F32)], axis=0)
    x1_ref[:, d:] = extra.T

    @pl.when(jnp.logical_and(b == 0, i == 0))
    def _():
        run_ref[...] = jnp.zeros_like(run_ref)

    member = lax.broadcasted_iota(I32, (CLASS_ROWS, tm), 0) == cls
    before = lax.broadcasted_iota(I32, (tm, tm), 0) < lax.broadcasted_iota(I32, (tm, tm), 1)
    earlier = jnp.dot(member.astype(BF16), before.astype(BF16), preferred_element_type=F32)
    run = run_ref[...]
    rank = jnp.sum(jnp.where(member, earlier + run, 0.0), axis=0, keepdims=True).astype(I32)
    run = run + jnp.sum(member.astype(F32), axis=1, keepdims=True)
    run_ref[...] = run
    cnt_ref[...] = jnp.broadcast_to(run, cnt_ref.shape).astype(I32)
    ri_ref[0] = jnp.concatenate([cls, rank, jnp.zeros((SUBLANES - 2, tm), I32)], axis=0)


def _out_proj(h_lru, zl, yb, yc, xc, mod, w_out, ln_g, ln_b, rw_split, rb, alpha, n_ctx_tiles):
    bsz, sk, _ = zl.shape
    d = xc.shape[1]
    tm = TOKEN_TILE
    nt = sk // tm

    def mod_map(b, i):
        return (jnp.where(i < n_ctx_tiles, bsz, b), 0, 0)

    return pl.pallas_call(
        functools.partial(_out_proj_kernel, alpha=alpha, n_ctx_tiles=n_ctx_tiles, n_batch=bsz),
        out_shape=(
            jax.ShapeDtypeStruct((bsz * sk, d + LANES), F32),
            jax.ShapeDtypeStruct((bsz, SUBLANES, sk), I32),
            jax.ShapeDtypeStruct((CLASS_ROWS, LANES), I32),
        ),
        grid=(bsz, nt),
        in_specs=[
            pl.BlockSpec((1, 1, tm, LRU_WIDTH), lambda b, i: (0, b, i, 0)),
            pl.BlockSpec((1, 1, tm, LRU_WIDTH), lambda b, i: (1, b, i, 0)),
            pl.BlockSpec((1, tm, LRU_WIDTH), lambda b, i: (b, i, 1)),
            pl.BlockSpec((1, tm, GQA_HEADS * GQA_HEAD_DIM), lambda b, i: (b, i, 0)),
            pl.BlockSpec((1, tm, DIFF_HEADS * DIFF_V_DIM), lambda b, i: (b, i, 0)),
            pl.BlockSpec((tm, d), lambda b, i: (b * nt + i, 0)),
            pl.BlockSpec((1, N_MOD, d), mod_map),
            pl.BlockSpec((d, d), lambda b, i: (0, 0)),
            pl.BlockSpec((1, d), lambda b, i: (0, 0)),
            pl.BlockSpec((1, d), lambda b, i: (0, 0)),
            pl.BlockSpec((2 * N_EXPERTS, d), lambda b, i: (0, 0)),
            pl.BlockSpec((N_EXPERTS, 1), lambda b, i: (0, 0)),
        ],
        out_specs=(
            pl.BlockSpec((tm, d + LANES), lambda b, i: (b * nt + i, 0)),
            pl.BlockSpec((1, SUBLANES, tm), lambda b, i: (b, 0, i)),
            pl.BlockSpec((CLASS_ROWS, LANES), lambda b, i: (0, 0)),
        ),
        scratch_shapes=[pltpu.VMEM((CLASS_ROWS, 1), F32)],
        compiler_params=_params(("arbitrary", "arbitrary")),
        name="out_proj_router",
    )(h_lru, h_lru, zl, yb, yc, xc, mod, w_out, ln_g, ln_b, rw_split, rb)


def _moe_kernel(cls_ref, rank_ref, pstart_ref, ea_ref, eb_ref, nv_ref, nu_ref,
                x_hbm, mod_ref, lng_ref, lnb_ref,
                wga_ref, wua_ref, wda_ref, wgb_ref, wub_ref, wdb_ref,
                o_hbm, tok_ref, xbuf, obuf, gsem, ssem, *, alpha, n_batch, n_tok):
    rows = xbuf.shape[1]
    d = obuf.shape[2]
    i = pl.program_id(0)
    n_steps = pl.num_programs(0)
    n_used = nu_ref[0]
    slot = i % 2

    @pl.when(i == 0)
    def _():
        def clear(r, c):
            tok_ref[r] = 0
            return c
        lax.fori_loop(0, tok_ref.shape[0], clear, 0, unroll=8)

        def place(t, c):
            tok_ref[pstart_ref[cls_ref[t]] + rank_ref[t]] = t
            return c
        lax.fori_loop(0, n_tok, place, 0, unroll=8)

        obuf[...] = jnp.zeros_like(obuf)

    def start_gather(step, slot_):
        for r in range(rows):
            pltpu.make_async_copy(x_hbm.at[pl.ds(tok_ref[step * rows + r], 1)], xbuf.at[slot_, pl.ds(r, 1)],
                                  gsem.at[slot_]).start()

    def wait_gather(slot_):
        pltpu.make_async_copy(x_hbm.at[pl.ds(0, rows)], xbuf.at[slot_], gsem.at[slot_]).wait()

    def start_scatter(step, n_valid, slot_):
        for r in range(rows):
            dst = jnp.where(r < n_valid, tok_ref[step * rows + r], n_tok + slot_ * rows + r)
            pltpu.make_async_copy(obuf.at[slot_, pl.ds(r, 1)], o_hbm.at[pl.ds(dst, 1)], ssem.at[slot_]).start()

    def wait_scatter(slot_):
        pltpu.make_async_copy(obuf.at[slot_], o_hbm.at[pl.ds(0, rows)], ssem.at[slot_]).wait()

    def block_step(slot):
        wait_gather(slot)
        start_gather(jnp.minimum(i + 1, n_used - 1), 1 - slot)
        prev = jnp.maximum(i - 1, 0)
        start_scatter(prev, jnp.where(i >= 1, nv_ref[prev], 0), 1 - slot)

        x = xbuf[slot, :, 0:d]
        gate_a = xbuf[slot, :, d:d + 1]
        gate_b = xbuf[slot, :, d + 1:d + 2]
        mid = xbuf[slot, :, d + 2:d + 3]

        def pick(kk):
            v = mod_ref[n_batch, kk:kk + 1, :]
            for b in range(n_batch):
                v = jnp.where(mid == float(b), mod_ref[b, kk:kk + 1, :], v)
            return v

        h = x * (1.0 + pick(4)) + pick(3)

        def ffn(wg, wu, wd):
            a = jnp.dot(h, wg[0], preferred_element_type=F32)
            u = jnp.dot(h, wu[0], preferred_element_type=F32)
            return jnp.dot((a * jax.nn.sigmoid(a)) * u, wd[0], preferred_element_type=F32)

        f = ffn(wga_ref, wua_ref, wda_ref) * gate_a + ffn(wgb_ref, wub_ref, wdb_ref) * gate_b
        y = _layer_norm(alpha * x + pick(5) * f, lng_ref[...], lnb_ref[...])
        wait_scatter(slot)
        obuf[slot] = y

    @pl.when(i == 0)
    def _():
        pltpu.make_async_copy(obuf.at[0], o_hbm.at[pl.ds(n_tok, rows)], ssem.at[0]).start()
        start_gather(0, 0)

    for parity in range(2):
        @pl.when(jnp.logical_and(i < n_used, i % 2 == parity))
        def _(parity=parity):
            block_step(parity)

    @pl.when(i == n_steps - 1)
    def _():
        last = n_used - 1
        q = last % 2
        start_scatter(last, nv_ref[last], q)
        wait_scatter(q)
        wait_scatter(1 - q)
        wait_gather(1 - q)


def _moe(x_wide, n_tok, cls, rank, pstart, blk_ea, blk_eb, blk_nv, n_used, mod, ln_g, ln_b,
         w_gate, w_up, w_down, layer, alpha, n_batch):
    d = x_wide.shape[1] - LANES
    rows = MOE_ROWS
    n_blk = blk_ea.shape[0]
    ff = w_gate.shape[-1]
    first = layer * N_EXPERTS

    def wa(i, cl, rk, ps, ea, eb, nv, nu):
        return (first + ea[i], 0, 0)

    def wb(i, cl, rk, ps, ea, eb, nv, nu):
        return (first + eb[i], 0, 0)

    const2 = lambda i, cl, rk, ps, ea, eb, nv, nu: (0, 0)
    grid_spec = pltpu.PrefetchScalarGridSpec(
        num_scalar_prefetch=7,
        grid=(n_blk,),
        in_specs=[
            pl.BlockSpec(memory_space=pl.ANY),
            pl.BlockSpec(mod.shape, lambda i, cl, rk, ps, ea, eb, nv, nu: (0, 0, 0)),
            pl.BlockSpec((1, d), const2),
            pl.BlockSpec((1, d), const2),
            pl.BlockSpec((1, d, ff), wa),
            pl.BlockSpec((1, d, ff), wa),
            pl.BlockSpec((1, ff, d), wa),
            pl.BlockSpec((1, d, ff), wb),
            pl.BlockSpec((1, d, ff), wb),
            pl.BlockSpec((1, ff, d), wb),
        ],
        out_specs=pl.BlockSpec(memory_space=pl.ANY),
        scratch_shapes=[
            pltpu.SMEM((n_blk * rows,), I32),
            pltpu.VMEM((2, rows, d + LANES), F32),
            pltpu.VMEM((2, rows, d), F32),
            pltpu.SemaphoreType.DMA((2,)),
            pltpu.SemaphoreType.DMA((2,)),
        ],
    )
    return pl.pallas_call(
        functools.partial(_moe_kernel, alpha=alpha, n_batch=n_batch, n_tok=n_tok),
        out_shape=jax.ShapeDtypeStruct((n_tok + 2 * rows, d), F32),
        grid_spec=grid_spec,
        compiler_params=_params(("arbitrary",)),
        name="moe_experts",
    )(cls, rank, pstart, blk_ea, blk_eb, blk_nv, n_used, x_wide, mod, ln_g, ln_b,
      w_gate, w_up, w_down, w_gate, w_up, w_down)


def _block_tables(counts, n_tok):
    rows = MOE_ROWS
    n_blk = n_tok // rows + N_CLASSES
    n_pairs = len(PAIRS_LO)
    padded = (counts + rows - 1) // rows * rows
    pend = jnp.cumsum(padded)
    pstart = pend - padded
    n_used = pend[-1] // rows
    blk = jnp.arange(n_blk, dtype=I32)
    blk_ids = jnp.minimum(blk, jnp.maximum(n_used - 1, 0))
    blk_cls = jnp.minimum(jnp.sum((blk_ids[:, None] * rows >= pend[None, :]).astype(I32), axis=1), N_CLASSES - 1)
    onehot = blk_cls[:, None] == jnp.arange(N_CLASSES, dtype=I32)[None, :]
    within = blk_ids * rows - jnp.sum(jnp.where(onehot, pstart[None, :], 0), axis=1)
    count_b = jnp.sum(jnp.where(onehot, counts[None, :], 0), axis=1)
    blk_nv = jnp.where(blk < n_used, jnp.clip(count_b - within, 0, rows), 0).astype(I32)
    pair = blk_cls % n_pairs
    lo = sum(jnp.where(pair == p, PAIRS_LO[p], 0) for p in range(n_pairs))
    hi = sum(jnp.where(pair == p, PAIRS_HI[p], 0) for p in range(n_pairs))
    first = blk_cls // n_pairs * EXPERTS_PER_GROUP
    pstart_pad = jnp.concatenate([pstart, jnp.zeros((CLASS_ROWS - N_CLASSES,), I32)]).astype(I32)
    return (pstart_pad, (first + lo).astype(I32), (first + hi).astype(I32), blk_nv,
            n_used.reshape(1).astype(I32))


def _rope_tables(n_lat, n_ctx, dim):
    rows = n_lat // GRID_W
    row = jnp.repeat(jnp.arange(rows, dtype=F32), GRID_W)
    col = jnp.tile(jnp.arange(GRID_W, dtype=F32), rows)
    n_freq = dim // 4
    inv = ROPE_THETA ** (-jnp.arange(n_freq, dtype=F32) / n_freq)
    ang = jnp.concatenate([row[:, None] * inv, col[:, None] * inv], axis=-1)
    cos = jnp.repeat(jnp.cos(ang), 2, axis=-1)
    sin = jnp.repeat(jnp.sin(ang), 2, axis=-1)
    even = (jnp.arange(dim) % 2 == 0)
    tabs = jnp.stack([cos, jnp.where(even, -sin, 0.0), jnp.where(even, 0.0, sin)])
    ident = jnp.stack([jnp.ones((n_ctx, dim), F32), jnp.zeros((n_ctx, dim), F32), jnp.zeros((n_ctx, dim), F32)])
    return jnp.tile(jnp.concatenate([ident, tabs], axis=1), (1, 1, LANES // dim))


def _block_diag(w):
    n, c, _ = w.shape
    eye = jnp.eye(n, dtype=w.dtype)
    return (eye[:, None, :, None] * w[:, :, None, :]).reshape(n * c, n * c)


def kernel(x, c, ctx, c_ctx, ada_w, ada_b, w_in, conv_w, conv_b, rg_wa, rg_ba, rg_wx, rg_bx, rg_lam, q_norm_g, k_norm_g, diff_lambda, diff_subln_g, w_out, ln1_g, ln1_b, ln2_g, ln2_b, router_w, router_b, exp_w_gate, exp_w_up, exp_w_down):
    bsz, s, d = x.shape
    n_ctx = ctx.shape[1]
    depth = w_in.shape[0]
    sk = n_ctx + s
    assert d == D_MODEL and n_ctx == TOKEN_TILE and s % TOKEN_TILE == 0 and bsz + 1 <= SUBLANES
    n_ctx_tiles = n_ctx // TOKEN_TILE
    n_tok = bsz * sk
    alpha = (2.0 * depth) ** 0.25

    c_rows = jnp.concatenate([c, c_ctx[None, :], jnp.zeros((SUBLANES - bsz - 1, d), F32)], axis=0)
    mods = _ada_modulation(c_rows, ada_w, ada_b).reshape(depth, SUBLANES, N_MOD, d)

    rope_g = _rope_tables(s, n_ctx, GQA_HEAD_DIM)
    rope_d = _rope_tables(s, n_ctx, DIFF_QK_DIM)
    lane = np.arange(LANES)
    seg = jnp.asarray(lane[:, None] // GQA_HEAD_DIM == lane[None, :] // GQA_HEAD_DIM, BF16)
    rw_t = router_w.T
    rw_hi = rw_t.astype(BF16)
    rw_split = jnp.concatenate([rw_hi, (rw_t - rw_hi.astype(F32)).astype(BF16)], axis=0)
    rb = router_b.reshape(N_EXPERTS, 1)
    zero_aux = jnp.zeros((SUBLANES, LANES), F32)

    xc = jnp.concatenate([jnp.concatenate([ctx, x], axis=1).reshape(n_tok, d),
                          jnp.zeros((2 * MOE_ROWS, d), F32)], axis=0)
    w_gate_all = exp_w_gate.reshape(depth * N_EXPERTS, d, EXPERT_FF)
    w_up_all = exp_w_up.reshape(depth * N_EXPERTS, d, EXPERT_FF)
    w_down_all = exp_w_down.reshape(depth * N_EXPERTS, EXPERT_FF, d)
    for li in range(depth):
        lam_init = 0.8 - 0.6 * math.exp(-0.3 * li)
        mod = mods[li]
        gains = jnp.concatenate([jnp.tile(jnp.tile(q_norm_g[li], 2)[None, :], (4, 1)),
                                 jnp.tile(k_norm_g[li], 2)[None, :], jnp.zeros((3, LANES), F32)], axis=0)
        zl, q, k, vt, dq, dk, dvt = _in_proj(xc, bsz, sk, mod, w_in[li].astype(BF16), gains, seg, rope_g, rope_d,
                                             n_ctx_tiles)

        w_gates = jnp.stack([jnp.concatenate([_block_diag(rg_wa[li, dd]), _block_diag(rg_wx[li, dd])], axis=1)
                             for dd in range(2)]).astype(BF16)
        b_gates = jnp.concatenate([rg_ba[li], rg_bx[li]], axis=-1)[:, None, :]
        h_lru = _lru(zl, conv_w[li], conv_b[li][None, :], w_gates, b_gates, rg_lam[li][:, None, :])

        yb = _attention(q, k, vt, zero_aux, diff=False, n_ctx_tiles=n_ctx_tiles)
        aux = jnp.concatenate([
            jnp.pad(diff_lambda[li], ((0, 0), (0, LANES - DIFF_QK_DIM))),
            jnp.full((1, LANES), lam_init, F32),
            jnp.tile(diff_subln_g[li], 2)[None, :],
            jnp.zeros((2, LANES), F32)], axis=0)
        yc = _attention(dq, dk, dvt, aux, diff=True, n_ctx_tiles=n_ctx_tiles)

        x1, route_i, counts = _out_proj(h_lru, zl, yb, yc, xc, mod, w_out[li].astype(BF16), ln1_g[li][None, :],
                                        ln1_b[li][None, :], rw_split, rb, alpha, n_ctx_tiles)

        pstart, blk_ea, blk_eb, blk_nv, n_used = _block_tables(counts[:N_CLASSES, 0], n_tok)
        xc = _moe(x1, n_tok, route_i[:, 0, :].reshape(n_tok), route_i[:, 1, :].reshape(n_tok), pstart,
                  blk_ea, blk_eb, blk_nv, n_used, mod, ln2_g[li][None, :], ln2_b[li][None, :],
                  w_gate_all, w_up_all, w_down_all, li, alpha, bsz)
    return xc[:n_tok].reshape(bsz, sk, d)[:, n_ctx:, :]
```

```python
import functools
import math

import numpy as np
import jax
import jax.numpy as jnp
from jax import lax
from jax.experimental import pallas as pl
from jax.experimental.pallas import tpu as pltpu

F32 = jnp.float32
BF16 = jnp.bfloat16
I32 = jnp.int32

D_MODEL = 1024
GRID_W = 64
LRU_WIDTH = 256
LRU_BLOCKS = 4
CONV_W = 4
RG_C = 8.0
GQA_HEADS = 8
GQA_KV_HEADS = 2
GQA_HEAD_DIM = 64
DIFF_HEADS = 4
DIFF_QK_DIM = 32
DIFF_V_DIM = 64
IN_WIDTH = 2048
N_EXPERTS = 16
N_GROUPS = 4
EXPERTS_PER_GROUP = 4
EXPERT_FF = 512
ROPE_THETA = 10000.0
EPS = 1e-6
N_MOD = 6
LOG2_E = math.log2(math.e)

COL_AX, COL_AG, COL_GQ, COL_GK, COL_GV, COL_DQ, COL_DK, COL_DV = 0, 256, 512, 1024, 1152, 1280, 1536, 1792

LANES = 128
SUBLANES = 8
VMEM_LIMIT = 56 * 1024 * 1024

TOKEN_TILE = 256
VT_ROWS = 80
PAIRS_A = (0, 0, 0, 1, 1, 3)
PAIRS_B = (1, 2, 3, 3, 2, 2)
N_CLASSES = N_GROUPS * len(PAIRS_A)
CLASS_ROWS = 32
MOE_ROWS = 128
KEY_TILES_PER_TRIP = 8
MAX_SHIFT_LAG = 60.0
MAX_SHIFT_RAISE = 120.0

NT_DIMS = (((1,), (1,)), ((), ()))


def _params(semantics):
    return pltpu.CompilerParams(dimension_semantics=semantics, vmem_limit_bytes=VMEM_LIMIT)


def _ada_kernel(c_ref, w_ref, b_ref, o_ref):
    c = c_ref[...]
    o_ref[0] = jnp.dot(c * jax.nn.sigmoid(c), w_ref[0], preferred_element_type=F32) + b_ref[0]


def _ada_modulation(c_rows, ada_w, ada_b):
    depth, d, n = ada_w.shape
    tn = 1536
    return pl.pallas_call(
        _ada_kernel,
        out_shape=jax.ShapeDtypeStruct((depth, SUBLANES, n), F32),
        grid=(depth, n // tn),
        in_specs=[
            pl.BlockSpec((SUBLANES, d), lambda l, j: (0, 0)),
            pl.BlockSpec((1, d, tn), lambda l, j: (l, 0, j)),
            pl.BlockSpec((1, 1, tn), lambda l, j: (l, 0, j)),
        ],
        out_specs=pl.BlockSpec((1, SUBLANES, tn), lambda l, j: (l, 0, j)),
        compiler_params=_params(("arbitrary", "arbitrary")),
        name="ada_modulation",
    )(c_rows, ada_w, ada_b.reshape(depth, 1, n))


def _rotate_pairs(v, cos, sin_next, sin_prev):
    return v * cos + pltpu.roll(v, LANES - 1, 1) * sin_next + pltpu.roll(v, 1, 1) * sin_prev


def _in_proj_kernel(x_ref, mod_ref, w_ref, gain_ref, seg_ref, rope_g_ref, rope_d_ref,
                    zl_ref, q_ref, k_ref, vt_ref, dq_ref, dk_ref, dvt_ref):
    tm = x_ref.shape[0]
    x = x_ref[...]
    shift = mod_ref[0, 0:1, :]
    scale = mod_ref[0, 1:2, :]
    h = (x * (1.0 + scale) + shift).astype(BF16)
    z = jnp.dot(h, w_ref[...], preferred_element_type=F32)
    zl_ref[0] = z[:, COL_AX:COL_GQ]

    ones_rows = jnp.where(lax.broadcasted_iota(I32, (VT_ROWS - GQA_HEAD_DIM, tm), 0) == 0, 1.0, 0.0).astype(BF16)

    seg = seg_ref[...]
    cos, s_next, s_prev = rope_g_ref[0], rope_g_ref[1], rope_g_ref[2]
    for g in range(5):
        v = z[:, COL_GQ + g * LANES:COL_GQ + (g + 1) * LANES]
        sq = v * v
        sq_hi = sq.astype(BF16)
        sq_lo = (sq - sq_hi.astype(F32)).astype(BF16)
        ssq = jnp.dot(sq_hi, seg, preferred_element_type=F32) + jnp.dot(sq_lo, seg, preferred_element_type=F32)
        v = v * lax.rsqrt(ssq * (1.0 / GQA_HEAD_DIM) + EPS) * gain_ref[g:g + 1, :]
        v = _rotate_pairs(v, cos, s_next, s_prev)
        if g < 4:
            v = (v * (GQA_HEAD_DIM ** -0.5 * LOG2_E)).astype(BF16)
            q_ref[0, 2 * g] = v[:, :GQA_HEAD_DIM]
            q_ref[0, 2 * g + 1] = v[:, GQA_HEAD_DIM:]
        else:
            v = v.astype(BF16)
            k_ref[0, 0] = v[:, :GQA_HEAD_DIM]
            k_ref[0, 1] = v[:, GQA_HEAD_DIM:]
    vt = z[:, COL_GV:COL_DQ].T.astype(BF16)
    for hd in range(GQA_KV_HEADS):
        vt_ref[0, hd, 0, 0:GQA_HEAD_DIM, :] = vt[hd * GQA_HEAD_DIM:(hd + 1) * GQA_HEAD_DIM]
        vt_ref[0, hd, 0, GQA_HEAD_DIM:VT_ROWS, :] = ones_rows

    cos, s_next, s_prev = rope_d_ref[0], rope_d_ref[1], rope_d_ref[2]
    for g in range(4):
        v = _rotate_pairs(z[:, COL_DQ + g * LANES:COL_DQ + (g + 1) * LANES], cos, s_next, s_prev)
        if g < 2:
            v = v * (DIFF_QK_DIM ** -0.5 * LOG2_E)
        v = v.astype(BF16)
        dst = dq_ref if g < 2 else dk_ref
        for j in range(4):
            dst[0, (g % 2) * 4 + j] = v[:, j * DIFF_QK_DIM:(j + 1) * DIFF_QK_DIM]
    dvt = z[:, COL_DV:IN_WIDTH].T.astype(BF16)
    for hd in range(DIFF_HEADS):
        dvt_ref[0, hd, 0, 0:DIFF_V_DIM, :] = dvt[hd * DIFF_V_DIM:(hd + 1) * DIFF_V_DIM]
        dvt_ref[0, hd, 0, DIFF_V_DIM:VT_ROWS, :] = ones_rows


def _in_proj(xc, bsz, sk, mod, w_in, gains, seg, rope_g, rope_d, n_ctx_tiles):
    d = xc.shape[1]
    tm = TOKEN_TILE
    nt = sk // tm
    n_mod_rows = mod.shape[0]

    def mod_map(b, i):
        return (jnp.where(i < n_ctx_tiles, bsz, b), 0, 0)

    assert n_mod_rows > bsz
    return pl.pallas_call(
        _in_proj_kernel,
        out_shape=(
            jax.ShapeDtypeStruct((bsz, sk, COL_GQ), F32),
            jax.ShapeDtypeStruct((bsz, GQA_HEADS, sk, GQA_HEAD_DIM), BF16),
            jax.ShapeDtypeStruct((bsz, GQA_KV_HEADS, sk, GQA_HEAD_DIM), BF16),
            jax.ShapeDtypeStruct((bsz, GQA_KV_HEADS, nt, VT_ROWS, tm), BF16),
            jax.ShapeDtypeStruct((bsz, 2 * DIFF_HEADS, sk, DIFF_QK_DIM), BF16),
            jax.ShapeDtypeStruct((bsz, 2 * DIFF_HEADS, sk, DIFF_QK_DIM), BF16),
            jax.ShapeDtypeStruct((bsz, DIFF_HEADS, nt, VT_ROWS, tm), BF16),
        ),
        grid=(bsz, nt),
        in_specs=[
            pl.BlockSpec((tm, d), lambda b, i: (b * nt + i, 0)),
            pl.BlockSpec((1, N_MOD, d), mod_map),
            pl.BlockSpec((d, IN_WIDTH), lambda b, i: (0, 0)),
            pl.BlockSpec((SUBLANES, LANES), lambda b, i: (0, 0)),
            pl.BlockSpec((LANES, LANES), lambda b, i: (0, 0)),
            pl.BlockSpec((3, tm, LANES), lambda b, i: (0, i, 0)),
            pl.BlockSpec((3, tm, LANES), lambda b, i: (0, i, 0)),
        ],
        out_specs=(
            pl.BlockSpec((1, tm, COL_GQ), lambda b, i: (b, i, 0)),
            pl.BlockSpec((1, GQA_HEADS, tm, GQA_HEAD_DIM), lambda b, i: (b, 0, i, 0)),
            pl.BlockSpec((1, GQA_KV_HEADS, tm, GQA_HEAD_DIM), lambda b, i: (b, 0, i, 0)),
            pl.BlockSpec((1, GQA_KV_HEADS, 1, VT_ROWS, tm), lambda b, i: (b, 0, i, 0, 0)),
            pl.BlockSpec((1, 2 * DIFF_HEADS, tm, DIFF_QK_DIM), lambda b, i: (b, 0, i, 0)),
            pl.BlockSpec((1, 2 * DIFF_HEADS, tm, DIFF_QK_DIM), lambda b, i: (b, 0, i, 0)),
            pl.BlockSpec((1, DIFF_HEADS, 1, VT_ROWS, tm), lambda b, i: (b, 0, i, 0, 0)),
        ),
        compiler_params=_params(("arbitrary", "arbitrary")),
        name="in_proj",
    )(xc, mod, w_in, gains, seg, rope_g, rope_d)


def _expm1(x):
    u = jnp.exp(x)
    return jnp.where(u == 1.0, x, (u - 1.0) * x / jnp.log(u))


def _lru_kernel(ax_ref, prev_ref, next_ref, cw_ref, cb_ref, w_ref, b_ref, lam_ref, h_ref, carry_ref, *, nt):
    tm = ax_ref.shape[1]
    d = pl.program_id(1)
    j = pl.program_id(2)
    blk = jnp.where(d == 0, j, jnp.where(j == 0, 0, nt - j))

    @pl.when(j == 0)
    def _():
        carry_ref[...] = jnp.zeros_like(carry_ref)

    x = ax_ref[0]
    no_left = jnp.logical_or(blk == 0, blk == 1)
    no_right = jnp.logical_or(blk == 0, blk == nt - 1)
    left = jnp.where(no_left, 0.0, prev_ref[0, SUBLANES - 1:SUBLANES, :])
    right0 = jnp.where(no_right, 0.0, next_ref[0, 0:1, :])
    right1 = jnp.where(no_right, 0.0, next_ref[0, 1:2, :])
    row = lax.broadcasted_iota(I32, (tm, 1), 0)
    x_m1 = jnp.where(row == 0, left, pltpu.roll(x, 1, 0))
    x_p1 = jnp.where(row == tm - 1, right0, pltpu.roll(x, tm - 1, 0))
    x_p2 = jnp.where(row == tm - 2, right0, jnp.where(row == tm - 1, right1, pltpu.roll(x, tm - 2, 0)))
    u = cb_ref[...] + x_m1 * cw_ref[0:1, :] + x * cw_ref[1:2, :] + x_p1 * cw_ref[2:3, :] + x_p2 * cw_ref[3:4, :]

    g = jnp.dot(u.astype(BF16), w_ref[0], preferred_element_type=F32) + b_ref[0]
    r = jax.nn.sigmoid(g[:, :LRU_WIDTH])
    gate_in = jax.nn.sigmoid(g[:, LRU_WIDTH:])
    neg_lam = -lam_ref[0]
    softplus = jnp.maximum(neg_lam, 0.0) + jnp.log1p(jnp.exp(-jnp.abs(neg_lam)))
    log_a = (-RG_C * r) * softplus
    a = jnp.exp(log_a)
    b = jnp.sqrt(-_expm1(2.0 * log_a)) * (gate_in * u)

    def scan(reverse):
        aa, bb = a, b
        in_group = row % SUBLANES
        s = 1
        while s < SUBLANES:
            shift = tm - s if reverse else s
            keep = (in_group < SUBLANES - s) if reverse else (in_group >= s)
            a_sh = jnp.where(keep, pltpu.roll(aa, shift, 0), 1.0)
            b_sh = jnp.where(keep, pltpu.roll(bb, shift, 0), 0.0)
            bb = aa * b_sh + bb
            aa = aa * a_sh
            s *= 2
        state = carry_ref[...]
        n_groups = tm // SUBLANES
        for g in (range(n_groups - 1, -1, -1) if reverse else range(n_groups)):
            lo = g * SUBLANES
            h = aa[lo:lo + SUBLANES] * state + bb[lo:lo + SUBLANES]
            h_ref[0, 0, lo:lo + SUBLANES, :] = h
            state = h[0:1, :] if reverse else h[SUBLANES - 1:SUBLANES, :]
        carry_ref[...] = state

    @pl.when(d == 0)
    def _():
        scan(False)

    @pl.when(d == 1)
    def _():
        scan(True)


def _lru(zl, conv_w, conv_b, w_gates, b_gates, lam):
    bsz, sk, _ = zl.shape
    tm = TOKEN_TILE
    nt = sk // tm
    per = tm // SUBLANES

    def blk_of(d, j):
        return jnp.where(d == 0, j, jnp.where(j == 0, 0, nt - j))

    return pl.pallas_call(
        functools.partial(_lru_kernel, nt=nt),
        out_shape=jax.ShapeDtypeStruct((2, bsz, sk, LRU_WIDTH), F32),
        grid=(bsz, 2, nt),
        in_specs=[
            pl.BlockSpec((1, tm, LRU_WIDTH), lambda b, d, j: (b, blk_of(d, j), 0)),
            pl.BlockSpec((1, SUBLANES, LRU_WIDTH), lambda b, d, j: (b, jnp.maximum(blk_of(d, j) * per - 1, 0), 0)),
            pl.BlockSpec((1, SUBLANES, LRU_WIDTH),
                         lambda b, d, j: (b, jnp.minimum((blk_of(d, j) + 1) * per, nt * per - 1), 0)),
            pl.BlockSpec((CONV_W, LRU_WIDTH), lambda b, d, j: (0, 0)),
            pl.BlockSpec((1, LRU_WIDTH), lambda b, d, j: (0, 0)),
            pl.BlockSpec((1, LRU_WIDTH, 2 * LRU_WIDTH), lambda b, d, j: (d, 0, 0)),
            pl.BlockSpec((1, 1, 2 * LRU_WIDTH), lambda b, d, j: (d, 0, 0)),
            pl.BlockSpec((1, 1, LRU_WIDTH), lambda b, d, j: (d, 0, 0)),
        ],
        out_specs=pl.BlockSpec((1, 1, tm, LRU_WIDTH), lambda b, d, j: (d, b, blk_of(d, j), 0)),
        scratch_shapes=[pltpu.VMEM((1, LRU_WIDTH), F32)],
        compiler_params=_params(("arbitrary", "arbitrary", "arbitrary")),
        name="rg_lru",
    )(zl, zl, zl, conv_w, conv_b, w_gates, b_gates, lam)


def _attn_kernel(q_ref, k_ref, vt_ref, aux_ref, o_ref, s_ref, m_ref, acc_ref, *, k_heads, v_heads, nt, n_ctx_tiles, diff):
    tq = q_ref.shape[2]
    tk = tq
    dv = GQA_HEAD_DIM
    qi = pl.program_id(2)
    n_k = jnp.where(qi < n_ctx_tiles, n_ctx_tiles, nt)

    def scores(kt):
        start = pl.multiple_of(kt * tk, tk)
        return jnp.concatenate(
            [lax.dot_general(k_ref[0, k_heads[j], pl.ds(start, tk), :], q_ref[0, j], NT_DIMS,
                             preferred_element_type=F32) for j in range(4)], axis=1)

    def weighted_values(kt, p):
        return jnp.concatenate(
            [jnp.dot(vt_ref[0, v_heads[j], kt], p[:, j * tq:(j + 1) * tq], preferred_element_type=F32)
             for j in range(4)], axis=1)

    def col_max(s, m):
        return jnp.maximum(m, jnp.max(s, axis=0, keepdims=True))

    s_first = scores(0)
    s_ref[0] = s_first
    m_ref[...] = jnp.max(s_first, axis=0, keepdims=True)
    acc_ref[0] = jnp.zeros(acc_ref.shape[1:], F32)

    def trip(carry):
        i, cur = carry
        kt = KEY_TILES_PER_TRIP * i
        m0 = m_ref[...]
        s_cur = s_ref[cur]
        peak = jnp.zeros((1, 4 * tq), BF16)
        acc = acc_ref[cur]
        for u in range(KEY_TILES_PER_TRIP):
            s_next = scores(kt + u + 1)
            p = jnp.exp2(s_cur - m0).astype(BF16)
            peak = jnp.maximum(peak, jnp.max(p, axis=0, keepdims=True))
            acc = acc + weighted_values(kt + u, p)
            s_cur = s_next
        s_ref[1 - cur] = s_cur
        acc_ref[1 - cur] = acc
        peak = peak.astype(F32)
        lagging = jnp.max(peak) > 2.0 ** MAX_SHIFT_LAG

        @pl.when(lagging)
        def _():
            m_new = m0 + jnp.clip(jnp.log(peak) * LOG2_E, 0.0, MAX_SHIFT_RAISE)
            acc_ref[cur] = acc_ref[cur] * jnp.exp2(m0 - m_new)
            m_ref[...] = m_new

        return jnp.where(lagging, i, i + 1), jnp.where(lagging, cur, 1 - cur)

    n_trips = (n_k - 1) // KEY_TILES_PER_TRIP
    _, cur = lax.while_loop(lambda c: c[0] < n_trips, trip, (jnp.int32(0), jnp.int32(0)))

    s0 = s_ref[cur]
    m0 = m_ref[...]
    m1 = col_max(s0, m0)
    p0 = jnp.exp2(s0 - m1).astype(BF16)
    acc_all = acc_ref[cur] * jnp.exp2(m0 - m1) + weighted_values(n_k - 1, p0)

    outs = []
    for j in range(4):
        acc = acc_all[:, j * tq:(j + 1) * tq]
        outs.append(acc[0:dv] / acc[dv:dv + 1])
    if diff:
        dl = aux_ref[0:4, :]
        lam_init = aux_ref[4:5, 0:1]
        lam = (jnp.exp(jnp.sum(dl[0:1] * dl[1:2], axis=1, keepdims=True))
               - jnp.exp(jnp.sum(dl[2:3] * dl[3:4], axis=1, keepdims=True)) + lam_init)
        heads = []
        for hd in range(2):
            o = outs[2 * hd] - lam * outs[2 * hd + 1]
            heads.append(o * lax.rsqrt(jnp.mean(o * o, axis=0, keepdims=True) + EPS))
        o_ref[0] = ((jnp.concatenate(heads, axis=0).T * aux_ref[5:6, :]) * (1.0 - lam_init)).astype(o_ref.dtype)
    else:
        for pair in range(2):
            o_ref[0, :, pair * 2 * dv:(pair + 1) * 2 * dv] = (
                jnp.concatenate(outs[2 * pair:2 * pair + 2], axis=0).T.astype(o_ref.dtype))


def _attention(q, k, vt, aux, *, diff, n_ctx_tiles):
    bsz, n_q, sk, dh = q.shape
    n_k_heads = k.shape[1]
    n_v_heads = vt.shape[1]
    tq = TOKEN_TILE
    nt = sk // tq
    assert (nt - 1) % KEY_TILES_PER_TRIP == 0 and (n_ctx_tiles - 1) % KEY_TILES_PER_TRIP == 0
    groups = n_q // 4
    kpg = n_k_heads // groups
    vpg = n_v_heads // groups
    k_heads = tuple(j * kpg // 4 for j in range(4))
    v_heads = tuple(j * vpg // 4 for j in range(4))
    width = 2 * DIFF_V_DIM if diff else 4 * GQA_HEAD_DIM
    return pl.pallas_call(
        functools.partial(_attn_kernel, k_heads=k_heads, v_heads=v_heads, nt=nt, n_ctx_tiles=n_ctx_tiles, diff=diff),
        out_shape=jax.ShapeDtypeStruct((bsz, sk, groups * width), BF16),
        grid=(bsz, groups, nt),
        in_specs=[
            pl.BlockSpec((1, 4, tq, dh), lambda b, g, i: (b, g, i, 0)),
            pl.BlockSpec((1, kpg, sk, dh), lambda b, g, i: (b, g, 0, 0)),
            pl.BlockSpec((1, vpg, nt, VT_ROWS, tq), lambda b, g, i: (b, g, 0, 0, 0)),
            pl.BlockSpec((SUBLANES, LANES), lambda b, g, i: (0, 0)),
        ],
        out_specs=pl.BlockSpec((1, tq, width), lambda b, g, i: (b, i, g)),
        scratch_shapes=[pltpu.VMEM((2, tq, 4 * tq), F32), pltpu.VMEM((1, 4 * tq), F32),
                        pltpu.VMEM((2, VT_ROWS, 4 * tq), F32)],
        compiler_params=_params(("arbitrary", "arbitrary", "arbitrary")),
        name="diff_attention" if diff else "gqa_attention",
    )(q, k, vt, aux)


def _layer_norm(r, g, b):
    mu = jnp.mean(r, axis=-1, keepdims=True)
    c = r - mu
    var = jnp.mean(c * c, axis=-1, keepdims=True)
    return c * lax.rsqrt(var + EPS) * g + b


def _route(logits, bias):
    mx = jnp.max(logits, axis=0, keepdims=True)
    e = jnp.exp(logits - mx)
    probs = e / jnp.sum(e, axis=0, keepdims=True)
    sel = probs + bias
    epg = EXPERTS_PER_GROUP
    in_top2, scores = [], []
    for g in range(N_GROUPS):
        v = [sel[g * epg + i:g * epg + i + 1] for i in range(epg)]
        masks = []
        for i in range(epg):
            rank = jnp.zeros_like(v[i])
            for j in range(epg):
                if j != i:
                    beats = (v[j] > v[i]) if j > i else (v[j] >= v[i])
                    rank = rank + beats.astype(F32)
            masks.append(rank < 2.0)
        in_top2.append(masks)
        scores.append(sum(jnp.where(masks[i], v[i], 0.0) for i in range(epg)))
    chosen = []
    for g in range(N_GROUPS):
        c = None
        for j in range(N_GROUPS):
            if j != g:
                t = (scores[j] < scores[g]) if j < g else (scores[j] <= scores[g])
                c = t if c is None else jnp.logical_and(c, t)
        chosen.append(c)
    picked, weight = [], []
    for i in range(epg):
        m = None
        w = jnp.zeros_like(scores[0])
        for g in range(N_GROUPS):
            t = jnp.logical_and(chosen[g], in_top2[g][i])
            m = t if m is None else jnp.logical_or(m, t)
            w = w + jnp.where(t, probs[g * epg + i:g * epg + i + 1], 0.0)
        picked.append(m)
        weight.append(w)
    total = weight[0] + weight[1] + weight[2] + weight[3]
    gate = [w / total for w in weight]
    lo = jnp.where(picked[0], 0.0, jnp.where(picked[1], 1.0, 2.0))
    hi = jnp.where(picked[3], 3.0, jnp.where(picked[2], 2.0, 1.0))
    gate_lo = jnp.where(picked[0], gate[0], jnp.where(picked[1], gate[1], gate[2]))
    gate_hi = jnp.where(picked[3], gate[3], jnp.where(picked[2], gate[2], gate[1]))
    group = sum(jnp.where(chosen[g], float(g), 0.0) for g in range(N_GROUPS))
    base = jnp.where(lo == 0.0, 0.0, jnp.where(lo == 1.0, 3.0, 5.0))
    natural = base + hi - lo - 1.0
    pair = jnp.where(natural == 3.0, 4.0, jnp.where(natural == 4.0, 3.0, natural))
    swapped = natural == 5.0
    cls = group * float(len(PAIRS_A)) + pair
    return cls.astype(I32), jnp.where(swapped, gate_hi, gate_lo), jnp.where(swapped, gate_lo, gate_hi)


def _out_proj_kernel(hf_ref, hb_ref, ag_ref, yb_ref, yc_ref, x_ref, mod_ref, w_ref, lng_ref, lnb_ref,
                     rw_ref, rb_ref, x1_ref, ri_ref, cnt_ref, run_ref, *, alpha, n_ctx_tiles, n_batch):
    tm = x_ref.shape[0]
    ya = (jax.nn.gelu(ag_ref[0]) * (hf_ref[0, 0] + hb_ref[0, 0])).astype(BF16)
    n_a = LRU_WIDTH
    n_b = n_a + GQA_HEADS * GQA_HEAD_DIM
    y = (jnp.dot(ya, w_ref[0:n_a, :], preferred_element_type=F32)
         + jnp.dot(yb_ref[0], w_ref[n_a:n_b, :], preferred_element_type=F32)
         + jnp.dot(yc_ref[0], w_ref[n_b:, :], preferred_element_type=F32))
    x1 = _layer_norm(alpha * x_ref[...] + mod_ref[0, 2:3, :] * y, lng_ref[...], lnb_ref[...])
    d = x1.shape[1]
    x1_ref[:, 0:d] = x1

    h2 = x1 * (1.0 + mod_ref[0, 4:5, :]) + mod_ref[0, 3:4, :]
    h_hi = h2.astype(BF16)
    h_lo = (h2 - h_hi.astype(F32)).astype(BF16)
    both = lax.dot_general(rw_ref[...], h_hi, NT_DIMS, preferred_element_type=F32)
    logits = (both[0:N_EXPERTS] + both[N_EXPERTS:]
              + lax.dot_general(rw_ref[0:N_EXPERTS, :], h_lo, NT_DIMS, preferred_element_type=F32))
    cls, gate_lo, gate_hi = _route(logits, rb_ref[...])

    b = pl.program_id(0)
    i = pl.program_id(1)
    mod_row = jnp.where(i < n_ctx_tiles, n_batch, b).astype(F32)
    extra = jnp.concatenate([gate_lo, gate_hi, jnp.full((1, tm), mod_row, F32), jnp.zeros((LANES - 3, tm), F32)], axis=0)
    x1_ref[:, d:] = extra.T

    @pl.when(jnp.logical_and(b == 0, i == 0))
    def _():
        run_ref[...] = jnp.zeros_like(run_ref)

    member = lax.broadcasted_iota(I32, (CLASS_ROWS, tm), 0) == cls
    before = lax.broadcasted_iota(I32, (tm, tm), 0) < lax.broadcasted_iota(I32, (tm, tm), 1)
    earlier = jnp.dot(member.astype(BF16), before.astype(BF16), preferred_element_type=F32)
    run = run_ref[...]
    rank = jnp.sum(jnp.where(member, earlier + run, 0.0), axis=0, keepdims=True).astype(I32)
    run = run + jnp.sum(member.astype(F32), axis=1, keepdims=True)
    run_ref[...] = run
    cnt_ref[...] = jnp.broadcast_to(run, cnt_ref.shape).astype(I32)
    ri_ref[0] = jnp.concatenate([cls, rank, jnp.zeros((SUBLANES - 2, tm), I32)], axis=0)


def _out_proj(h_lru, zl, yb, yc, xc, mod, w_out, ln_g, ln_b, rw_split, rb, alpha, n_ctx_tiles):
    bsz, sk, _ = zl.shape
    d = xc.shape[1]
    tm = TOKEN_TILE
    nt = sk // tm

    def mod_map(b, i):
        return (jnp.where(i < n_ctx_tiles, bsz, b), 0, 0)

    return pl.pallas_call(
        functools.partial(_out_proj_kernel, alpha=alpha, n_ctx_tiles=n_ctx_tiles, n_batch=bsz),
        out_shape=(
            jax.ShapeDtypeStruct((bsz * sk, d + LANES), F32),
            jax.ShapeDtypeStruct((bsz, SUBLANES, sk), I32),
            jax.ShapeDtypeStruct((CLASS_ROWS, LANES), I32),
        ),
        grid=(bsz, nt),
        in_specs=[
            pl.BlockSpec((1, 1, tm, LRU_WIDTH), lambda b, i: (0, b, i, 0)),
            pl.BlockSpec((1, 1, tm, LRU_WIDTH), lambda b, i: (1, b, i, 0)),
            pl.BlockSpec((1, tm, LRU_WIDTH), lambda b, i: (b, i, 1)),
            pl.BlockSpec((1, tm, GQA_HEADS * GQA_HEAD_DIM), lambda b, i: (b, i, 0)),
            pl.BlockSpec((1, tm, DIFF_HEADS * DIFF_V_DIM), lambda b, i: (b, i, 0)),
            pl.BlockSpec((tm, d), lambda b, i: (b * nt + i, 0)),
            pl.BlockSpec((1, N_MOD, d), mod_map),
            pl.BlockSpec((d, d), lambda b, i: (0, 0)),
            pl.BlockSpec((1, d), lambda b, i: (0, 0)),
            pl.BlockSpec((1, d), lambda b, i: (0, 0)),
            pl.BlockSpec((2 * N_EXPERTS, d), lambda b, i: (0, 0)),
            pl.BlockSpec((N_EXPERTS, 1), lambda b, i: (0, 0)),
        ],
        out_specs=(
            pl.BlockSpec((tm, d + LANES), lambda b, i: (b * nt + i, 0)),
            pl.BlockSpec((1, SUBLANES, tm), lambda b, i: (b, 0, i)),
            pl.BlockSpec((CLASS_ROWS, LANES), lambda b, i: (0, 0)),
        ),
        scratch_shapes=[pltpu.VMEM((CLASS_ROWS, 1), F32)],
        compiler_params=_params(("arbitrary", "arbitrary")),
        name="out_proj_router",
    )(h_lru, h_lru, zl, yb, yc, xc, mod, w_out, ln_g, ln_b, rw_split, rb)


def _moe_kernel(cls_ref, rank_ref, pstart_ref, ea_ref, eb_ref, nv_ref, nu_ref,
                x_hbm, mod_ref, lng_ref, lnb_ref,
                wga_ref, wua_ref, wda_ref, wgb_ref, wub_ref, wdb_ref,
                o_hbm, tok_ref, xbuf, obuf, gsem, ssem, *, alpha, n_batch, n_tok):
    rows = xbuf.shape[1]
    d = obuf.shape[2]
    i = pl.program_id(0)
    n_steps = pl.num_programs(0)
    n_used = nu_ref[0]
    slot = i % 2

    @pl.when(i == 0)
    def _():
        def clear(r, c):
            tok_ref[r] = 0
            return c
        lax.fori_loop(0, tok_ref.shape[0], clear, 0, unroll=8)

        def place(t, c):
            tok_ref[pstart_ref[cls_ref[t]] + rank_ref[t]] = t
            return c
        lax.fori_loop(0, n_tok, place, 0, unroll=8)

        obuf[...] = jnp.zeros_like(obuf)

    def start_gather(step, slot_):
        for r in range(rows):
            pltpu.make_async_copy(x_hbm.at[pl.ds(tok_ref[step * rows + r], 1)], xbuf.at[slot_, pl.ds(r, 1)],
                                  gsem.at[slot_]).start()

    def wait_gather(slot_):
        pltpu.make_async_copy(x_hbm.at[pl.ds(0, rows)], xbuf.at[slot_], gsem.at[slot_]).wait()

    def start_scatter(step, n_valid, slot_):
        for r in range(rows):
            dst = jnp.where(r < n_valid, tok_ref[step * rows + r], n_tok + slot_ * rows + r)
            pltpu.make_async_copy(obuf.at[slot_, pl.ds(r, 1)], o_hbm.at[pl.ds(dst, 1)], ssem.at[slot_]).start()

    def wait_scatter(slot_):
        pltpu.make_async_copy(obuf.at[slot_], o_hbm.at[pl.ds(0, rows)], ssem.at[slot_]).wait()

    def block_step(slot):
        wait_gather(slot)
        start_gather(jnp.minimum(i + 1, n_used - 1), 1 - slot)
        prev = jnp.maximum(i - 1, 0)
        start_scatter(prev, jnp.where(i >= 1, nv_ref[prev], 0), 1 - slot)

        x = xbuf[slot, :, 0:d]
        gate_a = xbuf[slot, :, d:d + 1]
        gate_b = xbuf[slot, :, d + 1:d + 2]
        mid = xbuf[slot, :, d + 2:d + 3]

        def pick(kk):
            v = mod_ref[n_batch, kk:kk + 1, :]
            for b in range(n_batch):
                v = jnp.where(mid == float(b), mod_ref[b, kk:kk + 1, :], v)
            return v

        h = x * (1.0 + pick(4)) + pick(3)

        def ffn(wg, wu, wd):
            a = jnp.dot(h, wg[0], preferred_element_type=F32)
            u = jnp.dot(h, wu[0], preferred_element_type=F32)
            return jnp.dot((a * jax.nn.sigmoid(a)) * u, wd[0], preferred_element_type=F32)

        f = ffn(wga_ref, wua_ref, wda_ref) * gate_a + ffn(wgb_ref, wub_ref, wdb_ref) * gate_b
        y = _layer_norm(alpha * x + pick(5) * f, lng_ref[...], lnb_ref[...])
        wait_scatter(slot)
        obuf[slot] = y

    @pl.when(i == 0)
    def _():
        pltpu.make_async_copy(obuf.at[0], o_hbm.at[pl.ds(n_tok, rows)], ssem.at[0]).start()
        start_gather(0, 0)

    for parity in range(2):
        @pl.when(jnp.logical_and(i < n_used, i % 2 == parity))
        def _(parity=parity):
            block_step(parity)

    @pl.when(i == n_steps - 1)
    def _():
        last = n_used - 1
        q = last % 2
        start_scatter(last, nv_ref[last], q)
        wait_scatter(q)
        wait_scatter(1 - q)
        wait_gather(1 - q)


def _moe(x_wide, n_tok, cls, rank, pstart, blk_ea, blk_eb, blk_nv, n_used, mod, ln_g, ln_b,
         w_gate, w_up, w_down, layer, alpha, n_batch):
    d = x_wide.shape[1] - LANES
    rows = MOE_ROWS
    n_blk = blk_ea.shape[0]
    ff = w_gate.shape[-1]
    first = layer * N_EXPERTS

    def wa(i, cl, rk, ps, ea, eb, nv, nu):
        return (first + ea[i], 0, 0)

    def wb(i, cl, rk, ps, ea, eb, nv, nu):
        return (first + eb[i], 0, 0)

    const2 = lambda i, cl, rk, ps, ea, eb, nv, nu: (0, 0)
    grid_spec = pltpu.PrefetchScalarGridSpec(
        num_scalar_prefetch=7,
        grid=(n_blk,),
        in_specs=[
            pl.BlockSpec(memory_space=pl.ANY),
            pl.BlockSpec(mod.shape, lambda i, cl, rk, ps, ea, eb, nv, nu: (0, 0, 0)),
            pl.BlockSpec((1, d), const2),
            pl.BlockSpec((1, d), const2),
            pl.BlockSpec((1, d, ff), wa),
            pl.BlockSpec((1, d, ff), wa),
            pl.BlockSpec((1, ff, d), wa),
            pl.BlockSpec((1, d, ff), wb),
            pl.BlockSpec((1, d, ff), wb),
            pl.BlockSpec((1, ff, d), wb),
        ],
        out_specs=pl.BlockSpec(memory_space=pl.ANY),
        scratch_shapes=[
            pltpu.SMEM((n_blk * rows,), I32),
            pltpu.VMEM((2, rows, d + LANES), F32),
            pltpu.VMEM((2, rows, d), F32),
            pltpu.SemaphoreType.DMA((2,)),
            pltpu.SemaphoreType.DMA((2,)),
        ],
    )
    return pl.pallas_call(
        functools.partial(_moe_kernel, alpha=alpha, n_batch=n_batch, n_tok=n_tok),
        out_shape=jax.ShapeDtypeStruct((n_tok + 2 * rows, d), F32),
        grid_spec=grid_spec,
        compiler_params=_params(("arbitrary",)),
        name="moe_experts",
    )(cls, rank, pstart, blk_ea, blk_eb, blk_nv, n_used, x_wide, mod, ln_g, ln_b,
      w_gate, w_up, w_down, w_gate, w_up, w_down)


def _block_tables(counts, n_tok):
    rows = MOE_ROWS
    n_blk = n_tok // rows + N_CLASSES
    n_pairs = len(PAIRS_A)
    padded = (counts + rows - 1) // rows * rows
    pend = jnp.cumsum(padded)
    pstart = pend - padded
    n_used = pend[-1] // rows
    blk = jnp.arange(n_blk, dtype=I32)
    blk_ids = jnp.minimum(blk, jnp.maximum(n_used - 1, 0))
    blk_cls = jnp.minimum(jnp.sum((blk_ids[:, None] * rows >= pend[None, :]).astype(I32), axis=1), N_CLASSES - 1)
    onehot = blk_cls[:, None] == jnp.arange(N_CLASSES, dtype=I32)[None, :]
    within = blk_ids * rows - jnp.sum(jnp.where(onehot, pstart[None, :], 0), axis=1)
    count_b = jnp.sum(jnp.where(onehot, counts[None, :], 0), axis=1)
    blk_nv = jnp.where(blk < n_used, jnp.clip(count_b - within, 0, rows), 0).astype(I32)
    pair = blk_cls % n_pairs
    slot_a = sum(jnp.where(pair == p, PAIRS_A[p], 0) for p in range(n_pairs))
    slot_b = sum(jnp.where(pair == p, PAIRS_B[p], 0) for p in range(n_pairs))
    first = blk_cls // n_pairs * EXPERTS_PER_GROUP
    pstart_pad = jnp.concatenate([pstart, jnp.zeros((CLASS_ROWS - N_CLASSES,), I32)]).astype(I32)
    return (pstart_pad, (first + slot_a).astype(I32), (first + slot_b).astype(I32), blk_nv,
            n_used.reshape(1).astype(I32))


def _rope_tables(n_lat, n_ctx, dim):
    rows = n_lat // GRID_W
    row = jnp.repeat(jnp.arange(rows, dtype=F32), GRID_W)
    col = jnp.tile(jnp.arange(GRID_W, dtype=F32), rows)
    n_freq = dim // 4
    inv = ROPE_THETA ** (-jnp.arange(n_freq, dtype=F32) / n_freq)
    ang = jnp.concatenate([row[:, None] * inv, col[:, None] * inv], axis=-1)
    cos = jnp.repeat(jnp.cos(ang), 2, axis=-1)
    sin = jnp.repeat(jnp.sin(ang), 2, axis=-1)
    even = (jnp.arange(dim) % 2 == 0)
    tabs = jnp.stack([cos, jnp.where(even, -sin, 0.0), jnp.where(even, 0.0, sin)])
    ident = jnp.stack([jnp.ones((n_ctx, dim), F32), jnp.zeros((n_ctx, dim), F32), jnp.zeros((n_ctx, dim), F32)])
    return jnp.tile(jnp.concatenate([ident, tabs], axis=1), (1, 1, LANES // dim))


def _block_diag(w):
    n, c, _ = w.shape
    eye = jnp.eye(n, dtype=w.dtype)
    return (eye[:, None, :, None] * w[:, :, None, :]).reshape(n * c, n * c)


def kernel(x, c, ctx, c_ctx, ada_w, ada_b, w_in, conv_w, conv_b, rg_wa, rg_ba, rg_wx, rg_bx, rg_lam, q_norm_g, k_norm_g, diff_lambda, diff_subln_g, w_out, ln1_g, ln1_b, ln2_g, ln2_b, router_w, router_b, exp_w_gate, exp_w_up, exp_w_down):
    bsz, s, d = x.shape
    n_ctx = ctx.shape[1]
    depth = w_in.shape[0]
    sk = n_ctx + s
    assert d == D_MODEL and n_ctx == TOKEN_TILE and s % TOKEN_TILE == 0 and bsz + 1 <= SUBLANES
    n_ctx_tiles = n_ctx // TOKEN_TILE
    n_tok = bsz * sk
    alpha = (2.0 * depth) ** 0.25

    c_rows = jnp.concatenate([c, c_ctx[None, :], jnp.zeros((SUBLANES - bsz - 1, d), F32)], axis=0)
    mods = _ada_modulation(c_rows, ada_w, ada_b).reshape(depth, SUBLANES, N_MOD, d)

    rope_g = _rope_tables(s, n_ctx, GQA_HEAD_DIM)
    rope_d = _rope_tables(s, n_ctx, DIFF_QK_DIM)
    lane = np.arange(LANES)
    seg = jnp.asarray(lane[:, None] // GQA_HEAD_DIM == lane[None, :] // GQA_HEAD_DIM, BF16)
    rw_t = router_w.T
    rw_hi = rw_t.astype(BF16)
    rw_split = jnp.concatenate([rw_hi, (rw_t - rw_hi.astype(F32)).astype(BF16)], axis=0)
    rb = router_b.reshape(N_EXPERTS, 1)
    zero_aux = jnp.zeros((SUBLANES, LANES), F32)

    xc = jnp.concatenate([jnp.concatenate([ctx, x], axis=1).reshape(n_tok, d),
                          jnp.zeros((2 * MOE_ROWS, d), F32)], axis=0)
    w_gate_all = exp_w_gate.reshape(depth * N_EXPERTS, d, EXPERT_FF)
    w_up_all = exp_w_up.reshape(depth * N_EXPERTS, d, EXPERT_FF)
    w_down_all = exp_w_down.reshape(depth * N_EXPERTS, EXPERT_FF, d)
    for li in range(depth):
        lam_init = 0.8 - 0.6 * math.exp(-0.3 * li)
        mod = mods[li]
        gains = jnp.concatenate([jnp.tile(jnp.tile(q_norm_g[li], 2)[None, :], (4, 1)),
                                 jnp.tile(k_norm_g[li], 2)[None, :], jnp.zeros((3, LANES), F32)], axis=0)
        zl, q, k, vt, dq, dk, dvt = _in_proj(xc, bsz, sk, mod, w_in[li].astype(BF16), gains, seg, rope_g, rope_d,
                                             n_ctx_tiles)

        w_gates = jnp.stack([jnp.concatenate([_block_diag(rg_wa[li, dd]), _block_diag(rg_wx[li, dd])], axis=1)
                             for dd in range(2)]).astype(BF16)
        b_gates = jnp.concatenate([rg_ba[li], rg_bx[li]], axis=-1)[:, None, :]
        h_lru = _lru(zl, conv_w[li], conv_b[li][None, :], w_gates, b_gates, rg_lam[li][:, None, :])

        yb = _attention(q, k, vt, zero_aux, diff=False, n_ctx_tiles=n_ctx_tiles)
        aux = jnp.concatenate([
            jnp.pad(diff_lambda[li], ((0, 0), (0, LANES - DIFF_QK_DIM))),
            jnp.full((1, LANES), lam_init, F32),
            jnp.tile(diff_subln_g[li], 2)[None, :],
            jnp.zeros((2, LANES), F32)], axis=0)
        yc = _attention(dq, dk, dvt, aux, diff=True, n_ctx_tiles=n_ctx_tiles)

        x1, route_i, counts = _out_proj(h_lru, zl, yb, yc, xc, mod, w_out[li].astype(BF16), ln1_g[li][None, :],
                                        ln1_b[li][None, :], rw_split, rb, alpha, n_ctx_tiles)

        pstart, blk_ea, blk_eb, blk_nv, n_used = _block_tables(counts[:N_CLASSES, 0], n_tok)
        xc = _moe(x1, n_tok, route_i[:, 0, :].reshape(n_tok), route_i[:, 1, :].reshape(n_tok), pstart,
                  blk_ea, blk_eb, blk_nv, n_used, mod, ln2_g[li][None, :], ln2_b[li][None, :],
                  w_gate_all, w_up_all, w_down_all, li, alpha, bsz)
    return xc[:n_tok].reshape(bsz, sk, d)[:, n_ctx:, :]
```

```python
import functools
import math

import numpy as np
import jax
import jax.numpy as jnp
from jax import lax
from jax.experimental import pallas as pl
from jax.experimental.pallas import tpu as pltpu

F32 = jnp.float32
BF16 = jnp.bfloat16
I32 = jnp.int32

D_MODEL = 1024
GRID_W = 64
LRU_WIDTH = 256
LRU_BLOCKS = 4
CONV_W = 4
RG_C = 8.0
GQA_HEADS = 8
GQA_KV_HEADS = 2
GQA_HEAD_DIM = 64
DIFF_HEADS = 4
DIFF_QK_DIM = 32
DIFF_V_DIM = 64
IN_WIDTH = 2048
N_EXPERTS = 16
N_GROUPS = 4
EXPERTS_PER_GROUP = 4
EXPERT_FF = 512
ROPE_THETA = 10000.0
EPS = 1e-6
N_MOD = 6
LOG2_E = math.log2(math.e)

COL_AX, COL_AG, COL_GQ, COL_GK, COL_GV, COL_DQ, COL_DK, COL_DV = 0, 256, 512, 1024, 1152, 1280, 1536, 1792

LANES = 128
SUBLANES = 8
VMEM_LIMIT = 56 * 1024 * 1024

TOKEN_TILE = 256
VT_ROWS = 80
PAIRS_A = (0, 0, 0, 1, 1, 3)
PAIRS_B = (1, 2, 3, 3, 2, 2)
N_CLASSES = N_GROUPS * len(PAIRS_A)
CLASS_ROWS = 32
MOE_ROWS = 128
MAX_KEY_TILES_PER_TRIP = 32
MAX_SHIFT_LAG = 60.0
MAX_SHIFT_RAISE = 120.0

NT_DIMS = (((1,), (1,)), ((), ()))


def _params(semantics):
    return pltpu.CompilerParams(dimension_semantics=semantics, vmem_limit_bytes=VMEM_LIMIT)


def _ada_kernel(c_ref, w_ref, b_ref, o_ref):
    c = c_ref[...]
    o_ref[0] = jnp.dot(c * jax.nn.sigmoid(c), w_ref[0], preferred_element_type=F32) + b_ref[0]


def _ada_modulation(c_rows, ada_w, ada_b):
    depth, d, n = ada_w.shape
    tn = 1536
    return pl.pallas_call(
        _ada_kernel,
        out_shape=jax.ShapeDtypeStruct((depth, SUBLANES, n), F32),
        grid=(depth, n // tn),
        in_specs=[
            pl.BlockSpec((SUBLANES, d), lambda l, j: (0, 0)),
            pl.BlockSpec((1, d, tn), lambda l, j: (l, 0, j)),
            pl.BlockSpec((1, 1, tn), lambda l, j: (l, 0, j)),
        ],
        out_specs=pl.BlockSpec((1, SUBLANES, tn), lambda l, j: (l, 0, j)),
        compiler_params=_params(("arbitrary", "arbitrary")),
        name="ada_modulation",
    )(c_rows, ada_w, ada_b.reshape(depth, 1, n))


def _rotate_pairs(v, cos, sin_next, sin_prev):
    return v * cos + pltpu.roll(v, LANES - 1, 1) * sin_next + pltpu.roll(v, 1, 1) * sin_prev


def _in_proj_kernel(x_ref, mod_ref, w_ref, gain_ref, seg_ref, rope_g_ref, rope_d_ref,
                    zl_ref, q_ref, k_ref, vt_ref, dq_ref, dk_ref, dvt_ref):
    tm = x_ref.shape[0]
    x = x_ref[...]
    shift = mod_ref[0, 0:1, :]
    scale = mod_ref[0, 1:2, :]
    h = (x * (1.0 + scale) + shift).astype(BF16)
    z = jnp.dot(h, w_ref[...], preferred_element_type=F32)
    zl_ref[0] = z[:, COL_AX:COL_GQ]

    ones_rows = jnp.where(lax.broadcasted_iota(I32, (VT_ROWS - GQA_HEAD_DIM, tm), 0) == 0, 1.0, 0.0).astype(BF16)

    seg = seg_ref[...]
    cos, s_next, s_prev = rope_g_ref[0], rope_g_ref[1], rope_g_ref[2]
    for g in range(5):
        v = z[:, COL_GQ + g * LANES:COL_GQ + (g + 1) * LANES]
        sq = v * v
        sq_hi = sq.astype(BF16)
        sq_lo = (sq - sq_hi.astype(F32)).astype(BF16)
        ssq = jnp.dot(sq_hi, seg, preferred_element_type=F32) + jnp.dot(sq_lo, seg, preferred_element_type=F32)
        v = v * lax.rsqrt(ssq * (1.0 / GQA_HEAD_DIM) + EPS) * gain_ref[g:g + 1, :]
        v = _rotate_pairs(v, cos, s_next, s_prev)
        if g < 4:
            v = (v * (GQA_HEAD_DIM ** -0.5 * LOG2_E)).astype(BF16)
            q_ref[0, 2 * g] = v[:, :GQA_HEAD_DIM]
            q_ref[0, 2 * g + 1] = v[:, GQA_HEAD_DIM:]
        else:
            v = v.astype(BF16)
            k_ref[0, 0] = v[:, :GQA_HEAD_DIM]
            k_ref[0, 1] = v[:, GQA_HEAD_DIM:]
    vt = z[:, COL_GV:COL_DQ].T.astype(BF16)
    for hd in range(GQA_KV_HEADS):
        vt_ref[0, hd, 0, 0:GQA_HEAD_DIM, :] = vt[hd * GQA_HEAD_DIM:(hd + 1) * GQA_HEAD_DIM]
        vt_ref[0, hd, 0, GQA_HEAD_DIM:VT_ROWS, :] = ones_rows

    cos, s_next, s_prev = rope_d_ref[0], rope_d_ref[1], rope_d_ref[2]
    for g in range(4):
        v = _rotate_pairs(z[:, COL_DQ + g * LANES:COL_DQ + (g + 1) * LANES], cos, s_next, s_prev)
        if g < 2:
            v = v * (DIFF_QK_DIM ** -0.5 * LOG2_E)
        v = v.astype(BF16)
        dst = dq_ref if g < 2 else dk_ref
        for j in range(4):
            dst[0, (g % 2) * 4 + j] = v[:, j * DIFF_QK_DIM:(j + 1) * DIFF_QK_DIM]
    dvt = z[:, COL_DV:IN_WIDTH].T.astype(BF16)
    for hd in range(DIFF_HEADS):
        dvt_ref[0, hd, 0, 0:DIFF_V_DIM, :] = dvt[hd * DIFF_V_DIM:(hd + 1) * DIFF_V_DIM]
        dvt_ref[0, hd, 0, DIFF_V_DIM:VT_ROWS, :] = ones_rows


def _in_proj(xc, bsz, sk, mod, w_in, gains, seg, rope_g, rope_d, n_ctx_tiles):
    d = xc.shape[1]
    tm = TOKEN_TILE
    nt = sk // tm
    n_mod_rows = mod.shape[0]

    def mod_map(b, i):
        return (jnp.where(i < n_ctx_tiles, bsz, b), 0, 0)

    assert n_mod_rows > bsz
    return pl.pallas_call(
        _in_proj_kernel,
        out_shape=(
            jax.ShapeDtypeStruct((bsz, sk, COL_GQ), F32),
            jax.ShapeDtypeStruct((bsz, GQA_HEADS, sk, GQA_HEAD_DIM), BF16),
            jax.ShapeDtypeStruct((bsz, GQA_KV_HEADS, sk, GQA_HEAD_DIM), BF16),
            jax.ShapeDtypeStruct((bsz, GQA_KV_HEADS, nt, VT_ROWS, tm), BF16),
            jax.ShapeDtypeStruct((bsz, 2 * DIFF_HEADS, sk, DIFF_QK_DIM), BF16),
            jax.ShapeDtypeStruct((bsz, 2 * DIFF_HEADS, sk, DIFF_QK_DIM), BF16),
            jax.ShapeDtypeStruct((bsz, DIFF_HEADS, nt, VT_ROWS, tm), BF16),
        ),
        grid=(bsz, nt),
        in_specs=[
            pl.BlockSpec((tm, d), lambda b, i: (b * nt + i, 0)),
            pl.BlockSpec((1, N_MOD, d), mod_map),
            pl.BlockSpec((d, IN_WIDTH), lambda b, i: (0, 0)),
            pl.BlockSpec((SUBLANES, LANES), lambda b, i: (0, 0)),
            pl.BlockSpec((LANES, LANES), lambda b, i: (0, 0)),
            pl.BlockSpec((3, tm, LANES), lambda b, i: (0, i, 0)),
            pl.BlockSpec((3, tm, LANES), lambda b, i: (0, i, 0)),
        ],
        out_specs=(
            pl.BlockSpec((1, tm, COL_GQ), lambda b, i: (b, i, 0)),
            pl.BlockSpec((1, GQA_HEADS, tm, GQA_HEAD_DIM), lambda b, i: (b, 0, i, 0)),
            pl.BlockSpec((1, GQA_KV_HEADS, tm, GQA_HEAD_DIM), lambda b, i: (b, 0, i, 0)),
            pl.BlockSpec((1, GQA_KV_HEADS, 1, VT_ROWS, tm), lambda b, i: (b, 0, i, 0, 0)),
            pl.BlockSpec((1, 2 * DIFF_HEADS, tm, DIFF_QK_DIM), lambda b, i: (b, 0, i, 0)),
            pl.BlockSpec((1, 2 * DIFF_HEADS, tm, DIFF_QK_DIM), lambda b, i: (b, 0, i, 0)),
            pl.BlockSpec((1, DIFF_HEADS, 1, VT_ROWS, tm), lambda b, i: (b, 0, i, 0, 0)),
        ),
        compiler_params=_params(("arbitrary", "arbitrary")),
        name="in_proj",
    )(xc, mod, w_in, gains, seg, rope_g, rope_d)


def _expm1(x):
    u = jnp.exp(x)
    return jnp.where(u == 1.0, x, (u - 1.0) * x / jnp.log(u))


def _lru_kernel(ax_ref, prev_ref, next_ref, cw_ref, cb_ref, w_ref, b_ref, lam_ref, h_ref, carry_ref, *, nt):
    tm = ax_ref.shape[1]
    d = pl.program_id(1)
    j = pl.program_id(2)
    blk = jnp.where(d == 0, j, jnp.where(j == 0, 0, nt - j))

    @pl.when(j == 0)
    def _():
        carry_ref[...] = jnp.zeros_like(carry_ref)

    x = ax_ref[0]
    no_left = jnp.logical_or(blk == 0, blk == 1)
    no_right = jnp.logical_or(blk == 0, blk == nt - 1)
    left = jnp.where(no_left, 0.0, prev_ref[0, SUBLANES - 1:SUBLANES, :])
    right0 = jnp.where(no_right, 0.0, next_ref[0, 0:1, :])
    right1 = jnp.where(no_right, 0.0, next_ref[0, 1:2, :])
    row = lax.broadcasted_iota(I32, (tm, 1), 0)
    x_m1 = jnp.where(row == 0, left, pltpu.roll(x, 1, 0))
    x_p1 = jnp.where(row == tm - 1, right0, pltpu.roll(x, tm - 1, 0))
    x_p2 = jnp.where(row == tm - 2, right0, jnp.where(row == tm - 1, right1, pltpu.roll(x, tm - 2, 0)))
    u = cb_ref[...] + x_m1 * cw_ref[0:1, :] + x * cw_ref[1:2, :] + x_p1 * cw_ref[2:3, :] + x_p2 * cw_ref[3:4, :]

    g = jnp.dot(u.astype(BF16), w_ref[0], preferred_element_type=F32) + b_ref[0]
    r = jax.nn.sigmoid(g[:, :LRU_WIDTH])
    gate_in = jax.nn.sigmoid(g[:, LRU_WIDTH:])
    neg_lam = -lam_ref[0]
    softplus = jnp.maximum(neg_lam, 0.0) + jnp.log1p(jnp.exp(-jnp.abs(neg_lam)))
    log_a = (-RG_C * r) * softplus
    a = jnp.exp(log_a)
    b = jnp.sqrt(-_expm1(2.0 * log_a)) * (gate_in * u)

    def scan(reverse):
        aa, bb = a, b
        in_group = row % SUBLANES
        s = 1
        while s < SUBLANES:
            shift = tm - s if reverse else s
            keep = (in_group < SUBLANES - s) if reverse else (in_group >= s)
            a_sh = jnp.where(keep, pltpu.roll(aa, shift, 0), 1.0)
            b_sh = jnp.where(keep, pltpu.roll(bb, shift, 0), 0.0)
            bb = aa * b_sh + bb
            aa = aa * a_sh
            s *= 2
        state = carry_ref[...]
        n_groups = tm // SUBLANES
        for g in (range(n_groups - 1, -1, -1) if reverse else range(n_groups)):
            lo = g * SUBLANES
            h = aa[lo:lo + SUBLANES] * state + bb[lo:lo + SUBLANES]
            h_ref[0, 0, lo:lo + SUBLANES, :] = h
            state = h[0:1, :] if reverse else h[SUBLANES - 1:SUBLANES, :]
        carry_ref[...] = state

    @pl.when(d == 0)
    def _():
        scan(False)

    @pl.when(d == 1)
    def _():
        scan(True)


def _lru(zl, conv_w, conv_b, w_gates, b_gates, lam):
    bsz, sk, _ = zl.shape
    tm = TOKEN_TILE
    nt = sk // tm
    per = tm // SUBLANES

    def blk_of(d, j):
        return jnp.where(d == 0, j, jnp.where(j == 0, 0, nt - j))

    return pl.pallas_call(
        functools.partial(_lru_kernel, nt=nt),
        out_shape=jax.ShapeDtypeStruct((2, bsz, sk, LRU_WIDTH), F32),
        grid=(bsz, 2, nt),
        in_specs=[
            pl.BlockSpec((1, tm, LRU_WIDTH), lambda b, d, j: (b, blk_of(d, j), 0)),
            pl.BlockSpec((1, SUBLANES, LRU_WIDTH), lambda b, d, j: (b, jnp.maximum(blk_of(d, j) * per - 1, 0), 0)),
            pl.BlockSpec((1, SUBLANES, LRU_WIDTH),
                         lambda b, d, j: (b, jnp.minimum((blk_of(d, j) + 1) * per, nt * per - 1), 0)),
            pl.BlockSpec((CONV_W, LRU_WIDTH), lambda b, d, j: (0, 0)),
            pl.BlockSpec((1, LRU_WIDTH), lambda b, d, j: (0, 0)),
            pl.BlockSpec((1, LRU_WIDTH, 2 * LRU_WIDTH), lambda b, d, j: (d, 0, 0)),
            pl.BlockSpec((1, 1, 2 * LRU_WIDTH), lambda b, d, j: (d, 0, 0)),
            pl.BlockSpec((1, 1, LRU_WIDTH), lambda b, d, j: (d, 0, 0)),
        ],
        out_specs=pl.BlockSpec((1, 1, tm, LRU_WIDTH), lambda b, d, j: (d, b, blk_of(d, j), 0)),
        scratch_shapes=[pltpu.VMEM((1, LRU_WIDTH), F32)],
        compiler_params=_params(("arbitrary", "arbitrary", "arbitrary")),
        name="rg_lru",
    )(zl, zl, zl, conv_w, conv_b, w_gates, b_gates, lam)


def _attn_kernel(q_ref, k_ref, vt_ref, aux_ref, o_ref, s_ref, m_ref, acc_ref, *, k_heads, v_heads, nt, n_ctx_tiles, per_trip,
                 diff):
    tq = q_ref.shape[2]
    tk = tq
    dv = GQA_HEAD_DIM
    qi = pl.program_id(2)
    n_k = jnp.where(qi < n_ctx_tiles, n_ctx_tiles, nt)

    def scores(kt):
        start = pl.multiple_of(kt * tk, tk)
        return jnp.concatenate(
            [lax.dot_general(k_ref[0, k_heads[j], pl.ds(start, tk), :], q_ref[0, j], NT_DIMS,
                             preferred_element_type=F32) for j in range(4)], axis=1)

    def weighted_values(kt, p):
        return jnp.concatenate(
            [jnp.dot(vt_ref[0, v_heads[j], kt], p[:, j * tq:(j + 1) * tq], preferred_element_type=F32)
             for j in range(4)], axis=1)

    def col_max(s, m):
        return jnp.maximum(m, jnp.max(s, axis=0, keepdims=True))

    s_first = scores(0)
    s_ref[0] = s_first
    m_ref[...] = jnp.max(s_first, axis=0, keepdims=True)
    acc_ref[0] = jnp.zeros(acc_ref.shape[1:], F32)

    def trip(carry):
        i, cur = carry
        kt = per_trip * i
        m0 = m_ref[...]
        s_cur = s_ref[cur]
        peak = jnp.zeros((1, 4 * tq), BF16)
        acc = acc_ref[cur]
        for u in range(per_trip):
            s_next = scores(kt + u + 1)
            p = jnp.exp2(s_cur - m0).astype(BF16)
            peak = jnp.maximum(peak, jnp.max(p, axis=0, keepdims=True))
            acc = acc + weighted_values(kt + u, p)
            s_cur = s_next
        s_ref[1 - cur] = s_cur
        acc_ref[1 - cur] = acc
        peak = peak.astype(F32)
        lagging = jnp.max(peak) > 2.0 ** MAX_SHIFT_LAG

        @pl.when(lagging)
        def _():
            m_new = m0 + jnp.clip(jnp.log(peak) * LOG2_E, 0.0, MAX_SHIFT_RAISE)
            acc_ref[cur] = acc_ref[cur] * jnp.exp2(m0 - m_new)
            m_ref[...] = m_new

        return jnp.where(lagging, i, i + 1), jnp.where(lagging, cur, 1 - cur)

    n_trips = (n_k - 1) // per_trip
    _, cur = lax.while_loop(lambda c: c[0] < n_trips, trip, (jnp.int32(0), jnp.int32(0)))

    s0 = s_ref[cur]
    m0 = m_ref[...]
    m1 = col_max(s0, m0)
    p0 = jnp.exp2(s0 - m1).astype(BF16)
    acc_all = acc_ref[cur] * jnp.exp2(m0 - m1) + weighted_values(n_k - 1, p0)

    outs = []
    for j in range(4):
        acc = acc_all[:, j * tq:(j + 1) * tq]
        outs.append(acc[0:dv] / acc[dv:dv + 1])
    if diff:
        dl = aux_ref[0:4, :]
        lam_init = aux_ref[4:5, 0:1]
        lam = (jnp.exp(jnp.sum(dl[0:1] * dl[1:2], axis=1, keepdims=True))
               - jnp.exp(jnp.sum(dl[2:3] * dl[3:4], axis=1, keepdims=True)) + lam_init)
        heads = []
        for hd in range(2):
            o = outs[2 * hd] - lam * outs[2 * hd + 1]
            heads.append(o * lax.rsqrt(jnp.mean(o * o, axis=0, keepdims=True) + EPS))
        o_ref[0] = ((jnp.concatenate(heads, axis=0).T * aux_ref[5:6, :]) * (1.0 - lam_init)).astype(o_ref.dtype)
    else:
        for pair in range(2):
            o_ref[0, :, pair * 2 * dv:(pair + 1) * 2 * dv] = (
                jnp.concatenate(outs[2 * pair:2 * pair + 2], axis=0).T.astype(o_ref.dtype))


def _attention(q, k, vt, aux, *, diff, n_ctx_tiles):
    bsz, n_q, sk, dh = q.shape
    n_k_heads = k.shape[1]
    n_v_heads = vt.shape[1]
    tq = TOKEN_TILE
    nt = sk // tq
    assert n_ctx_tiles == 1
    per_trip = math.gcd(nt - 1, MAX_KEY_TILES_PER_TRIP)
    groups = n_q // 4
    kpg = n_k_heads // groups
    vpg = n_v_heads // groups
    k_heads = tuple(j * kpg // 4 for j in range(4))
    v_heads = tuple(j * vpg // 4 for j in range(4))
    width = 2 * DIFF_V_DIM if diff else 4 * GQA_HEAD_DIM
    return pl.pallas_call(
        functools.partial(_attn_kernel, k_heads=k_heads, v_heads=v_heads, nt=nt, n_ctx_tiles=n_ctx_tiles,
                          per_trip=per_trip, diff=diff),
        out_shape=jax.ShapeDtypeStruct((bsz, sk, groups * width), BF16),
        grid=(bsz, groups, nt),
        in_specs=[
            pl.BlockSpec((1, 4, tq, dh), lambda b, g, i: (b, g, i, 0)),
            pl.BlockSpec((1, kpg, sk, dh), lambda b, g, i: (b, g, 0, 0)),
            pl.BlockSpec((1, vpg, nt, VT_ROWS, tq), lambda b, g, i: (b, g, 0, 0, 0)),
            pl.BlockSpec((SUBLANES, LANES), lambda b, g, i: (0, 0)),
        ],
        out_specs=pl.BlockSpec((1, tq, width), lambda b, g, i: (b, i, g)),
        scratch_shapes=[pltpu.VMEM((2, tq, 4 * tq), F32), pltpu.VMEM((1, 4 * tq), F32),
                        pltpu.VMEM((2, VT_ROWS, 4 * tq), F32)],
        compiler_params=_params(("arbitrary", "arbitrary", "arbitrary")),
        name="diff_attention" if diff else "gqa_attention",
    )(q, k, vt, aux)


def _layer_norm(r, g, b):
    mu = jnp.mean(r, axis=-1, keepdims=True)
    c = r - mu
    var = jnp.mean(c * c, axis=-1, keepdims=True)
    return c * lax.rsqrt(var + EPS) * g + b


def _route(logits, bias):
    mx = jnp.max(logits, axis=0, keepdims=True)
    e = jnp.exp(logits - mx)
    probs = e / jnp.sum(e, axis=0, keepdims=True)
    sel = probs + bias
    epg = EXPERTS_PER_GROUP
    in_top2, scores = [], []
    for g in range(N_GROUPS):
        v = [sel[g * epg + i:g * epg + i + 1] for i in range(epg)]
        masks = []
        for i in range(epg):
            rank = jnp.zeros_like(v[i])
            for j in range(epg):
                if j != i:
                    beats = (v[j] > v[i]) if j > i else (v[j] >= v[i])
                    rank = rank + beats.astype(F32)
            masks.append(rank < 2.0)
        in_top2.append(masks)
        scores.append(sum(jnp.where(masks[i], v[i], 0.0) for i in range(epg)))
    chosen = []
    for g in range(N_GROUPS):
        c = None
        for j in range(N_GROUPS):
            if j != g:
                t = (scores[j] < scores[g]) if j < g else (scores[j] <= scores[g])
                c = t if c is None else jnp.logical_and(c, t)
        chosen.append(c)
    picked, weight = [], []
    for i in range(epg):
        m = None
        w = jnp.zeros_like(scores[0])
        for g in range(N_GROUPS):
            t = jnp.logical_and(chosen[g], in_top2[g][i])
            m = t if m is None else jnp.logical_or(m, t)
            w = w + jnp.where(t, probs[g * epg + i:g * epg + i + 1], 0.0)
        picked.append(m)
        weight.append(w)
    total = weight[0] + weight[1] + weight[2] + weight[3]
    gate = [w / total for w in weight]
    lo = jnp.where(picked[0], 0.0, jnp.where(picked[1], 1.0, 2.0))
    hi = jnp.where(picked[3], 3.0, jnp.where(picked[2], 2.0, 1.0))
    gate_lo = jnp.where(picked[0], gate[0], jnp.where(picked[1], gate[1], gate[2]))
    gate_hi = jnp.where(picked[3], gate[3], jnp.where(picked[2], gate[2], gate[1]))
    group = sum(jnp.where(chosen[g], float(g), 0.0) for g in range(N_GROUPS))
    base = jnp.where(lo == 0.0, 0.0, jnp.where(lo == 1.0, 3.0, 5.0))
    natural = base + hi - lo - 1.0
    pair = jnp.where(natural == 3.0, 4.0, jnp.where(natural == 4.0, 3.0, natural))
    swapped = natural == 5.0
    cls = group * float(len(PAIRS_A)) + pair
    return cls.astype(I32), jnp.where(swapped, gate_hi, gate_lo), jnp.where(swapped, gate_lo, gate_hi)


def _out_proj_kernel(hf_ref, hb_ref, ag_ref, yb_ref, yc_ref, x_ref, mod_ref, w_ref, lng_ref, lnb_ref,
                     rw_ref, rb_ref, x1_ref, ri_ref, cnt_ref, run_ref, *, alpha, n_ctx_tiles, n_batch):
    tm = x_ref.shape[0]
    ya = (jax.nn.gelu(ag_ref[0]) * (hf_ref[0, 0] + hb_ref[0, 0])).astype(BF16)
    n_a = LRU_WIDTH
    n_b = n_a + GQA_HEADS * GQA_HEAD_DIM
    y = (jnp.dot(ya, w_ref[0:n_a, :], preferred_element_type=F32)
         + jnp.dot(yb_ref[0], w_ref[n_a:n_b, :], preferred_element_type=F32)
         + jnp.dot(yc_ref[0], w_ref[n_b:, :], preferred_element_type=F32))
    x1 = _layer_norm(alpha * x_ref[...] + mod_ref[0, 2:3, :] * y, lng_ref[...], lnb_ref[...])
    d = x1.shape[1]
    x1_ref[:, 0:d] = x1

    h2 = x1 * (1.0 + mod_ref[0, 4:5, :]) + mod_ref[0, 3:4, :]
    h_hi = h2.astype(BF16)
    h_lo = (h2 - h_hi.astype(F32)).astype(BF16)
    both = lax.dot_general(rw_ref[...], h_hi, NT_DIMS, preferred_element_type=F32)
    logits = (both[0:N_EXPERTS] + both[N_EXPERTS:]
              + lax.dot_general(rw_ref[0:N_EXPERTS, :], h_lo, NT_DIMS, preferred_element_type=F32))
    cls, gate_lo, gate_hi = _route(logits, rb_ref[...])

    b = pl.program_id(0)
    i = pl.program_id(1)
    mod_row = jnp.where(i < n_ctx_tiles, n_batch, b).astype(F32)
    extra = jnp.concatenate([gate_lo, gate_hi, jnp.full((1, tm), mod_row, F32), jnp.zeros((LANES - 3, tm), F32)], axis=0)
    x1_ref[:, d:] = extra.T

    @pl.when(jnp.logical_and(b == 0, i == 0))
    def _():
        run_ref[...] = jnp.zeros_like(run_ref)

    member = lax.broadcasted_iota(I32, (CLASS_ROWS, tm), 0) == cls
    before = lax.broadcasted_iota(I32, (tm, tm), 0) < lax.broadcasted_iota(I32, (tm, tm), 1)
    earlier = jnp.dot(member.astype(BF16), before.astype(BF16), preferred_element_type=F32)
    run = run_ref[...]
    rank = jnp.sum(jnp.where(member, earlier + run, 0.0), axis=0, keepdims=True).astype(I32)
    run = run + jnp.sum(member.astype(F32), axis=1, keepdims=True)
    run_ref[...] = run
    cnt_ref[...] = jnp.broadcast_to(run, cnt_ref.shape).astype(I32)
    ri_ref[0] = jnp.concatenate([cls, rank, jnp.zeros((SUBLANES - 2, tm), I32)], axis=0)


def _out_proj(h_lru, zl, yb, yc, xc, mod, w_out, ln_g, ln_b, rw_split, rb, alpha, n_ctx_tiles):
    bsz, sk, _ = zl.shape
    d = xc.shape[1]
    tm = TOKEN_TILE
    nt = sk // tm

    def mod_map(b, i):
        return (jnp.where(i < n_ctx_tiles, bsz, b), 0, 0)

    return pl.pallas_call(
        functools.partial(_out_proj_kernel, alpha=alpha, n_ctx_tiles=n_ctx_tiles, n_batch=bsz),
        out_shape=(
            jax.ShapeDtypeStruct((bsz * sk, d + LANES), F32),
            jax.ShapeDtypeStruct((bsz, SUBLANES, sk), I32),
            jax.ShapeDtypeStruct((CLASS_ROWS, LANES), I32),
        ),
        grid=(bsz, nt),
        in_specs=[
            pl.BlockSpec((1, 1, tm, LRU_WIDTH), lambda b, i: (0, b, i, 0)),
            pl.BlockSpec((1, 1, tm, LRU_WIDTH), lambda b, i: (1, b, i, 0)),
            pl.BlockSpec((1, tm, LRU_WIDTH), lambda b, i: (b, i, 1)),
            pl.BlockSpec((1, tm, GQA_HEADS * GQA_HEAD_DIM), lambda b, i: (b, i, 0)),
            pl.BlockSpec((1, tm, DIFF_HEADS * DIFF_V_DIM), lambda b, i: (b, i, 0)),
            pl.BlockSpec((tm, d), lambda b, i: (b * nt + i, 0)),
            pl.BlockSpec((1, N_MOD, d), mod_map),
            pl.BlockSpec((d, d), lambda b, i: (0, 0)),
            pl.BlockSpec((1, d), lambda b, i: (0, 0)),
            pl.BlockSpec((1, d), lambda b, i: (0, 0)),
            pl.BlockSpec((2 * N_EXPERTS, d), lambda b, i: (0, 0)),
            pl.BlockSpec((N_EXPERTS, 1), lambda b, i: (0, 0)),
        ],
        out_specs=(
            pl.BlockSpec((tm, d + LANES), lambda b, i: (b * nt + i, 0)),
            pl.BlockSpec((1, SUBLANES, tm), lambda b, i: (b, 0, i)),
            pl.BlockSpec((CLASS_ROWS, LANES), lambda b, i: (0, 0)),
        ),
        scratch_shapes=[pltpu.VMEM((CLASS_ROWS, 1), F32)],
        compiler_params=_params(("arbitrary", "arbitrary")),
        name="out_proj_router",
    )(h_lru, h_lru, zl, yb, yc, xc, mod, w_out, ln_g, ln_b, rw_split, rb)


def _moe_kernel(cls_ref, rank_ref, pstart_ref, ea_ref, eb_ref, nv_ref, nu_ref,
                x_hbm, mod_ref, lng_ref, lnb_ref,
                wga_ref, wua_ref, wda_ref, wgb_ref, wub_ref, wdb_ref,
                o_hbm, tok_ref, xbuf, obuf, gsem, ssem, *, alpha, n_batch, n_tok):
    rows = xbuf.shape[1]
    d = obuf.shape[2]
    i = pl.program_id(0)
    n_steps = pl.num_programs(0)
    n_used = nu_ref[0]
    slot = i % 2

    @pl.when(i == 0)
    def _():
        def clear(r, c):
            tok_ref[r] = 0
            return c
        lax.fori_loop(0, tok_ref.shape[0], clear, 0, unroll=8)

        def place(t, c):
            tok_ref[pstart_ref[cls_ref[t]] + rank_ref[t]] = t
            return c
        lax.fori_loop(0, n_tok, place, 0, unroll=8)

        obuf[...] = jnp.zeros_like(obuf)

    def start_gather(step, slot_):
        for r in range(rows):
            pltpu.make_async_copy(x_hbm.at[pl.ds(tok_ref[step * rows + r], 1)], xbuf.at[slot_, pl.ds(r, 1)],
                                  gsem.at[slot_]).start()

    def wait_gather(slot_):
        pltpu.make_async_copy(x_hbm.at[pl.ds(0, rows)], xbuf.at[slot_], gsem.at[slot_]).wait()

    def start_scatter(step, n_valid, slot_):
        for r in range(rows):
            dst = jnp.where(r < n_valid, tok_ref[step * rows + r], n_tok + slot_ * rows + r)
            pltpu.make_async_copy(obuf.at[slot_, pl.ds(r, 1)], o_hbm.at[pl.ds(dst, 1)], ssem.at[slot_]).start()

    def wait_scatter(slot_):
        pltpu.make_async_copy(obuf.at[slot_], o_hbm.at[pl.ds(0, rows)], ssem.at[slot_]).wait()

    def block_step(slot):
        wait_gather(slot)
        start_gather(jnp.minimum(i + 1, n_used - 1), 1 - slot)
        prev = jnp.maximum(i - 1, 0)
        start_scatter(prev, jnp.where(i >= 1, nv_ref[prev], 0), 1 - slot)

        x = xbuf[slot, :, 0:d]
        gate_a = xbuf[slot, :, d:d + 1]
        gate_b = xbuf[slot, :, d + 1:d + 2]
        mid = xbuf[slot, :, d + 2:d + 3]

        def pick(kk):
            v = mod_ref[n_batch, kk:kk + 1, :]
            for b in range(n_batch):
                v = jnp.where(mid == float(b), mod_ref[b, kk:kk + 1, :], v)
            return v

        h = x * (1.0 + pick(4)) + pick(3)

        def ffn(wg, wu, wd):
            a = jnp.dot(h, wg[0], preferred_element_type=F32)
            u = jnp.dot(h, wu[0], preferred_element_type=F32)
            return jnp.dot((a * jax.nn.sigmoid(a)) * u, wd[0], preferred_element_type=F32)

        f = ffn(wga_ref, wua_ref, wda_ref) * gate_a + ffn(wgb_ref, wub_ref, wdb_ref) * gate_b
        y = _layer_norm(alpha * x + pick(5) * f, lng_ref[...], lnb_ref[...])
        wait_scatter(slot)
        obuf[slot] = y

    @pl.when(i == 0)
    def _():
        pltpu.make_async_copy(obuf.at[0], o_hbm.at[pl.ds(n_tok, rows)], ssem.at[0]).start()
        start_gather(0, 0)

    for parity in range(2):
        @pl.when(jnp.logical_and(i < n_used, i % 2 == parity))
        def _(parity=parity):
            block_step(parity)

    @pl.when(i == n_steps - 1)
    def _():
        last = n_used - 1
        q = last % 2
        start_scatter(last, nv_ref[last], q)
        wait_scatter(q)
        wait_scatter(1 - q)
        wait_gather(1 - q)


def _moe(x_wide, n_tok, cls, rank, pstart, blk_ea, blk_eb, blk_nv, n_used, mod, ln_g, ln_b,
         w_gate, w_up, w_down, layer, alpha, n_batch):
    d = x_wide.shape[1] - LANES
    rows = MOE_ROWS
    n_blk = blk_ea.shape[0]
    ff = w_gate.shape[-1]
    first = layer * N_EXPERTS

    def wa(i, cl, rk, ps, ea, eb, nv, nu):
        return (first + ea[i], 0, 0)

    def wb(i, cl, rk, ps, ea, eb, nv, nu):
        return (first + eb[i], 0, 0)

    const2 = lambda i, cl, rk, ps, ea, eb, nv, nu: (0, 0)
    grid_spec = pltpu.PrefetchScalarGridSpec(
        num_scalar_prefetch=7,
        grid=(n_blk,),
        in_specs=[
            pl.BlockSpec(memory_space=pl.ANY),
            pl.BlockSpec(mod.shape, lambda i, cl, rk, ps, ea, eb, nv, nu: (0, 0, 0)),
            pl.BlockSpec((1, d), const2),
            pl.BlockSpec((1, d), const2),
            pl.BlockSpec((1, d, ff), wa),
            pl.BlockSpec((1, d, ff), wa),
            pl.BlockSpec((1, ff, d), wa),
            pl.BlockSpec((1, d, ff), wb),
            pl.BlockSpec((1, d, ff), wb),
            pl.BlockSpec((1, ff, d), wb),
        ],
        out_specs=pl.BlockSpec(memory_space=pl.ANY),
        scratch_shapes=[
            pltpu.SMEM((n_blk * rows,), I32),
            pltpu.VMEM((2, rows, d + LANES), F32),
            pltpu.VMEM((2, rows, d), F32),
            pltpu.SemaphoreType.DMA((2,)),
            pltpu.SemaphoreType.DMA((2,)),
        ],
    )
    return pl.pallas_call(
        functools.partial(_moe_kernel, alpha=alpha, n_batch=n_batch, n_tok=n_tok),
        out_shape=jax.ShapeDtypeStruct((n_tok + 2 * rows, d), F32),
        grid_spec=grid_spec,
        compiler_params=_params(("arbitrary",)),
        name="moe_experts",
    )(cls, rank, pstart, blk_ea, blk_eb, blk_nv, n_used, x_wide, mod, ln_g, ln_b,
      w_gate, w_up, w_down, w_gate, w_up, w_down)


def _block_tables(counts, n_tok):
    rows = MOE_ROWS
    n_blk = n_tok // rows + N_CLASSES
    n_pairs = len(PAIRS_A)
    padded = (counts + rows - 1) // rows * rows
    pend = jnp.cumsum(padded)
    pstart = pend - padded
    n_used = pend[-1] // rows
    blk = jnp.arange(n_blk, dtype=I32)
    blk_ids = jnp.minimum(blk, jnp.maximum(n_used - 1, 0))
    blk_cls = jnp.minimum(jnp.sum((blk_ids[:, None] * rows >= pend[None, :]).astype(I32), axis=1), N_CLASSES - 1)
    onehot = blk_cls[:, None] == jnp.arange(N_CLASSES, dtype=I32)[None, :]
    within = blk_ids * rows - jnp.sum(jnp.where(onehot, pstart[None, :], 0), axis=1)
    count_b = jnp.sum(jnp.where(onehot, counts[None, :], 0), axis=1)
    blk_nv = jnp.where(blk < n_used, jnp.clip(count_b - within, 0, rows), 0).astype(I32)
    pair = blk_cls % n_pairs
    slot_a = sum(jnp.where(pair == p, PAIRS_A[p], 0) for p in range(n_pairs))
    slot_b = sum(jnp.where(pair == p, PAIRS_B[p], 0) for p in range(n_pairs))
    first = blk_cls // n_pairs * EXPERTS_PER_GROUP
    pstart_pad = jnp.concatenate([pstart, jnp.zeros((CLASS_ROWS - N_CLASSES,), I32)]).astype(I32)
    return (pstart_pad, (first + slot_a).astype(I32), (first + slot_b).astype(I32), blk_nv,
            n_used.reshape(1).astype(I32))


def _rope_tables(n_lat, n_ctx, dim):
    rows = n_lat // GRID_W
    row = jnp.repeat(jnp.arange(rows, dtype=F32), GRID_W)
    col = jnp.tile(jnp.arange(GRID_W, dtype=F32), rows)
    n_freq = dim // 4
    inv = ROPE_THETA ** (-jnp.arange(n_freq, dtype=F32) / n_freq)
    ang = jnp.concatenate([row[:, None] * inv, col[:, None] * inv], axis=-1)
    cos = jnp.repeat(jnp.cos(ang), 2, axis=-1)
    sin = jnp.repeat(jnp.sin(ang), 2, axis=-1)
    even = (jnp.arange(dim) % 2 == 0)
    tabs = jnp.stack([cos, jnp.where(even, -sin, 0.0), jnp.where(even, 0.0, sin)])
    ident = jnp.stack([jnp.ones((n_ctx, dim), F32), jnp.zeros((n_ctx, dim), F32), jnp.zeros((n_ctx, dim), F32)])
    return jnp.tile(jnp.concatenate([ident, tabs], axis=1), (1, 1, LANES // dim))


def _block_diag(w):
    n, c, _ = w.shape
    eye = jnp.eye(n, dtype=w.dtype)
    return (eye[:, None, :, None] * w[:, :, None, :]).reshape(n * c, n * c)


def kernel(x, c, ctx, c_ctx, ada_w, ada_b, w_in, conv_w, conv_b, rg_wa, rg_ba, rg_wx, rg_bx, rg_lam, q_norm_g, k_norm_g, diff_lambda, diff_subln_g, w_out, ln1_g, ln1_b, ln2_g, ln2_b, router_w, router_b, exp_w_gate, exp_w_up, exp_w_down):
    bsz, s, d = x.shape
    n_ctx = ctx.shape[1]
    depth = w_in.shape[0]
    sk = n_ctx + s
    assert d == D_MODEL and n_ctx == TOKEN_TILE and s % TOKEN_TILE == 0 and bsz + 1 <= SUBLANES
    n_ctx_tiles = n_ctx // TOKEN_TILE
    n_tok = bsz * sk
    alpha = (2.0 * depth) ** 0.25

    c_rows = jnp.concatenate([c, c_ctx[None, :], jnp.zeros((SUBLANES - bsz - 1, d), F32)], axis=0)
    mods = _ada_modulation(c_rows, ada_w, ada_b).reshape(depth, SUBLANES, N_MOD, d)

    rope_g = _rope_tables(s, n_ctx, GQA_HEAD_DIM)
    rope_d = _rope_tables(s, n_ctx, DIFF_QK_DIM)
    lane = np.arange(LANES)
    seg = jnp.asarray(lane[:, None] // GQA_HEAD_DIM == lane[None, :] // GQA_HEAD_DIM, BF16)
    rw_t = router_w.T
    rw_hi = rw_t.astype(BF16)
    rw_split = jnp.concatenate([rw_hi, (rw_t - rw_hi.astype(F32)).astype(BF16)], axis=0)
    rb = router_b.reshape(N_EXPERTS, 1)
    zero_aux = jnp.zeros((SUBLANES, LANES), F32)

    xc = jnp.concatenate([jnp.concatenate([ctx, x], axis=1).reshape(n_tok, d),
                          jnp.zeros((2 * MOE_ROWS, d), F32)], axis=0)
    w_gate_all = exp_w_gate.reshape(depth * N_EXPERTS, d, EXPERT_FF)
    w_up_all = exp_w_up.reshape(depth * N_EXPERTS, d, EXPERT_FF)
    w_down_all = exp_w_down.reshape(depth * N_EXPERTS, EXPERT_FF, d)
    for li in range(depth):
        lam_init = 0.8 - 0.6 * math.exp(-0.3 * li)
        mod = mods[li]
        gains = jnp.concatenate([jnp.tile(jnp.tile(q_norm_g[li], 2)[None, :], (4, 1)),
                                 jnp.tile(k_norm_g[li], 2)[None, :], jnp.zeros((3, LANES), F32)], axis=0)
        zl, q, k, vt, dq, dk, dvt = _in_proj(xc, bsz, sk, mod, w_in[li].astype(BF16), gains, seg, rope_g, rope_d,
                                             n_ctx_tiles)

        w_gates = jnp.stack([jnp.concatenate([_block_diag(rg_wa[li, dd]), _block_diag(rg_wx[li, dd])], axis=1)
                             for dd in range(2)]).astype(BF16)
        b_gates = jnp.concatenate([rg_ba[li], rg_bx[li]], axis=-1)[:, None, :]
        h_lru = _lru(zl, conv_w[li], conv_b[li][None, :], w_gates, b_gates, rg_lam[li][:, None, :])

        yb = _attention(q, k, vt, zero_aux, diff=False, n_ctx_tiles=n_ctx_tiles)
        aux = jnp.concatenate([
            jnp.pad(diff_lambda[li], ((0, 0), (0, LANES - DIFF_QK_DIM))),
            jnp.full((1, LANES), lam_init, F32),
            jnp.tile(diff_subln_g[li], 2)[None, :],
            jnp.zeros((2, LANES), F32)], axis=0)
        yc = _attention(dq, dk, dvt, aux, diff=True, n_ctx_tiles=n_ctx_tiles)

        x1, route_i, counts = _out_proj(h_lru, zl, yb, yc, xc, mod, w_out[li].astype(BF16), ln1_g[li][None, :],
                                        ln1_b[li][None, :], rw_split, rb, alpha, n_ctx_tiles)

        pstart, blk_ea, blk_eb, blk_nv, n_used = _block_tables(counts[:N_CLASSES, 0], n_tok)
        xc = _moe(x1, n_tok, route_i[:, 0, :].reshape(n_tok), route_i[:, 1, :].reshape(n_tok), pstart,
                  blk_ea, blk_eb, blk_nv, n_used, mod, ln2_g[li][None, :], ln2_b[li][None, :],
                  w_gate_all, w_up_all, w_down_all, li, alpha, bsz)
    return xc[:n_tok].reshape(bsz, sk, d)[:, n_ctx:, :]
```

```python
import functools
import math

import numpy as np
import jax
import jax.numpy as jnp
from jax import lax
from jax.experimental import pallas as pl
from jax.experimental.pallas import tpu as pltpu

F32 = jnp.float32
BF16 = jnp.bfloat16
I32 = jnp.int32

D_MODEL = 1024
GRID_W = 64
LRU_WIDTH = 256
LRU_BLOCKS = 4
CONV_W = 4
RG_C = 8.0
GQA_HEADS = 8
GQA_KV_HEADS = 2
GQA_HEAD_DIM = 64
DIFF_HEADS = 4
DIFF_QK_DIM = 32
DIFF_V_DIM = 64
IN_WIDTH = 2048
N_EXPERTS = 16
N_GROUPS = 4
EXPERTS_PER_GROUP = 4
EXPERT_FF = 512
ROPE_THETA = 10000.0
EPS = 1e-6
N_MOD = 6
LOG2_E = math.log2(math.e)

COL_AX, COL_AG, COL_GQ, COL_GK, COL_GV, COL_DQ, COL_DK, COL_DV = 0, 256, 512, 1024, 1152, 1280, 1536, 1792

LANES = 128
SUBLANES = 8
VMEM_LIMIT = 56 * 1024 * 1024

TOKEN_TILE = 256
VT_ROWS = 80
PAIRS_A = (0, 0, 0, 1, 1, 3)
PAIRS_B = (1, 2, 3, 3, 2, 2)
N_CLASSES = N_GROUPS * len(PAIRS_A)
CLASS_ROWS = 32
MOE_ROWS = 128
MAX_SHIFT_LAG = 60.0
MAX_SHIFT_RAISE = 120.0

NT_DIMS = (((1,), (1,)), ((), ()))


def _params(semantics):
    return pltpu.CompilerParams(dimension_semantics=semantics, vmem_limit_bytes=VMEM_LIMIT)


def _ada_kernel(c_ref, w_ref, b_ref, o_ref):
    c = c_ref[...]
    o_ref[0] = jnp.dot(c * jax.nn.sigmoid(c), w_ref[0], preferred_element_type=F32) + b_ref[0]


def _ada_modulation(c_rows, ada_w, ada_b):
    depth, d, n = ada_w.shape
    tn = 1536
    return pl.pallas_call(
        _ada_kernel,
        out_shape=jax.ShapeDtypeStruct((depth, SUBLANES, n), F32),
        grid=(depth, n // tn),
        in_specs=[
            pl.BlockSpec((SUBLANES, d), lambda l, j: (0, 0)),
            pl.BlockSpec((1, d, tn), lambda l, j: (l, 0, j)),
            pl.BlockSpec((1, 1, tn), lambda l, j: (l, 0, j)),
        ],
        out_specs=pl.BlockSpec((1, SUBLANES, tn), lambda l, j: (l, 0, j)),
        compiler_params=_params(("arbitrary", "arbitrary")),
        name="ada_modulation",
    )(c_rows, ada_w, ada_b.reshape(depth, 1, n))


def _rotate_pairs(v, cos, sin_next, sin_prev):
    return v * cos + pltpu.roll(v, LANES - 1, 1) * sin_next + pltpu.roll(v, 1, 1) * sin_prev


def _in_proj_kernel(x_ref, mod_ref, w_ref, gain_ref, seg_ref, rope_g_ref, rope_d_ref,
                    zl_ref, q_ref, k_ref, vt_ref, dq_ref, dk_ref, dvt_ref):
    tm = x_ref.shape[0]
    x = x_ref[...]
    shift = mod_ref[0, 0:1, :]
    scale = mod_ref[0, 1:2, :]
    h = (x * (1.0 + scale) + shift).astype(BF16)
    z = jnp.dot(h, w_ref[...], preferred_element_type=F32)
    zl_ref[0] = z[:, COL_AX:COL_GQ]

    ones_rows = jnp.where(lax.broadcasted_iota(I32, (VT_ROWS - GQA_HEAD_DIM, tm), 0) == 0, 1.0, 0.0).astype(BF16)

    seg = seg_ref[...]
    cos, s_next, s_prev = rope_g_ref[0], rope_g_ref[1], rope_g_ref[2]
    for g in range(5):
        v = z[:, COL_GQ + g * LANES:COL_GQ + (g + 1) * LANES]
        sq = v * v
        sq_hi = sq.astype(BF16)
        sq_lo = (sq - sq_hi.astype(F32)).astype(BF16)
        ssq = jnp.dot(sq_hi, seg, preferred_element_type=F32) + jnp.dot(sq_lo, seg, preferred_element_type=F32)
        v = v * lax.rsqrt(ssq * (1.0 / GQA_HEAD_DIM) + EPS) * gain_ref[g:g + 1, :]
        v = _rotate_pairs(v, cos, s_next, s_prev)
        if g < 4:
            v = (v * (GQA_HEAD_DIM ** -0.5 * LOG2_E)).astype(BF16)
            q_ref[0, 2 * g] = v[:, :GQA_HEAD_DIM]
            q_ref[0, 2 * g + 1] = v[:, GQA_HEAD_DIM:]
        else:
            v = v.astype(BF16)
            k_ref[0, 0] = v[:, :GQA_HEAD_DIM]
            k_ref[0, 1] = v[:, GQA_HEAD_DIM:]
    vt = z[:, COL_GV:COL_DQ].T.astype(BF16)
    for hd in range(GQA_KV_HEADS):
        vt_ref[0, hd, 0, 0:GQA_HEAD_DIM, :] = vt[hd * GQA_HEAD_DIM:(hd + 1) * GQA_HEAD_DIM]
        vt_ref[0, hd, 0, GQA_HEAD_DIM:VT_ROWS, :] = ones_rows

    cos, s_next, s_prev = rope_d_ref[0], rope_d_ref[1], rope_d_ref[2]
    for g in range(4):
        v = _rotate_pairs(z[:, COL_DQ + g * LANES:COL_DQ + (g + 1) * LANES], cos, s_next, s_prev)
        if g < 2:
            v = v * (DIFF_QK_DIM ** -0.5 * LOG2_E)
        v = v.astype(BF16)
        dst = dq_ref if g < 2 else dk_ref
        for j in range(4):
            dst[0, (g % 2) * 4 + j] = v[:, j * DIFF_QK_DIM:(j + 1) * DIFF_QK_DIM]
    dvt = z[:, COL_DV:IN_WIDTH].T.astype(BF16)
    for hd in range(DIFF_HEADS):
        dvt_ref[0, hd, 0, 0:DIFF_V_DIM, :] = dvt[hd * DIFF_V_DIM:(hd + 1) * DIFF_V_DIM]
        dvt_ref[0, hd, 0, DIFF_V_DIM:VT_ROWS, :] = ones_rows


def _in_proj(xc, bsz, sk, mod, w_in, gains, seg, rope_g, rope_d, n_ctx_tiles):
    d = xc.shape[1]
    tm = TOKEN_TILE
    nt = sk // tm
    n_mod_rows = mod.shape[0]

    def mod_map(b, i):
        return (jnp.where(i < n_ctx_tiles, bsz, b), 0, 0)

    assert n_mod_rows > bsz
    return pl.pallas_call(
        _in_proj_kernel,
        out_shape=(
            jax.ShapeDtypeStruct((bsz, sk, COL_GQ), F32),
            jax.ShapeDtypeStruct((bsz, GQA_HEADS, sk, GQA_HEAD_DIM), BF16),
            jax.ShapeDtypeStruct((bsz, GQA_KV_HEADS, sk, GQA_HEAD_DIM), BF16),
            jax.ShapeDtypeStruct((bsz, GQA_KV_HEADS, nt, VT_ROWS, tm), BF16),
            jax.ShapeDtypeStruct((bsz, 2 * DIFF_HEADS, sk, DIFF_QK_DIM), BF16),
            jax.ShapeDtypeStruct((bsz, 2 * DIFF_HEADS, sk, DIFF_QK_DIM), BF16),
            jax.ShapeDtypeStruct((bsz, DIFF_HEADS, nt, VT_ROWS, tm), BF16),
        ),
        grid=(bsz, nt),
        in_specs=[
            pl.BlockSpec((tm, d), lambda b, i: (b * nt + i, 0)),
            pl.BlockSpec((1, N_MOD, d), mod_map),
            pl.BlockSpec((d, IN_WIDTH), lambda b, i: (0, 0)),
            pl.BlockSpec((SUBLANES, LANES), lambda b, i: (0, 0)),
            pl.BlockSpec((LANES, LANES), lambda b, i: (0, 0)),
            pl.BlockSpec((3, tm, LANES), lambda b, i: (0, i, 0)),
            pl.BlockSpec((3, tm, LANES), lambda b, i: (0, i, 0)),
        ],
        out_specs=(
            pl.BlockSpec((1, tm, COL_GQ), lambda b, i: (b, i, 0)),
            pl.BlockSpec((1, GQA_HEADS, tm, GQA_HEAD_DIM), lambda b, i: (b, 0, i, 0)),
            pl.BlockSpec((1, GQA_KV_HEADS, tm, GQA_HEAD_DIM), lambda b, i: (b, 0, i, 0)),
            pl.BlockSpec((1, GQA_KV_HEADS, 1, VT_ROWS, tm), lambda b, i: (b, 0, i, 0, 0)),
            pl.BlockSpec((1, 2 * DIFF_HEADS, tm, DIFF_QK_DIM), lambda b, i: (b, 0, i, 0)),
            pl.BlockSpec((1, 2 * DIFF_HEADS, tm, DIFF_QK_DIM), lambda b, i: (b, 0, i, 0)),
            pl.BlockSpec((1, DIFF_HEADS, 1, VT_ROWS, tm), lambda b, i: (b, 0, i, 0, 0)),
        ),
        compiler_params=_params(("arbitrary", "arbitrary")),
        name="in_proj",
    )(xc, mod, w_in, gains, seg, rope_g, rope_d)


def _expm1(x):
    u = jnp.exp(x)
    return jnp.where(u == 1.0, x, (u - 1.0) * x / jnp.log(u))


def _lru_kernel(ax_ref, prev_ref, next_ref, cw_ref, cb_ref, w_ref, b_ref, lam_ref, h_ref, carry_ref, *, nt):
    tm = ax_ref.shape[1]
    d = pl.program_id(1)
    j = pl.program_id(2)
    blk = jnp.where(d == 0, j, jnp.where(j == 0, 0, nt - j))

    @pl.when(j == 0)
    def _():
        carry_ref[...] = jnp.zeros_like(carry_ref)

    x = ax_ref[0]
    no_left = jnp.logical_or(blk == 0, blk == 1)
    no_right = jnp.logical_or(blk == 0, blk == nt - 1)
    left = jnp.where(no_left, 0.0, prev_ref[0, SUBLANES - 1:SUBLANES, :])
    right0 = jnp.where(no_right, 0.0, next_ref[0, 0:1, :])
    right1 = jnp.where(no_right, 0.0, next_ref[0, 1:2, :])
    row = lax.broadcasted_iota(I32, (tm, 1), 0)
    x_m1 = jnp.where(row == 0, left, pltpu.roll(x, 1, 0))
    x_p1 = jnp.where(row == tm - 1, right0, pltpu.roll(x, tm - 1, 0))
    x_p2 = jnp.where(row == tm - 2, right0, jnp.where(row == tm - 1, right1, pltpu.roll(x, tm - 2, 0)))
    u = cb_ref[...] + x_m1 * cw_ref[0:1, :] + x * cw_ref[1:2, :] + x_p1 * cw_ref[2:3, :] + x_p2 * cw_ref[3:4, :]

    g = jnp.dot(u.astype(BF16), w_ref[0], preferred_element_type=F32) + b_ref[0]
    r = jax.nn.sigmoid(g[:, :LRU_WIDTH])
    gate_in = jax.nn.sigmoid(g[:, LRU_WIDTH:])
    neg_lam = -lam_ref[0]
    softplus = jnp.maximum(neg_lam, 0.0) + jnp.log1p(jnp.exp(-jnp.abs(neg_lam)))
    log_a = (-RG_C * r) * softplus
    a = jnp.exp(log_a)
    b = jnp.sqrt(-_expm1(2.0 * log_a)) * (gate_in * u)

    def scan(reverse):
        aa, bb = a, b
        in_group = row % SUBLANES
        s = 1
        while s < SUBLANES:
            shift = tm - s if reverse else s
            keep = (in_group < SUBLANES - s) if reverse else (in_group >= s)
            a_sh = jnp.where(keep, pltpu.roll(aa, shift, 0), 1.0)
            b_sh = jnp.where(keep, pltpu.roll(bb, shift, 0), 0.0)
            bb = aa * b_sh + bb
            aa = aa * a_sh
            s *= 2
        state = carry_ref[...]
        n_groups = tm // SUBLANES
        for g in (range(n_groups - 1, -1, -1) if reverse else range(n_groups)):
            lo = g * SUBLANES
            h = aa[lo:lo + SUBLANES] * state + bb[lo:lo + SUBLANES]
            h_ref[0, 0, lo:lo + SUBLANES, :] = h
            state = h[0:1, :] if reverse else h[SUBLANES - 1:SUBLANES, :]
        carry_ref[...] = state

    @pl.when(d == 0)
    def _():
        scan(False)

    @pl.when(d == 1)
    def _():
        scan(True)


def _lru(zl, conv_w, conv_b, w_gates, b_gates, lam):
    bsz, sk, _ = zl.shape
    tm = TOKEN_TILE
    nt = sk // tm
    per = tm // SUBLANES

    def blk_of(d, j):
        return jnp.where(d == 0, j, jnp.where(j == 0, 0, nt - j))

    return pl.pallas_call(
        functools.partial(_lru_kernel, nt=nt),
        out_shape=jax.ShapeDtypeStruct((2, bsz, sk, LRU_WIDTH), F32),
        grid=(bsz, 2, nt),
        in_specs=[
            pl.BlockSpec((1, tm, LRU_WIDTH), lambda b, d, j: (b, blk_of(d, j), 0)),
            pl.BlockSpec((1, SUBLANES, LRU_WIDTH), lambda b, d, j: (b, jnp.maximum(blk_of(d, j) * per - 1, 0), 0)),
            pl.BlockSpec((1, SUBLANES, LRU_WIDTH),
                         lambda b, d, j: (b, jnp.minimum((blk_of(d, j) + 1) * per, nt * per - 1), 0)),
            pl.BlockSpec((CONV_W, LRU_WIDTH), lambda b, d, j: (0, 0)),
            pl.BlockSpec((1, LRU_WIDTH), lambda b, d, j: (0, 0)),
            pl.BlockSpec((1, LRU_WIDTH, 2 * LRU_WIDTH), lambda b, d, j: (d, 0, 0)),
            pl.BlockSpec((1, 1, 2 * LRU_WIDTH), lambda b, d, j: (d, 0, 0)),
            pl.BlockSpec((1, 1, LRU_WIDTH), lambda b, d, j: (d, 0, 0)),
        ],
        out_specs=pl.BlockSpec((1, 1, tm, LRU_WIDTH), lambda b, d, j: (d, b, blk_of(d, j), 0)),
        scratch_shapes=[pltpu.VMEM((1, LRU_WIDTH), F32)],
        compiler_params=_params(("arbitrary", "arbitrary", "arbitrary")),
        name="rg_lru",
    )(zl, zl, zl, conv_w, conv_b, w_gates, b_gates, lam)


def _attn_kernel(q_ref, k_ref, vt_ref, aux_ref, o_ref, s_ref, m_ref, acc_ref, *, k_heads, v_heads, nt, n_ctx_tiles,
                 diff):
    tq = q_ref.shape[2]
    tk = tq
    dv = GQA_HEAD_DIM
    qi = pl.program_id(2)

    def scores(kt):
        return jnp.concatenate(
            [lax.dot_general(k_ref[0, k_heads[j], kt * tk:(kt + 1) * tk, :], q_ref[0, j], NT_DIMS,
                             preferred_element_type=F32) for j in range(4)], axis=1)

    def weighted_values(kt, p):
        return jnp.concatenate(
            [jnp.dot(vt_ref[0, v_heads[j], kt], p[:, j * tq:(j + 1) * tq], preferred_element_type=F32)
             for j in range(4)], axis=1)

    def finalize(acc_all):
        outs = []
        for j in range(4):
            acc = acc_all[:, j * tq:(j + 1) * tq]
            outs.append(acc[0:dv] / acc[dv:dv + 1])
        if diff:
            dl = aux_ref[0:4, :]
            lam_init = aux_ref[4:5, 0:1]
            lam = (jnp.exp(jnp.sum(dl[0:1] * dl[1:2], axis=1, keepdims=True))
                   - jnp.exp(jnp.sum(dl[2:3] * dl[3:4], axis=1, keepdims=True)) + lam_init)
            heads = []
            for hd in range(2):
                o = outs[2 * hd] - lam * outs[2 * hd + 1]
                heads.append(o * lax.rsqrt(jnp.mean(o * o, axis=0, keepdims=True) + EPS))
            o_ref[0] = ((jnp.concatenate(heads, axis=0).T * aux_ref[5:6, :]) * (1.0 - lam_init)).astype(o_ref.dtype)
        else:
            for pair in range(2):
                o_ref[0, :, pair * 2 * dv:(pair + 1) * 2 * dv] = (
                    jnp.concatenate(outs[2 * pair:2 * pair + 2], axis=0).T.astype(o_ref.dtype))

    s_first = scores(0)
    m_first = jnp.max(s_first, axis=0, keepdims=True)

    @pl.when(qi < n_ctx_tiles)
    def _():
        finalize(weighted_values(0, jnp.exp2(s_first - m_first).astype(BF16)))

    @pl.when(qi >= n_ctx_tiles)
    def _():
        s_ref[...] = s_first
        m_ref[...] = m_first

        def attempt(again):
            m0 = m_ref[...]
            s_cur = s_ref[...]
            peak = jnp.zeros((1, 4 * tq), BF16)
            acc = None
            for u in range(nt):
                s_next = scores(u + 1) if u + 1 < nt else None
                p = jnp.exp2(s_cur - m0).astype(BF16)
                peak = jnp.maximum(peak, jnp.max(p, axis=0, keepdims=True))
                pv = weighted_values(u, p)
                acc = pv if acc is None else acc + pv
                s_cur = s_next
            acc_ref[...] = acc
            peak = peak.astype(F32)
            lagging = jnp.max(peak) > 2.0 ** MAX_SHIFT_LAG

            @pl.when(lagging)
            def _():
                m_ref[...] = m0 + jnp.clip(jnp.log(peak) * LOG2_E, 0.0, MAX_SHIFT_RAISE)

            return lagging

        lax.while_loop(lambda again: again, attempt, True)
        finalize(acc_ref[...])


def _attention(q, k, vt, aux, *, diff, n_ctx_tiles):
    bsz, n_q, sk, dh = q.shape
    n_k_heads = k.shape[1]
    n_v_heads = vt.shape[1]
    tq = TOKEN_TILE
    nt = sk // tq
    assert n_ctx_tiles == 1
    groups = n_q // 4
    kpg = n_k_heads // groups
    vpg = n_v_heads // groups
    k_heads = tuple(j * kpg // 4 for j in range(4))
    v_heads = tuple(j * vpg // 4 for j in range(4))
    width = 2 * DIFF_V_DIM if diff else 4 * GQA_HEAD_DIM
    return pl.pallas_call(
        functools.partial(_attn_kernel, k_heads=k_heads, v_heads=v_heads, nt=nt, n_ctx_tiles=n_ctx_tiles, diff=diff),
        out_shape=jax.ShapeDtypeStruct((bsz, sk, groups * width), BF16),
        grid=(bsz, groups, nt),
        in_specs=[
            pl.BlockSpec((1, 4, tq, dh), lambda b, g, i: (b, g, i, 0)),
            pl.BlockSpec((1, kpg, sk, dh), lambda b, g, i: (b, g, 0, 0)),
            pl.BlockSpec((1, vpg, nt, VT_ROWS, tq), lambda b, g, i: (b, g, 0, 0, 0)),
            pl.BlockSpec((SUBLANES, LANES), lambda b, g, i: (0, 0)),
        ],
        out_specs=pl.BlockSpec((1, tq, width), lambda b, g, i: (b, i, g)),
        scratch_shapes=[pltpu.VMEM((tq, 4 * tq), F32), pltpu.VMEM((1, 4 * tq), F32),
                        pltpu.VMEM((VT_ROWS, 4 * tq), F32)],
        compiler_params=_params(("arbitrary", "arbitrary", "arbitrary")),
        name="diff_attention" if diff else "gqa_attention",
    )(q, k, vt, aux)


def _layer_norm(r, g, b):
    mu = jnp.mean(r, axis=-1, keepdims=True)
    c = r - mu
    var = jnp.mean(c * c, axis=-1, keepdims=True)
    return c * lax.rsqrt(var + EPS) * g + b


def _route(logits, bias):
    mx = jnp.max(logits, axis=0, keepdims=True)
    e = jnp.exp(logits - mx)
    probs = e / jnp.sum(e, axis=0, keepdims=True)
    sel = probs + bias
    epg = EXPERTS_PER_GROUP
    in_top2, scores = [], []
    for g in range(N_GROUPS):
        v = [sel[g * epg + i:g * epg + i + 1] for i in range(epg)]
        masks = []
        for i in range(epg):
            rank = jnp.zeros_like(v[i])
            for j in range(epg):
                if j != i:
                    beats = (v[j] > v[i]) if j > i else (v[j] >= v[i])
                    rank = rank + beats.astype(F32)
            masks.append(rank < 2.0)
        in_top2.append(masks)
        scores.append(sum(jnp.where(masks[i], v[i], 0.0) for i in range(epg)))
    chosen = []
    for g in range(N_GROUPS):
        c = None
        for j in range(N_GROUPS):
            if j != g:
                t = (scores[j] < scores[g]) if j < g else (scores[j] <= scores[g])
                c = t if c is None else jnp.logical_and(c, t)
        chosen.append(c)
    picked, weight = [], []
    for i in range(epg):
        m = None
        w = jnp.zeros_like(scores[0])
        for g in range(N_GROUPS):
            t = jnp.logical_and(chosen[g], in_top2[g][i])
            m = t if m is None else jnp.logical_or(m, t)
            w = w + jnp.where(t, probs[g * epg + i:g * epg + i + 1], 0.0)
        picked.append(m)
        weight.append(w)
    total = weight[0] + weight[1] + weight[2] + weight[3]
    gate = [w / total for w in weight]
    lo = jnp.where(picked[0], 0.0, jnp.where(picked[1], 1.0, 2.0))
    hi = jnp.where(picked[3], 3.0, jnp.where(picked[2], 2.0, 1.0))
    gate_lo = jnp.where(picked[0], gate[0], jnp.where(picked[1], gate[1], gate[2]))
    gate_hi = jnp.where(picked[3], gate[3], jnp.where(picked[2], gate[2], gate[1]))
    group = sum(jnp.where(chosen[g], float(g), 0.0) for g in range(N_GROUPS))
    base = jnp.where(lo == 0.0, 0.0, jnp.where(lo == 1.0, 3.0, 5.0))
    natural = base + hi - lo - 1.0
    pair = jnp.where(natural == 3.0, 4.0, jnp.where(natural == 4.0, 3.0, natural))
    swapped = natural == 5.0
    cls = group * float(len(PAIRS_A)) + pair
    return cls.astype(I32), jnp.where(swapped, gate_hi, gate_lo), jnp.where(swapped, gate_lo, gate_hi)


def _out_proj_kernel(hf_ref, hb_ref, ag_ref, yb_ref, yc_ref, x_ref, mod_ref, w_ref, lng_ref, lnb_ref,
                     rw_ref, rb_ref, x1_ref, ri_ref, cnt_ref, run_ref, *, alpha, n_ctx_tiles, n_batch):
    tm = x_ref.shape[0]
    ya = (jax.nn.gelu(ag_ref[0]) * (hf_ref[0, 0] + hb_ref[0, 0])).astype(BF16)
    n_a = LRU_WIDTH
    n_b = n_a + GQA_HEADS * GQA_HEAD_DIM
    y = (jnp.dot(ya, w_ref[0:n_a, :], preferred_element_type=F32)
         + jnp.dot(yb_ref[0], w_ref[n_a:n_b, :], preferred_element_type=F32)
         + jnp.dot(yc_ref[0], w_ref[n_b:, :], preferred_element_type=F32))
    x1 = _layer_norm(alpha * x_ref[...] + mod_ref[0, 2:3, :] * y, lng_ref[...], lnb_ref[...])
    d = x1.shape[1]
    x1_ref[:, 0:d] = x1

    h2 = x1 * (1.0 + mod_ref[0, 4:5, :]) + mod_ref[0, 3:4, :]
    h_hi = h2.astype(BF16)
    h_lo = (h2 - h_hi.astype(F32)).astype(BF16)
    both = lax.dot_general(rw_ref[...], h_hi, NT_DIMS, preferred_element_type=F32)
    logits = (both[0:N_EXPERTS] + both[N_EXPERTS:]
              + lax.dot_general(rw_ref[0:N_EXPERTS, :], h_lo, NT_DIMS, preferred_element_type=F32))
    cls, gate_lo, gate_hi = _route(logits, rb_ref[...])

    b = pl.program_id(0)
    i = pl.program_id(1)
    mod_row = jnp.where(i < n_ctx_tiles, n_batch, b).astype(F32)
    extra = jnp.concatenate([gate_lo, gate_hi, jnp.full((1, tm), mod_row, F32), jnp.zeros((LANES - 3, tm), F32)], axis=0)
    x1_ref[:, d:] = extra.T

    @pl.when(jnp.logical_and(b == 0, i == 0))
    def _():
        run_ref[...] = jnp.zeros_like(run_ref)

    member = lax.broadcasted_iota(I32, (CLASS_ROWS, tm), 0) == cls
    before = lax.broadcasted_iota(I32, (tm, tm), 0) < lax.broadcasted_iota(I32, (tm, tm), 1)
    earlier = jnp.dot(member.astype(BF16), before.astype(BF16), preferred_element_type=F32)
    run = run_ref[...]
    rank = jnp.sum(jnp.where(member, earlier + run, 0.0), axis=0, keepdims=True).astype(I32)
    run = run + jnp.sum(member.astype(F32), axis=1, keepdims=True)
    run_ref[...] = run
    cnt_ref[...] = jnp.broadcast_to(run, cnt_ref.shape).astype(I32)
    ri_ref[0] = jnp.concatenate([cls, rank, jnp.zeros((SUBLANES - 2, tm), I32)], axis=0)


def _out_proj(h_lru, zl, yb, yc, xc, mod, w_out, ln_g, ln_b, rw_split, rb, alpha, n_ctx_tiles):
    bsz, sk, _ = zl.shape
    d = xc.shape[1]
    tm = TOKEN_TILE
    nt = sk // tm

    def mod_map(b, i):
        return (jnp.where(i < n_ctx_tiles, bsz, b), 0, 0)

    return pl.pallas_call(
        functools.partial(_out_proj_kernel, alpha=alpha, n_ctx_tiles=n_ctx_tiles, n_batch=bsz),
        out_shape=(
            jax.ShapeDtypeStruct((bsz * sk, d + LANES), F32),
            jax.ShapeDtypeStruct((bsz, SUBLANES, sk), I32),
            jax.ShapeDtypeStruct((CLASS_ROWS, LANES), I32),
        ),
        grid=(bsz, nt),
        in_specs=[
            pl.BlockSpec((1, 1, tm, LRU_WIDTH), lambda b, i: (0, b, i, 0)),
            pl.BlockSpec((1, 1, tm, LRU_WIDTH), lambda b, i: (1, b, i, 0)),
            pl.BlockSpec((1, tm, LRU_WIDTH), lambda b, i: (b, i, 1)),
            pl.BlockSpec((1, tm, GQA_HEADS * GQA_HEAD_DIM), lambda b, i: (b, i, 0)),
            pl.BlockSpec((1, tm, DIFF_HEADS * DIFF_V_DIM), lambda b, i: (b, i, 0)),
            pl.BlockSpec((tm, d), lambda b, i: (b * nt + i, 0)),
            pl.BlockSpec((1, N_MOD, d), mod_map),
            pl.BlockSpec((d, d), lambda b, i: (0, 0)),
            pl.BlockSpec((1, d), lambda b, i: (0, 0)),
            pl.BlockSpec((1, d), lambda b, i: (0, 0)),
            pl.BlockSpec((2 * N_EXPERTS, d), lambda b, i: (0, 0)),
            pl.BlockSpec((N_EXPERTS, 1), lambda b, i: (0, 0)),
        ],
        out_specs=(
            pl.BlockSpec((tm, d + LANES), lambda b, i: (b * nt + i, 0)),
            pl.BlockSpec((1, SUBLANES, tm), lambda b, i: (b, 0, i)),
            pl.BlockSpec((CLASS_ROWS, LANES), lambda b, i: (0, 0)),
        ),
        scratch_shapes=[pltpu.VMEM((CLASS_ROWS, 1), F32)],
        compiler_params=_params(("arbitrary", "arbitrary")),
        name="out_proj_router",
    )(h_lru, h_lru, zl, yb, yc, xc, mod, w_out, ln_g, ln_b, rw_split, rb)


def _moe_kernel(cls_ref, rank_ref, pstart_ref, ea_ref, eb_ref, nv_ref, nu_ref,
                x_hbm, mod_ref, lng_ref, lnb_ref,
                wga_ref, wua_ref, wda_ref, wgb_ref, wub_ref, wdb_ref,
                o_hbm, tok_ref, xbuf, obuf, gsem, ssem, *, alpha, n_batch, n_tok):
    rows = xbuf.shape[1]
    d = obuf.shape[2]
    i = pl.program_id(0)
    n_steps = pl.num_programs(0)
    n_used = nu_ref[0]
    slot = i % 2

    @pl.when(i == 0)
    def _():
        def clear(r, c):
            tok_ref[r] = 0
            return c
        lax.fori_loop(0, tok_ref.shape[0], clear, 0, unroll=8)

        def place(t, c):
            tok_ref[pstart_ref[cls_ref[t]] + rank_ref[t]] = t
            return c
        lax.fori_loop(0, n_tok, place, 0, unroll=8)

        obuf[...] = jnp.zeros_like(obuf)

    def start_gather(step, slot_):
        for r in range(rows):
            pltpu.make_async_copy(x_hbm.at[pl.ds(tok_ref[step * rows + r], 1)], xbuf.at[slot_, pl.ds(r, 1)],
                                  gsem.at[slot_]).start()

    def wait_gather(slot_):
        pltpu.make_async_copy(x_hbm.at[pl.ds(0, rows)], xbuf.at[slot_], gsem.at[slot_]).wait()

    def start_scatter(step, n_valid, slot_):
        for r in range(rows):
            dst = jnp.where(r < n_valid, tok_ref[step * rows + r], n_tok + slot_ * rows + r)
            pltpu.make_async_copy(obuf.at[slot_, pl.ds(r, 1)], o_hbm.at[pl.ds(dst, 1)], ssem.at[slot_]).start()

    def wait_scatter(slot_):
        pltpu.make_async_copy(obuf.at[slot_], o_hbm.at[pl.ds(0, rows)], ssem.at[slot_]).wait()

    def block_step(slot):
        wait_gather(slot)
        start_gather(jnp.minimum(i + 1, n_used - 1), 1 - slot)
        prev = jnp.maximum(i - 1, 0)
        start_scatter(prev, jnp.where(i >= 1, nv_ref[prev], 0), 1 - slot)

        x = xbuf[slot, :, 0:d]
        gate_a = xbuf[slot, :, d:d + 1]
        gate_b = xbuf[slot, :, d + 1:d + 2]
        mid = xbuf[slot, :, d + 2:d + 3]

        def pick(kk):
            v = mod_ref[n_batch, kk:kk + 1, :]
            for b in range(n_batch):
                v = jnp.where(mid == float(b), mod_ref[b, kk:kk + 1, :], v)
            return v

        h = x * (1.0 + pick(4)) + pick(3)

        def ffn(wg, wu, wd):
            a = jnp.dot(h, wg[0], preferred_element_type=F32)
            u = jnp.dot(h, wu[0], preferred_element_type=F32)
            return jnp.dot((a * jax.nn.sigmoid(a)) * u, wd[0], preferred_element_type=F32)

        f = ffn(wga_ref, wua_ref, wda_ref) * gate_a + ffn(wgb_ref, wub_ref, wdb_ref) * gate_b
        y = _layer_norm(alpha * x + pick(5) * f, lng_ref[...], lnb_ref[...])
        wait_scatter(slot)
        obuf[slot] = y

    @pl.when(i == 0)
    def _():
        pltpu.make_async_copy(obuf.at[0], o_hbm.at[pl.ds(n_tok, rows)], ssem.at[0]).start()
        start_gather(0, 0)

    for parity in range(2):
        @pl.when(jnp.logical_and(i < n_used, i % 2 == parity))
        def _(parity=parity):
            block_step(parity)

    @pl.when(i == n_steps - 1)
    def _():
        last = n_used - 1
        q = last % 2
        start_scatter(last, nv_ref[last], q)
        wait_scatter(q)
        wait_scatter(1 - q)
        wait_gather(1 - q)


def _moe(x_wide, n_tok, cls, rank, pstart, blk_ea, blk_eb, blk_nv, n_used, mod, ln_g, ln_b,
         w_gate, w_up, w_down, layer, alpha, n_batch):
    d = x_wide.shape[1] - LANES
    rows = MOE_ROWS
    n_blk = blk_ea.shape[0]
    ff = w_gate.shape[-1]
    first = layer * N_EXPERTS

    def wa(i, cl, rk, ps, ea, eb, nv, nu):
        return (first + ea[i], 0, 0)

    def wb(i, cl, rk, ps, ea, eb, nv, nu):
        return (first + eb[i], 0, 0)

    const2 = lambda i, cl, rk, ps, ea, eb, nv, nu: (0, 0)
    grid_spec = pltpu.PrefetchScalarGridSpec(
        num_scalar_prefetch=7,
        grid=(n_blk,),
        in_specs=[
            pl.BlockSpec(memory_space=pl.ANY),
            pl.BlockSpec(mod.shape, lambda i, cl, rk, ps, ea, eb, nv, nu: (0, 0, 0)),
            pl.BlockSpec((1, d), const2),
            pl.BlockSpec((1, d), const2),
            pl.BlockSpec((1, d, ff), wa),
            pl.BlockSpec((1, d, ff), wa),
            pl.BlockSpec((1, ff, d), wa),
            pl.BlockSpec((1, d, ff), wb),
            pl.BlockSpec((1, d, ff), wb),
            pl.BlockSpec((1, ff, d), wb),
        ],
        out_specs=pl.BlockSpec(memory_space=pl.ANY),
        scratch_shapes=[
            pltpu.SMEM((n_blk * rows,), I32),
            pltpu.VMEM((2, rows, d + LANES), F32),
            pltpu.VMEM((2, rows, d), F32),
            pltpu.SemaphoreType.DMA((2,)),
            pltpu.SemaphoreType.DMA((2,)),
        ],
    )
    return pl.pallas_call(
        functools.partial(_moe_kernel, alpha=alpha, n_batch=n_batch, n_tok=n_tok),
        out_shape=jax.ShapeDtypeStruct((n_tok + 2 * rows, d), F32),
        grid_spec=grid_spec,
        compiler_params=_params(("arbitrary",)),
        name="moe_experts",
    )(cls, rank, pstart, blk_ea, blk_eb, blk_nv, n_used, x_wide, mod, ln_g, ln_b,
      w_gate, w_up, w_down, w_gate, w_up, w_down)


def _block_tables(counts, n_tok):
    rows = MOE_ROWS
    n_blk = n_tok // rows + N_CLASSES
    n_pairs = len(PAIRS_A)
    padded = (counts + rows - 1) // rows * rows
    pend = jnp.cumsum(padded)
    pstart = pend - padded
    n_used = pend[-1] // rows
    blk = jnp.arange(n_blk, dtype=I32)
    blk_ids = jnp.minimum(blk, jnp.maximum(n_used - 1, 0))
    blk_cls = jnp.minimum(jnp.sum((blk_ids[:, None] * rows >= pend[None, :]).astype(I32), axis=1), N_CLASSES - 1)
    onehot = blk_cls[:, None] == jnp.arange(N_CLASSES, dtype=I32)[None, :]
    within = blk_ids * rows - jnp.sum(jnp.where(onehot, pstart[None, :], 0), axis=1)
    count_b = jnp.sum(jnp.where(onehot, counts[None, :], 0), axis=1)
    blk_nv = jnp.where(blk < n_used, jnp.clip(count_b - within, 0, rows), 0).astype(I32)
    pair = blk_cls % n_pairs
    slot_a = sum(jnp.where(pair == p, PAIRS_A[p], 0) for p in range(n_pairs))
    slot_b = sum(jnp.where(pair == p, PAIRS_B[p], 0) for p in range(n_pairs))
    first = blk_cls // n_pairs * EXPERTS_PER_GROUP
    pstart_pad = jnp.concatenate([pstart, jnp.zeros((CLASS_ROWS - N_CLASSES,), I32)]).astype(I32)
    return (pstart_pad, (first + slot_a).astype(I32), (first + slot_b).astype(I32), blk_nv,
            n_used.reshape(1).astype(I32))


def _rope_tables(n_lat, n_ctx, dim):
    rows = n_lat // GRID_W
    row = jnp.repeat(jnp.arange(rows, dtype=F32), GRID_W)
    col = jnp.tile(jnp.arange(GRID_W, dtype=F32), rows)
    n_freq = dim // 4
    inv = ROPE_THETA ** (-jnp.arange(n_freq, dtype=F32) / n_freq)
    ang = jnp.concatenate([row[:, None] * inv, col[:, None] * inv], axis=-1)
    cos = jnp.repeat(jnp.cos(ang), 2, axis=-1)
    sin = jnp.repeat(jnp.sin(ang), 2, axis=-1)
    even = (jnp.arange(dim) % 2 == 0)
    tabs = jnp.stack([cos, jnp.where(even, -sin, 0.0), jnp.where(even, 0.0, sin)])
    ident = jnp.stack([jnp.ones((n_ctx, dim), F32), jnp.zeros((n_ctx, dim), F32), jnp.zeros((n_ctx, dim), F32)])
    return jnp.tile(jnp.concatenate([ident, tabs], axis=1), (1, 1, LANES // dim))


def _block_diag(w):
    n, c, _ = w.shape
    eye = jnp.eye(n, dtype=w.dtype)
    return (eye[:, None, :, None] * w[:, :, None, :]).reshape(n * c, n * c)


def kernel(x, c, ctx, c_ctx, ada_w, ada_b, w_in, conv_w, conv_b, rg_wa, rg_ba, rg_wx, rg_bx, rg_lam, q_norm_g, k_norm_g, diff_lambda, diff_subln_g, w_out, ln1_g, ln1_b, ln2_g, ln2_b, router_w, router_b, exp_w_gate, exp_w_up, exp_w_down):
    bsz, s, d = x.shape
    n_ctx = ctx.shape[1]
    depth = w_in.shape[0]
    sk = n_ctx + s
    assert d == D_MODEL and n_ctx == TOKEN_TILE and s % TOKEN_TILE == 0 and bsz + 1 <= SUBLANES
    n_ctx_tiles = n_ctx // TOKEN_TILE
    n_tok = bsz * sk
    alpha = (2.0 * depth) ** 0.25

    c_rows = jnp.concatenate([c, c_ctx[None, :], jnp.zeros((SUBLANES - bsz - 1, d), F32)], axis=0)
    mods = _ada_modulation(c_rows, ada_w, ada_b).reshape(depth, SUBLANES, N_MOD, d)

    rope_g = _rope_tables(s, n_ctx, GQA_HEAD_DIM)
    rope_d = _rope_tables(s, n_ctx, DIFF_QK_DIM)
    lane = np.arange(LANES)
    seg = jnp.asarray(lane[:, None] // GQA_HEAD_DIM == lane[None, :] // GQA_HEAD_DIM, BF16)
    rw_t = router_w.T
    rw_hi = rw_t.astype(BF16)
    rw_split = jnp.concatenate([rw_hi, (rw_t - rw_hi.astype(F32)).astype(BF16)], axis=0)
    rb = router_b.reshape(N_EXPERTS, 1)
    zero_aux = jnp.zeros((SUBLANES, LANES), F32)

    xc = jnp.concatenate([part for b in range(bsz) for part in (ctx[b], x[b])], axis=0)
    w_gate_all = exp_w_gate.reshape(depth * N_EXPERTS, d, EXPERT_FF)
    w_up_all = exp_w_up.reshape(depth * N_EXPERTS, d, EXPERT_FF)
    w_down_all = exp_w_down.reshape(depth * N_EXPERTS, EXPERT_FF, d)
    for li in range(depth):
        lam_init = 0.8 - 0.6 * math.exp(-0.3 * li)
        mod = mods[li]
        gains = jnp.concatenate([jnp.tile(jnp.tile(q_norm_g[li], 2)[None, :], (4, 1)),
                                 jnp.tile(k_norm_g[li], 2)[None, :], jnp.zeros((3, LANES), F32)], axis=0)
        zl, q, k, vt, dq, dk, dvt = _in_proj(xc, bsz, sk, mod, w_in[li].astype(BF16), gains, seg, rope_g, rope_d,
                                             n_ctx_tiles)

        w_gates = jnp.stack([jnp.concatenate([_block_diag(rg_wa[li, dd]), _block_diag(rg_wx[li, dd])], axis=1)
                             for dd in range(2)]).astype(BF16)
        b_gates = jnp.concatenate([rg_ba[li], rg_bx[li]], axis=-1)[:, None, :]
        h_lru = _lru(zl, conv_w[li], conv_b[li][None, :], w_gates, b_gates, rg_lam[li][:, None, :])

        yb = _attention(q, k, vt, zero_aux, diff=False, n_ctx_tiles=n_ctx_tiles)
        aux = jnp.concatenate([
            jnp.pad(diff_lambda[li], ((0, 0), (0, LANES - DIFF_QK_DIM))),
            jnp.full((1, LANES), lam_init, F32),
            jnp.tile(diff_subln_g[li], 2)[None, :],
            jnp.zeros((2, LANES), F32)], axis=0)
        yc = _attention(dq, dk, dvt, aux, diff=True, n_ctx_tiles=n_ctx_tiles)

        x1, route_i, counts = _out_proj(h_lru, zl, yb, yc, xc, mod, w_out[li].astype(BF16), ln1_g[li][None, :],
                                        ln1_b[li][None, :], rw_split, rb, alpha, n_ctx_tiles)

        pstart, blk_ea, blk_eb, blk_nv, n_used = _block_tables(counts[:N_CLASSES, 0], n_tok)
        xc = _moe(x1, n_tok, route_i[:, 0, :].reshape(n_tok), route_i[:, 1, :].reshape(n_tok), pstart,
                  blk_ea, blk_eb, blk_nv, n_used, mod, ln2_g[li][None, :], ln2_b[li][None, :],
                  w_gate_all, w_up_all, w_down_all, li, alpha, bsz)
    return jnp.stack([xc[b * sk + n_ctx:(b + 1) * sk] for b in range(bsz)])
```

```python
import functools
import math

import numpy as np
import jax
import jax.numpy as jnp
from jax import lax
from jax.experimental import pallas as pl
from jax.experimental.pallas import tpu as pltpu

F32 = jnp.float32
BF16 = jnp.bfloat16
I32 = jnp.int32

D_MODEL = 1024
GRID_W = 64
LRU_WIDTH = 256
LRU_BLOCKS = 4
CONV_W = 4
RG_C = 8.0
GQA_HEADS = 8
GQA_KV_HEADS = 2
GQA_HEAD_DIM = 64
DIFF_HEADS = 4
DIFF_QK_DIM = 32
DIFF_V_DIM = 64
IN_WIDTH = 2048
N_EXPERTS = 16
N_GROUPS = 4
EXPERTS_PER_GROUP = 4
EXPERT_FF = 512
ROPE_THETA = 10000.0
EPS = 1e-6
N_MOD = 6
LOG2_E = math.log2(math.e)

COL_AX, COL_AG, COL_GQ, COL_GK, COL_GV, COL_DQ, COL_DK, COL_DV = 0, 256, 512, 1024, 1152, 1280, 1536, 1792

LANES = 128
SUBLANES = 8
VMEM_LIMIT = 56 * 1024 * 1024

TOKEN_TILE = 256
VT_ROWS = 80
PAIRS_A = (0, 0, 0, 1, 1, 3)
PAIRS_B = (1, 2, 3, 3, 2, 2)
N_CLASSES = N_GROUPS * len(PAIRS_A)
CLASS_ROWS = 32
MOE_ROWS = 128
OUT_PROJ_TILES = 3
MAX_SHIFT_LAG = 60.0
MAX_SHIFT_RAISE = 120.0

NT_DIMS = (((1,), (1,)), ((), ()))


def _params(semantics):
    return pltpu.CompilerParams(dimension_semantics=semantics, vmem_limit_bytes=VMEM_LIMIT)


def _ada_kernel(c_ref, w_ref, b_ref, o_ref):
    c = c_ref[...]
    o_ref[0] = jnp.dot(c * jax.nn.sigmoid(c), w_ref[0], preferred_element_type=F32) + b_ref[0]


def _ada_modulation(c_rows, ada_w, ada_b):
    depth, d, n = ada_w.shape
    tn = 1536
    return pl.pallas_call(
        _ada_kernel,
        out_shape=jax.ShapeDtypeStruct((depth, SUBLANES, n), F32),
        grid=(depth, n // tn),
        in_specs=[
            pl.BlockSpec((SUBLANES, d), lambda l, j: (0, 0)),
            pl.BlockSpec((1, d, tn), lambda l, j: (l, 0, j)),
            pl.BlockSpec((1, 1, tn), lambda l, j: (l, 0, j)),
        ],
        out_specs=pl.BlockSpec((1, SUBLANES, tn), lambda l, j: (l, 0, j)),
        compiler_params=_params(("arbitrary", "arbitrary")),
        name="ada_modulation",
    )(c_rows, ada_w, ada_b.reshape(depth, 1, n))


def _rotate_pairs(v, cos, sin_next, sin_prev):
    return v * cos + pltpu.roll(v, LANES - 1, 1) * sin_next + pltpu.roll(v, 1, 1) * sin_prev


def _in_proj_kernel(x_ref, mod_ref, w_ref, gain_ref, seg_ref, rope_g_ref, rope_d_ref,
                    zl_ref, q_ref, k_ref, vt_ref, dq_ref, dk_ref, dvt_ref):
    tm = x_ref.shape[0]
    x = x_ref[...]
    shift = mod_ref[0, 0:1, :]
    scale = mod_ref[0, 1:2, :]
    h = (x * (1.0 + scale) + shift).astype(BF16)
    z = jnp.dot(h, w_ref[...], preferred_element_type=F32)
    zl_ref[0] = z[:, COL_AX:COL_GQ]

    ones_rows = jnp.where(lax.broadcasted_iota(I32, (VT_ROWS - GQA_HEAD_DIM, tm), 0) == 0, 1.0, 0.0).astype(BF16)

    seg = seg_ref[...]
    cos, s_next, s_prev = rope_g_ref[0], rope_g_ref[1], rope_g_ref[2]
    for g in range(5):
        v = z[:, COL_GQ + g * LANES:COL_GQ + (g + 1) * LANES]
        sq = v * v
        sq_hi = sq.astype(BF16)
        sq_lo = (sq - sq_hi.astype(F32)).astype(BF16)
        ssq = jnp.dot(sq_hi, seg, preferred_element_type=F32) + jnp.dot(sq_lo, seg, preferred_element_type=F32)
        v = v * lax.rsqrt(ssq * (1.0 / GQA_HEAD_DIM) + EPS) * gain_ref[g:g + 1, :]
        v = _rotate_pairs(v, cos, s_next, s_prev)
        if g < 4:
            v = (v * (GQA_HEAD_DIM ** -0.5 * LOG2_E)).astype(BF16)
            q_ref[0, 2 * g] = v[:, :GQA_HEAD_DIM]
            q_ref[0, 2 * g + 1] = v[:, GQA_HEAD_DIM:]
        else:
            v = v.astype(BF16)
            k_ref[0, 0] = v[:, :GQA_HEAD_DIM]
            k_ref[0, 1] = v[:, GQA_HEAD_DIM:]
    vt = z[:, COL_GV:COL_DQ].T.astype(BF16)
    for hd in range(GQA_KV_HEADS):
        vt_ref[0, hd, 0, 0:GQA_HEAD_DIM, :] = vt[hd * GQA_HEAD_DIM:(hd + 1) * GQA_HEAD_DIM]
        vt_ref[0, hd, 0, GQA_HEAD_DIM:VT_ROWS, :] = ones_rows

    cos, s_next, s_prev = rope_d_ref[0], rope_d_ref[1], rope_d_ref[2]
    for g in range(4):
        v = _rotate_pairs(z[:, COL_DQ + g * LANES:COL_DQ + (g + 1) * LANES], cos, s_next, s_prev)
        if g < 2:
            v = v * (DIFF_QK_DIM ** -0.5 * LOG2_E)
        v = v.astype(BF16)
        dst = dq_ref if g < 2 else dk_ref
        for j in range(4):
            dst[0, (g % 2) * 4 + j] = v[:, j * DIFF_QK_DIM:(j + 1) * DIFF_QK_DIM]
    dvt = z[:, COL_DV:IN_WIDTH].T.astype(BF16)
    for hd in range(DIFF_HEADS):
        dvt_ref[0, hd, 0, 0:DIFF_V_DIM, :] = dvt[hd * DIFF_V_DIM:(hd + 1) * DIFF_V_DIM]
        dvt_ref[0, hd, 0, DIFF_V_DIM:VT_ROWS, :] = ones_rows


def _in_proj(xc, bsz, sk, mod, w_in, gains, seg, rope_g, rope_d, n_ctx_tiles):
    d = xc.shape[1]
    tm = TOKEN_TILE
    nt = sk // tm
    n_mod_rows = mod.shape[0]

    def mod_map(b, i):
        return (jnp.where(i < n_ctx_tiles, bsz, b), 0, 0)

    assert n_mod_rows > bsz
    return pl.pallas_call(
        _in_proj_kernel,
        out_shape=(
            jax.ShapeDtypeStruct((bsz, sk, COL_GQ), F32),
            jax.ShapeDtypeStruct((bsz, GQA_HEADS, sk, GQA_HEAD_DIM), BF16),
            jax.ShapeDtypeStruct((bsz, GQA_KV_HEADS, sk, GQA_HEAD_DIM), BF16),
            jax.ShapeDtypeStruct((bsz, GQA_KV_HEADS, nt, VT_ROWS, tm), BF16),
            jax.ShapeDtypeStruct((bsz, 2 * DIFF_HEADS, sk, DIFF_QK_DIM), BF16),
            jax.ShapeDtypeStruct((bsz, 2 * DIFF_HEADS, sk, DIFF_QK_DIM), BF16),
            jax.ShapeDtypeStruct((bsz, DIFF_HEADS, nt, VT_ROWS, tm), BF16),
        ),
        grid=(bsz, nt),
        in_specs=[
            pl.BlockSpec((tm, d), lambda b, i: (b * nt + i, 0)),
            pl.BlockSpec((1, N_MOD, d), mod_map),
            pl.BlockSpec((d, IN_WIDTH), lambda b, i: (0, 0)),
            pl.BlockSpec((SUBLANES, LANES), lambda b, i: (0, 0)),
            pl.BlockSpec((LANES, LANES), lambda b, i: (0, 0)),
            pl.BlockSpec((3, tm, LANES), lambda b, i: (0, i, 0)),
            pl.BlockSpec((3, tm, LANES), lambda b, i: (0, i, 0)),
        ],
        out_specs=(
            pl.BlockSpec((1, tm, COL_GQ), lambda b, i: (b, i, 0)),
            pl.BlockSpec((1, GQA_HEADS, tm, GQA_HEAD_DIM), lambda b, i: (b, 0, i, 0)),
            pl.BlockSpec((1, GQA_KV_HEADS, tm, GQA_HEAD_DIM), lambda b, i: (b, 0, i, 0)),
            pl.BlockSpec((1, GQA_KV_HEADS, 1, VT_ROWS, tm), lambda b, i: (b, 0, i, 0, 0)),
            pl.BlockSpec((1, 2 * DIFF_HEADS, tm, DIFF_QK_DIM), lambda b, i: (b, 0, i, 0)),
            pl.BlockSpec((1, 2 * DIFF_HEADS, tm, DIFF_QK_DIM), lambda b, i: (b, 0, i, 0)),
            pl.BlockSpec((1, DIFF_HEADS, 1, VT_ROWS, tm), lambda b, i: (b, 0, i, 0, 0)),
        ),
        compiler_params=_params(("arbitrary", "arbitrary")),
        name="in_proj",
    )(xc, mod, w_in, gains, seg, rope_g, rope_d)


def _expm1(x):
    u = jnp.exp(x)
    return jnp.where(u == 1.0, x, (u - 1.0) * x / jnp.log(u))


def _lru_kernel(ax_ref, prev_ref, next_ref, cw_ref, cb_ref, w_ref, b_ref, lam_ref, h_ref, carry_ref, *, nt):
    tm = ax_ref.shape[1]
    d = pl.program_id(1)
    j = pl.program_id(2)
    blk = jnp.where(d == 0, j, jnp.where(j == 0, 0, nt - j))

    @pl.when(j == 0)
    def _():
        carry_ref[...] = jnp.zeros_like(carry_ref)

    x = ax_ref[0]
    no_left = jnp.logical_or(blk == 0, blk == 1)
    no_right = jnp.logical_or(blk == 0, blk == nt - 1)
    left = jnp.where(no_left, 0.0, prev_ref[0, SUBLANES - 1:SUBLANES, :])
    right0 = jnp.where(no_right, 0.0, next_ref[0, 0:1, :])
    right1 = jnp.where(no_right, 0.0, next_ref[0, 1:2, :])
    row = lax.broadcasted_iota(I32, (tm, 1), 0)
    x_m1 = jnp.where(row == 0, left, pltpu.roll(x, 1, 0))
    x_p1 = jnp.where(row == tm - 1, right0, pltpu.roll(x, tm - 1, 0))
    x_p2 = jnp.where(row == tm - 2, right0, jnp.where(row == tm - 1, right1, pltpu.roll(x, tm - 2, 0)))
    u = cb_ref[...] + x_m1 * cw_ref[0:1, :] + x * cw_ref[1:2, :] + x_p1 * cw_ref[2:3, :] + x_p2 * cw_ref[3:4, :]

    g = jnp.dot(u.astype(BF16), w_ref[0], preferred_element_type=F32) + b_ref[0]
    r = jax.nn.sigmoid(g[:, :LRU_WIDTH])
    gate_in = jax.nn.sigmoid(g[:, LRU_WIDTH:])
    neg_lam = -lam_ref[0]
    softplus = jnp.maximum(neg_lam, 0.0) + jnp.log1p(jnp.exp(-jnp.abs(neg_lam)))
    log_a = (-RG_C * r) * softplus
    a = jnp.exp(log_a)
    b = jnp.sqrt(-_expm1(2.0 * log_a)) * (gate_in * u)

    def scan(reverse):
        aa, bb = a, b
        in_group = row % SUBLANES
        s = 1
        while s < SUBLANES:
            shift = tm - s if reverse else s
            keep = (in_group < SUBLANES - s) if reverse else (in_group >= s)
            a_sh = jnp.where(keep, pltpu.roll(aa, shift, 0), 1.0)
            b_sh = jnp.where(keep, pltpu.roll(bb, shift, 0), 0.0)
            bb = aa * b_sh + bb
            aa = aa * a_sh
            s *= 2
        state = carry_ref[...]
        n_groups = tm // SUBLANES
        for g in (range(n_groups - 1, -1, -1) if reverse else range(n_groups)):
            lo = g * SUBLANES
            h = aa[lo:lo + SUBLANES] * state + bb[lo:lo + SUBLANES]
            h_ref[0, 0, lo:lo + SUBLANES, :] = h
            state = h[0:1, :] if reverse else h[SUBLANES - 1:SUBLANES, :]
        carry_ref[...] = state

    @pl.when(d == 0)
    def _():
        scan(False)

    @pl.when(d == 1)
    def _():
        scan(True)


def _lru(zl, conv_w, conv_b, w_gates, b_gates, lam):
    bsz, sk, _ = zl.shape
    tm = TOKEN_TILE
    nt = sk // tm
    per = tm // SUBLANES

    def blk_of(d, j):
        return jnp.where(d == 0, j, jnp.where(j == 0, 0, nt - j))

    return pl.pallas_call(
        functools.partial(_lru_kernel, nt=nt),
        out_shape=jax.ShapeDtypeStruct((2, bsz, sk, LRU_WIDTH), F32),
        grid=(bsz, 2, nt),
        in_specs=[
            pl.BlockSpec((1, tm, LRU_WIDTH), lambda b, d, j: (b, blk_of(d, j), 0)),
            pl.BlockSpec((1, SUBLANES, LRU_WIDTH), lambda b, d, j: (b, jnp.maximum(blk_of(d, j) * per - 1, 0), 0)),
            pl.BlockSpec((1, SUBLANES, LRU_WIDTH),
                         lambda b, d, j: (b, jnp.minimum((blk_of(d, j) + 1) * per, nt * per - 1), 0)),
            pl.BlockSpec((CONV_W, LRU_WIDTH), lambda b, d, j: (0, 0)),
            pl.BlockSpec((1, LRU_WIDTH), lambda b, d, j: (0, 0)),
            pl.BlockSpec((1, LRU_WIDTH, 2 * LRU_WIDTH), lambda b, d, j: (d, 0, 0)),
            pl.BlockSpec((1, 1, 2 * LRU_WIDTH), lambda b, d, j: (d, 0, 0)),
            pl.BlockSpec((1, 1, LRU_WIDTH), lambda b, d, j: (d, 0, 0)),
        ],
        out_specs=pl.BlockSpec((1, 1, tm, LRU_WIDTH), lambda b, d, j: (d, b, blk_of(d, j), 0)),
        scratch_shapes=[pltpu.VMEM((1, LRU_WIDTH), F32)],
        compiler_params=_params(("arbitrary", "arbitrary", "arbitrary")),
        name="rg_lru",
    )(zl, zl, zl, conv_w, conv_b, w_gates, b_gates, lam)


def _attn_kernel(q_ref, k_ref, vt_ref, aux_ref, o_ref, s_ref, m_ref, acc_ref, *, k_heads, v_heads, nt, n_ctx_tiles,
                 diff):
    tq = q_ref.shape[2]
    tk = tq
    dv = GQA_HEAD_DIM
    qi = pl.program_id(2)

    def scores(kt):
        return jnp.concatenate(
            [lax.dot_general(k_ref[0, k_heads[j], kt * tk:(kt + 1) * tk, :], q_ref[0, j], NT_DIMS,
                             preferred_element_type=F32) for j in range(4)], axis=1)

    def weighted_values(kt, p):
        return jnp.concatenate(
            [jnp.dot(vt_ref[0, v_heads[j], kt], p[:, j * tq:(j + 1) * tq], preferred_element_type=F32)
             for j in range(4)], axis=1)

    def finalize(acc_all):
        outs = []
        for j in range(4):
            acc = acc_all[:, j * tq:(j + 1) * tq]
            outs.append(acc[0:dv] / acc[dv:dv + 1])
        if diff:
            dl = aux_ref[0:4, :]
            lam_init = aux_ref[4:5, 0:1]
            lam = (jnp.exp(jnp.sum(dl[0:1] * dl[1:2], axis=1, keepdims=True))
                   - jnp.exp(jnp.sum(dl[2:3] * dl[3:4], axis=1, keepdims=True)) + lam_init)
            heads = []
            for hd in range(2):
                o = outs[2 * hd] - lam * outs[2 * hd + 1]
                heads.append(o * lax.rsqrt(jnp.mean(o * o, axis=0, keepdims=True) + EPS))
            o_ref[0] = ((jnp.concatenate(heads, axis=0).T * aux_ref[5:6, :]) * (1.0 - lam_init)).astype(o_ref.dtype)
        else:
            for pair in range(2):
                o_ref[0, :, pair * 2 * dv:(pair + 1) * 2 * dv] = (
                    jnp.concatenate(outs[2 * pair:2 * pair + 2], axis=0).T.astype(o_ref.dtype))

    s_first = scores(0)
    m_first = jnp.max(s_first, axis=0, keepdims=True)

    @pl.when(qi < n_ctx_tiles)
    def _():
        finalize(weighted_values(0, jnp.exp2(s_first - m_first).astype(BF16)))

    @pl.when(qi >= n_ctx_tiles)
    def _():
        s_ref[...] = s_first
        m_ref[...] = m_first

        def attempt(again):
            m0 = m_ref[...]
            s_cur = s_ref[...]
            peak = jnp.zeros((1, 4 * tq), BF16)
            acc = None
            for u in range(nt):
                s_next = scores(u + 1) if u + 1 < nt else None
                p = jnp.exp2(s_cur - m0).astype(BF16)
                peak = jnp.maximum(peak, jnp.max(p, axis=0, keepdims=True))
                pv = weighted_values(u, p)
                acc = pv if acc is None else acc + pv
                s_cur = s_next
            acc_ref[...] = acc
            peak = peak.astype(F32)
            lagging = jnp.max(peak) > 2.0 ** MAX_SHIFT_LAG

            @pl.when(lagging)
            def _():
                m_ref[...] = m0 + jnp.clip(jnp.log(peak) * LOG2_E, 0.0, MAX_SHIFT_RAISE)

            return lagging

        lax.while_loop(lambda again: again, attempt, True)
        finalize(acc_ref[...])


def _attention(q, k, vt, aux, *, diff, n_ctx_tiles):
    bsz, n_q, sk, dh = q.shape
    n_k_heads = k.shape[1]
    n_v_heads = vt.shape[1]
    tq = TOKEN_TILE
    nt = sk // tq
    assert n_ctx_tiles == 1
    groups = n_q // 4
    kpg = n_k_heads // groups
    vpg = n_v_heads // groups
    k_heads = tuple(j * kpg // 4 for j in range(4))
    v_heads = tuple(j * vpg // 4 for j in range(4))
    width = 2 * DIFF_V_DIM if diff else 4 * GQA_HEAD_DIM
    return pl.pallas_call(
        functools.partial(_attn_kernel, k_heads=k_heads, v_heads=v_heads, nt=nt, n_ctx_tiles=n_ctx_tiles, diff=diff),
        out_shape=jax.ShapeDtypeStruct((bsz, sk, groups * width), BF16),
        grid=(bsz, groups, nt),
        in_specs=[
            pl.BlockSpec((1, 4, tq, dh), lambda b, g, i: (b, g, i, 0)),
            pl.BlockSpec((1, kpg, sk, dh), lambda b, g, i: (b, g, 0, 0)),
            pl.BlockSpec((1, vpg, nt, VT_ROWS, tq), lambda b, g, i: (b, g, 0, 0, 0)),
            pl.BlockSpec((SUBLANES, LANES), lambda b, g, i: (0, 0)),
        ],
        out_specs=pl.BlockSpec((1, tq, width), lambda b, g, i: (b, i, g)),
        scratch_shapes=[pltpu.VMEM((tq, 4 * tq), F32), pltpu.VMEM((1, 4 * tq), F32),
                        pltpu.VMEM((VT_ROWS, 4 * tq), F32)],
        compiler_params=_params(("arbitrary", "arbitrary", "arbitrary")),
        name="diff_attention" if diff else "gqa_attention",
    )(q, k, vt, aux)


def _layer_norm(r, g, b):
    mu = jnp.mean(r, axis=-1, keepdims=True)
    c = r - mu
    var = jnp.mean(c * c, axis=-1, keepdims=True)
    return c * lax.rsqrt(var + EPS) * g + b


def _route(logits, bias):
    mx = jnp.max(logits, axis=0, keepdims=True)
    e = jnp.exp(logits - mx)
    probs = e / jnp.sum(e, axis=0, keepdims=True)
    sel = probs + bias
    epg = EXPERTS_PER_GROUP
    in_top2, scores = [], []
    for g in range(N_GROUPS):
        v = [sel[g * epg + i:g * epg + i + 1] for i in range(epg)]
        masks = []
        for i in range(epg):
            rank = jnp.zeros_like(v[i])
            for j in range(epg):
                if j != i:
                    beats = (v[j] > v[i]) if j > i else (v[j] >= v[i])
                    rank = rank + beats.astype(F32)
            masks.append(rank < 2.0)
        in_top2.append(masks)
        scores.append(sum(jnp.where(masks[i], v[i], 0.0) for i in range(epg)))
    chosen = []
    for g in range(N_GROUPS):
        c = None
        for j in range(N_GROUPS):
            if j != g:
                t = (scores[j] < scores[g]) if j < g else (scores[j] <= scores[g])
                c = t if c is None else jnp.logical_and(c, t)
        chosen.append(c)
    picked, weight = [], []
    for i in range(epg):
        m = None
        w = jnp.zeros_like(scores[0])
        for g in range(N_GROUPS):
            t = jnp.logical_and(chosen[g], in_top2[g][i])
            m = t if m is None else jnp.logical_or(m, t)
            w = w + jnp.where(t, probs[g * epg + i:g * epg + i + 1], 0.0)
        picked.append(m)
        weight.append(w)
    total = weight[0] + weight[1] + weight[2] + weight[3]
    gate = [w / total for w in weight]
    lo = jnp.where(picked[0], 0.0, jnp.where(picked[1], 1.0, 2.0))
    hi = jnp.where(picked[3], 3.0, jnp.where(picked[2], 2.0, 1.0))
    gate_lo = jnp.where(picked[0], gate[0], jnp.where(picked[1], gate[1], gate[2]))
    gate_hi = jnp.where(picked[3], gate[3], jnp.where(picked[2], gate[2], gate[1]))
    group = sum(jnp.where(chosen[g], float(g), 0.0) for g in range(N_GROUPS))
    base = jnp.where(lo == 0.0, 0.0, jnp.where(lo == 1.0, 3.0, 5.0))
    natural = base + hi - lo - 1.0
    pair = jnp.where(natural == 3.0, 4.0, jnp.where(natural == 4.0, 3.0, natural))
    swapped = natural == 5.0
    cls = group * float(len(PAIRS_A)) + pair
    return cls.astype(I32), jnp.where(swapped, gate_hi, gate_lo), jnp.where(swapped, gate_lo, gate_hi)


def _out_proj_kernel(hf_ref, hb_ref, ag_ref, yb_ref, yc_ref, x_ref, mod_ref, w_ref, lng_ref, lnb_ref,
                     rw_ref, rb_ref, x1_ref, ri_ref, cnt_ref, run_ref, *, alpha, n_ctx_tiles, n_batch, sub_tiles):
    tm = x_ref.shape[0] // sub_tiles
    d = x_ref.shape[1]
    n_a = LRU_WIDTH
    n_b = n_a + GQA_HEADS * GQA_HEAD_DIM
    b = pl.program_id(0)
    i = pl.program_id(1)

    @pl.when(jnp.logical_and(b == 0, i == 0))
    def _():
        run_ref[...] = jnp.zeros_like(run_ref)

    before = (lax.broadcasted_iota(I32, (tm, tm), 0) < lax.broadcasted_iota(I32, (tm, tm), 1)).astype(BF16)
    run = run_ref[...]
    logit_parts, mod_rows = [], []
    for sub in range(sub_tiles):
        rows = slice(sub * tm, (sub + 1) * tm)
        mod_row = jnp.where(i * sub_tiles + sub < n_ctx_tiles, n_batch, b)
        ya = (jax.nn.gelu(ag_ref[0, rows, :]) * (hf_ref[0, 0, rows, :] + hb_ref[0, 0, rows, :])).astype(BF16)
        y = (jnp.dot(ya, w_ref[0:n_a, :], preferred_element_type=F32)
             + jnp.dot(yb_ref[0, rows, :], w_ref[n_a:n_b, :], preferred_element_type=F32)
             + jnp.dot(yc_ref[0, rows, :], w_ref[n_b:, :], preferred_element_type=F32))
        x1 = _layer_norm(alpha * x_ref[rows, :] + mod_ref[mod_row, 2:3, :] * y, lng_ref[...], lnb_ref[...])
        x1_ref[rows, 0:d] = x1

        h2 = x1 * (1.0 + mod_ref[mod_row, 4:5, :]) + mod_ref[mod_row, 3:4, :]
        h_hi = h2.astype(BF16)
        h_lo = (h2 - h_hi.astype(F32)).astype(BF16)
        both = lax.dot_general(rw_ref[...], h_hi, NT_DIMS, preferred_element_type=F32)
        logit_parts.append(both[0:N_EXPERTS] + both[N_EXPERTS:]
                           + lax.dot_general(rw_ref[0:N_EXPERTS, :], h_lo, NT_DIMS, preferred_element_type=F32))
        mod_rows.append(mod_row)

    cls, gate_a, gate_b = _route(jnp.concatenate(logit_parts, axis=1), rb_ref[...])
    member = lax.broadcasted_iota(I32, (CLASS_ROWS, sub_tiles * tm), 0) == cls
    for sub in range(sub_tiles):
        rows = slice(sub * tm, (sub + 1) * tm)
        extra = jnp.concatenate([gate_a[:, rows], gate_b[:, rows], jnp.full((1, tm), mod_rows[sub].astype(F32), F32),
                                 jnp.zeros((LANES - 3, tm), F32)], axis=0)
        x1_ref[rows, d:] = extra.T

        in_class = member[:, rows]
        earlier = jnp.dot(in_class.astype(BF16), before, preferred_element_type=F32)
        rank = jnp.sum(jnp.where(in_class, earlier + run, 0.0), axis=0, keepdims=True).astype(I32)
        run = run + jnp.sum(in_class.astype(F32), axis=1, keepdims=True)
        ri_ref[0, :, rows] = jnp.concatenate([cls[:, rows], rank, jnp.zeros((SUBLANES - 2, tm), I32)], axis=0)
    run_ref[...] = run
    cnt_ref[...] = jnp.broadcast_to(run, cnt_ref.shape).astype(I32)


def _out_proj(h_lru, zl, yb, yc, xc, mod, w_out, ln_g, ln_b, rw_split, rb, alpha, n_ctx_tiles):
    bsz, sk, _ = zl.shape
    d = xc.shape[1]
    nt = sk // TOKEN_TILE
    sub_tiles = OUT_PROJ_TILES if nt % OUT_PROJ_TILES == 0 else 1
    tm = sub_tiles * TOKEN_TILE
    nt = nt // sub_tiles

    return pl.pallas_call(
        functools.partial(_out_proj_kernel, alpha=alpha, n_ctx_tiles=n_ctx_tiles, n_batch=bsz,
                          sub_tiles=sub_tiles),
        out_shape=(
            jax.ShapeDtypeStruct((bsz * sk, d + LANES), F32),
            jax.ShapeDtypeStruct((bsz, SUBLANES, sk), I32),
            jax.ShapeDtypeStruct((CLASS_ROWS, LANES), I32),
        ),
        grid=(bsz, nt),
        in_specs=[
            pl.BlockSpec((1, 1, tm, LRU_WIDTH), lambda b, i: (0, b, i, 0)),
            pl.BlockSpec((1, 1, tm, LRU_WIDTH), lambda b, i: (1, b, i, 0)),
            pl.BlockSpec((1, tm, LRU_WIDTH), lambda b, i: (b, i, 1)),
            pl.BlockSpec((1, tm, GQA_HEADS * GQA_HEAD_DIM), lambda b, i: (b, i, 0)),
            pl.BlockSpec((1, tm, DIFF_HEADS * DIFF_V_DIM), lambda b, i: (b, i, 0)),
            pl.BlockSpec((tm, d), lambda b, i: (b * nt + i, 0)),
            pl.BlockSpec(mod.shape, lambda b, i: (0, 0, 0)),
            pl.BlockSpec((d, d), lambda b, i: (0, 0)),
            pl.BlockSpec((1, d), lambda b, i: (0, 0)),
            pl.BlockSpec((1, d), lambda b, i: (0, 0)),
            pl.BlockSpec((2 * N_EXPERTS, d), lambda b, i: (0, 0)),
            pl.BlockSpec((N_EXPERTS, 1), lambda b, i: (0, 0)),
        ],
        out_specs=(
            pl.BlockSpec((tm, d + LANES), lambda b, i: (b * nt + i, 0)),
            pl.BlockSpec((1, SUBLANES, tm), lambda b, i: (b, 0, i)),
            pl.BlockSpec((CLASS_ROWS, LANES), lambda b, i: (0, 0)),
        ),
        scratch_shapes=[pltpu.VMEM((CLASS_ROWS, 1), F32)],
        compiler_params=_params(("arbitrary", "arbitrary")),
        name="out_proj_router",
    )(h_lru, h_lru, zl, yb, yc, xc, mod, w_out, ln_g, ln_b, rw_split, rb)


def _moe_kernel(cls_ref, rank_ref, pstart_ref, ea_ref, eb_ref, nv_ref, nu_ref,
                x_hbm, mod_ref, lng_ref, lnb_ref,
                wga_ref, wua_ref, wda_ref, wgb_ref, wub_ref, wdb_ref,
                o_hbm, tok_ref, xbuf, obuf, gsem, ssem, *, alpha, n_batch, n_tok):
    rows = xbuf.shape[1]
    d = obuf.shape[2]
    i = pl.program_id(0)
    n_steps = pl.num_programs(0)
    n_used = nu_ref[0]
    slot = i % 2

    @pl.when(i == 0)
    def _():
        def clear(r, c):
            tok_ref[r] = 0
            return c
        lax.fori_loop(0, tok_ref.shape[0], clear, 0, unroll=8)

        def place(t, c):
            tok_ref[pstart_ref[cls_ref[t]] + rank_ref[t]] = t
            return c
        lax.fori_loop(0, n_tok, place, 0, unroll=8)

        obuf[...] = jnp.zeros_like(obuf)

    def start_gather(step, slot_):
        for r in range(rows):
            pltpu.make_async_copy(x_hbm.at[pl.ds(tok_ref[step * rows + r], 1)], xbuf.at[slot_, pl.ds(r, 1)],
                                  gsem.at[slot_]).start()

    def wait_gather(slot_):
        pltpu.make_async_copy(x_hbm.at[pl.ds(0, rows)], xbuf.at[slot_], gsem.at[slot_]).wait()

    def start_scatter(step, n_valid, slot_):
        for r in range(rows):
            dst = jnp.where(r < n_valid, tok_ref[step * rows + r], n_tok + slot_ * rows + r)
            pltpu.make_async_copy(obuf.at[slot_, pl.ds(r, 1)], o_hbm.at[pl.ds(dst, 1)], ssem.at[slot_]).start()

    def wait_scatter(slot_):
        pltpu.make_async_copy(obuf.at[slot_], o_hbm.at[pl.ds(0, rows)], ssem.at[slot_]).wait()

    def block_step(slot):
        wait_gather(slot)
        start_gather(jnp.minimum(i + 1, n_used - 1), 1 - slot)
        prev = jnp.maximum(i - 1, 0)
        start_scatter(prev, jnp.where(i >= 1, nv_ref[prev], 0), 1 - slot)

        x = xbuf[slot, :, 0:d]
        gate_a = xbuf[slot, :, d:d + 1]
        gate_b = xbuf[slot, :, d + 1:d + 2]
        mid = xbuf[slot, :, d + 2:d + 3]

        def pick(kk):
            v = mod_ref[n_batch, kk:kk + 1, :]
            for b in range(n_batch):
                v = jnp.where(mid == float(b), mod_ref[b, kk:kk + 1, :], v)
            return v

        h = x * (1.0 + pick(4)) + pick(3)

        def ffn(wg, wu, wd):
            a = jnp.dot(h, wg[0], preferred_element_type=F32)
            u = jnp.dot(h, wu[0], preferred_element_type=F32)
            return jnp.dot((a * jax.nn.sigmoid(a)) * u, wd[0], preferred_element_type=F32)

        f = ffn(wga_ref, wua_ref, wda_ref) * gate_a + ffn(wgb_ref, wub_ref, wdb_ref) * gate_b
        y = _layer_norm(alpha * x + pick(5) * f, lng_ref[...], lnb_ref[...])
        wait_scatter(slot)
        obuf[slot] = y

    @pl.when(i == 0)
    def _():
        pltpu.make_async_copy(obuf.at[0], o_hbm.at[pl.ds(n_tok, rows)], ssem.at[0]).start()
        start_gather(0, 0)

    for parity in range(2):
        @pl.when(jnp.logical_and(i < n_used, i % 2 == parity))
        def _(parity=parity):
            block_step(parity)

    @pl.when(i == n_steps - 1)
    def _():
        last = n_used - 1
        q = last % 2
        start_scatter(last, nv_ref[last], q)
        wait_scatter(q)
        wait_scatter(1 - q)
        wait_gather(1 - q)


def _moe(x_wide, n_tok, cls, rank, pstart, blk_ea, blk_eb, blk_nv, n_used, mod, ln_g, ln_b,
         w_gate, w_up, w_down, layer, alpha, n_batch):
    d = x_wide.shape[1] - LANES
    rows = MOE_ROWS
    n_blk = blk_ea.shape[0]
    ff = w_gate.shape[-1]
    first = layer * N_EXPERTS

    def wa(i, cl, rk, ps, ea, eb, nv, nu):
        return (first + ea[i], 0, 0)

    def wb(i, cl, rk, ps, ea, eb, nv, nu):
        return (first + eb[i], 0, 0)

    const2 = lambda i, cl, rk, ps, ea, eb, nv, nu: (0, 0)
    grid_spec = pltpu.PrefetchScalarGridSpec(
        num_scalar_prefetch=7,
        grid=(n_blk,),
        in_specs=[
            pl.BlockSpec(memory_space=pl.ANY),
            pl.BlockSpec(mod.shape, lambda i, cl, rk, ps, ea, eb, nv, nu: (0, 0, 0)),
            pl.BlockSpec((1, d), const2),
            pl.BlockSpec((1, d), const2),
            pl.BlockSpec((1, d, ff), wa),
            pl.BlockSpec((1, d, ff), wa),
            pl.BlockSpec((1, ff, d), wa),
            pl.BlockSpec((1, d, ff), wb),
            pl.BlockSpec((1, d, ff), wb),
            pl.BlockSpec((1, ff, d), wb),
        ],
        out_specs=pl.BlockSpec(memory_space=pl.ANY),
        scratch_shapes=[
            pltpu.SMEM((n_blk * rows,), I32),
            pltpu.VMEM((2, rows, d + LANES), F32),
            pltpu.VMEM((2, rows, d), F32),
            pltpu.SemaphoreType.DMA((2,)),
            pltpu.SemaphoreType.DMA((2,)),
        ],
    )
    return pl.pallas_call(
        functools.partial(_moe_kernel, alpha=alpha, n_batch=n_batch, n_tok=n_tok),
        out_shape=jax.ShapeDtypeStruct((n_tok + 2 * rows, d), F32),
        grid_spec=grid_spec,
        compiler_params=_params(("arbitrary",)),
        name="moe_experts",
    )(cls, rank, pstart, blk_ea, blk_eb, blk_nv, n_used, x_wide, mod, ln_g, ln_b,
      w_gate, w_up, w_down, w_gate, w_up, w_down)


def _block_tables(counts, n_tok):
    rows = MOE_ROWS
    n_blk = n_tok // rows + N_CLASSES
    n_pairs = len(PAIRS_A)
    padded = (counts + rows - 1) // rows * rows
    pend = jnp.cumsum(padded)
    pstart = pend - padded
    n_used = pend[-1] // rows
    blk = jnp.arange(n_blk, dtype=I32)
    blk_ids = jnp.minimum(blk, jnp.maximum(n_used - 1, 0))
    blk_cls = jnp.minimum(jnp.sum((blk_ids[:, None] * rows >= pend[None, :]).astype(I32), axis=1), N_CLASSES - 1)
    onehot = blk_cls[:, None] == jnp.arange(N_CLASSES, dtype=I32)[None, :]
    within = blk_ids * rows - jnp.sum(jnp.where(onehot, pstart[None, :], 0), axis=1)
    count_b = jnp.sum(jnp.where(onehot, counts[None, :], 0), axis=1)
    blk_nv = jnp.where(blk < n_used, jnp.clip(count_b - within, 0, rows), 0).astype(I32)
    pair = blk_cls % n_pairs
    slot_a = sum(jnp.where(pair == p, PAIRS_A[p], 0) for p in range(n_pairs))
    slot_b = sum(jnp.where(pair == p, PAIRS_B[p], 0) for p in range(n_pairs))
    first = blk_cls // n_pairs * EXPERTS_PER_GROUP
    pstart_pad = jnp.concatenate([pstart, jnp.zeros((CLASS_ROWS - N_CLASSES,), I32)]).astype(I32)
    return (pstart_pad, (first + slot_a).astype(I32), (first + slot_b).astype(I32), blk_nv,
            n_used.reshape(1).astype(I32))


def _rope_tables(n_lat, n_ctx, dim):
    rows = n_lat // GRID_W
    row = jnp.repeat(jnp.arange(rows, dtype=F32), GRID_W)
    col = jnp.tile(jnp.arange(GRID_W, dtype=F32), rows)
    n_freq = dim // 4
    inv = ROPE_THETA ** (-jnp.arange(n_freq, dtype=F32) / n_freq)
    ang = jnp.concatenate([row[:, None] * inv, col[:, None] * inv], axis=-1)
    cos = jnp.repeat(jnp.cos(ang), 2, axis=-1)
    sin = jnp.repeat(jnp.sin(ang), 2, axis=-1)
    even = (jnp.arange(dim) % 2 == 0)
    tabs = jnp.stack([cos, jnp.where(even, -sin, 0.0), jnp.where(even, 0.0, sin)])
    ident = jnp.stack([jnp.ones((n_ctx, dim), F32), jnp.zeros((n_ctx, dim), F32), jnp.zeros((n_ctx, dim), F32)])
    return jnp.tile(jnp.concatenate([ident, tabs], axis=1), (1, 1, LANES // dim))


def _block_diag(w):
    n, c, _ = w.shape
    eye = jnp.eye(n, dtype=w.dtype)
    return (eye[:, None, :, None] * w[:, :, None, :]).reshape(n * c, n * c)


def kernel(x, c, ctx, c_ctx, ada_w, ada_b, w_in, conv_w, conv_b, rg_wa, rg_ba, rg_wx, rg_bx, rg_lam, q_norm_g, k_norm_g, diff_lambda, diff_subln_g, w_out, ln1_g, ln1_b, ln2_g, ln2_b, router_w, router_b, exp_w_gate, exp_w_up, exp_w_down):
    bsz, s, d = x.shape
    n_ctx = ctx.shape[1]
    depth = w_in.shape[0]
    sk = n_ctx + s
    assert d == D_MODEL and n_ctx == TOKEN_TILE and s % TOKEN_TILE == 0 and bsz + 1 <= SUBLANES
    n_ctx_tiles = n_ctx // TOKEN_TILE
    n_tok = bsz * sk
    alpha = (2.0 * depth) ** 0.25

    c_rows = jnp.concatenate([c, c_ctx[None, :], jnp.zeros((SUBLANES - bsz - 1, d), F32)], axis=0)
    mods = _ada_modulation(c_rows, ada_w, ada_b).reshape(depth, SUBLANES, N_MOD, d)

    rope_g = _rope_tables(s, n_ctx, GQA_HEAD_DIM)
    rope_d = _rope_tables(s, n_ctx, DIFF_QK_DIM)
    lane = np.arange(LANES)
    seg = jnp.asarray(lane[:, None] // GQA_HEAD_DIM == lane[None, :] // GQA_HEAD_DIM, BF16)
    rw_t = router_w.T
    rw_hi = rw_t.astype(BF16)
    rw_split = jnp.concatenate([rw_hi, (rw_t - rw_hi.astype(F32)).astype(BF16)], axis=0)
    rb = router_b.reshape(N_EXPERTS, 1)
    zero_aux = jnp.zeros((SUBLANES, LANES), F32)

    xc = jnp.concatenate([part for b in range(bsz) for part in (ctx[b], x[b])], axis=0)
    w_gate_all = exp_w_gate.reshape(depth * N_EXPERTS, d, EXPERT_FF)
    w_up_all = exp_w_up.reshape(depth * N_EXPERTS, d, EXPERT_FF)
    w_down_all = exp_w_down.reshape(depth * N_EXPERTS, EXPERT_FF, d)
    for li in range(depth):
        lam_init = 0.8 - 0.6 * math.exp(-0.3 * li)
        mod = mods[li]
        gains = jnp.concatenate([jnp.tile(jnp.tile(q_norm_g[li], 2)[None, :], (4, 1)),
                                 jnp.tile(k_norm_g[li], 2)[None, :], jnp.zeros((3, LANES), F32)], axis=0)
        zl, q, k, vt, dq, dk, dvt = _in_proj(xc, bsz, sk, mod, w_in[li].astype(BF16), gains, seg, rope_g, rope_d,
                                             n_ctx_tiles)

        w_gates = jnp.stack([jnp.concatenate([_block_diag(rg_wa[li, dd]), _block_diag(rg_wx[li, dd])], axis=1)
                             for dd in range(2)]).astype(BF16)
        b_gates = jnp.concatenate([rg_ba[li], rg_bx[li]], axis=-1)[:, None, :]
        h_lru = _lru(zl, conv_w[li], conv_b[li][None, :], w_gates, b_gates, rg_lam[li][:, None, :])

        yb = _attention(q, k, vt, zero_aux, diff=False, n_ctx_tiles=n_ctx_tiles)
        aux = jnp.concatenate([
            jnp.pad(diff_lambda[li], ((0, 0), (0, LANES - DIFF_QK_DIM))),
            jnp.full((1, LANES), lam_init, F32),
            jnp.tile(diff_subln_g[li], 2)[None, :],
            jnp.zeros((2, LANES), F32)], axis=0)
        yc = _attention(dq, dk, dvt, aux, diff=True, n_ctx_tiles=n_ctx_tiles)

        x1, route_i, counts = _out_proj(h_lru, zl, yb, yc, xc, mod, w_out[li].astype(BF16), ln1_g[li][None, :],
                                        ln1_b[li][None, :], rw_split, rb, alpha, n_ctx_tiles)

        pstart, blk_ea, blk_eb, blk_nv, n_used = _block_tables(counts[:N_CLASSES, 0], n_tok)
        xc = _moe(x1, n_tok, route_i[:, 0, :].reshape(n_tok), route_i[:, 1, :].reshape(n_tok), pstart,
                  blk_ea, blk_eb, blk_nv, n_used, mod, ln2_g[li][None, :], ln2_b[li][None, :],
                  w_gate_all, w_up_all, w_down_all, li, alpha, bsz)
    return jnp.stack([xc[b * sk + n_ctx:(b + 1) * sk] for b in range(bsz)])
```

```python
import functools
import math

import numpy as np
import jax
import jax.numpy as jnp
from jax import lax
from jax.experimental import pallas as pl
from jax.experimental.pallas import tpu as pltpu

F32 = jnp.float32
BF16 = jnp.bfloat16
I32 = jnp.int32

D_MODEL = 1024
GRID_W = 64
LRU_WIDTH = 256
LRU_BLOCKS = 4
CONV_W = 4
RG_C = 8.0
GQA_HEADS = 8
GQA_KV_HEADS = 2
GQA_HEAD_DIM = 64
DIFF_HEADS = 4
DIFF_QK_DIM = 32
DIFF_V_DIM = 64
IN_WIDTH = 2048
N_EXPERTS = 16
N_GROUPS = 4
EXPERTS_PER_GROUP = 4
EXPERT_FF = 512
ROPE_THETA = 10000.0
EPS = 1e-6
N_MOD = 6
LOG2_E = math.log2(math.e)

COL_AX, COL_AG, COL_GQ, COL_GK, COL_GV, COL_DQ, COL_DK, COL_DV = 0, 256, 512, 1024, 1152, 1280, 1536, 1792

LANES = 128
SUBLANES = 8
VMEM_LIMIT = 56 * 1024 * 1024

TOKEN_TILE = 256
VT_ROWS = 80
PAIRS_A = (0, 0, 0, 1, 1, 3)
PAIRS_B = (1, 2, 3, 3, 2, 2)
N_CLASSES = N_GROUPS * len(PAIRS_A)
CLASS_ROWS = 32
MOE_ROWS = 128
OUT_PROJ_TILES = 3
MAX_SHIFT_LAG = 60.0
MAX_SHIFT_RAISE = 120.0

NT_DIMS = (((1,), (1,)), ((), ()))


def _params(semantics):
    return pltpu.CompilerParams(dimension_semantics=semantics, vmem_limit_bytes=VMEM_LIMIT)


def _ada_kernel(c_ref, w_ref, b_ref, o_ref):
    c = c_ref[...]
    o_ref[0] = jnp.dot(c * jax.nn.sigmoid(c), w_ref[0], preferred_element_type=F32) + b_ref[0]


def _ada_modulation(c_rows, ada_w, ada_b):
    depth, d, n = ada_w.shape
    tn = 1536
    return pl.pallas_call(
        _ada_kernel,
        out_shape=jax.ShapeDtypeStruct((depth, SUBLANES, n), F32),
        grid=(depth, n // tn),
        in_specs=[
            pl.BlockSpec((SUBLANES, d), lambda l, j: (0, 0)),
            pl.BlockSpec((1, d, tn), lambda l, j: (l, 0, j)),
            pl.BlockSpec((1, 1, tn), lambda l, j: (l, 0, j)),
        ],
        out_specs=pl.BlockSpec((1, SUBLANES, tn), lambda l, j: (l, 0, j)),
        compiler_params=_params(("arbitrary", "arbitrary")),
        name="ada_modulation",
    )(c_rows, ada_w, ada_b.reshape(depth, 1, n))


def _rotate_pairs(v, cos, sin_next, sin_prev):
    return v * cos + pltpu.roll(v, LANES - 1, 1) * sin_next + pltpu.roll(v, 1, 1) * sin_prev


def _in_proj_kernel(x_ref, mod_ref, w_ref, gain_ref, seg_ref, rope_g_ref, rope_d_ref,
                    zl_ref, q_ref, k_ref, vt_ref, dq_ref, dk_ref, dvt_ref):
    tm = x_ref.shape[0]
    x = x_ref[...]
    shift = mod_ref[0, 0:1, :]
    scale = mod_ref[0, 1:2, :]
    h = (x * (1.0 + scale) + shift).astype(BF16)
    z = jnp.dot(h, w_ref[...], preferred_element_type=F32)
    zl_ref[0] = z[:, COL_AX:COL_GQ]

    ones_rows = jnp.where(lax.broadcasted_iota(I32, (VT_ROWS - GQA_HEAD_DIM, tm), 0) == 0, 1.0, 0.0).astype(BF16)

    seg = seg_ref[...]
    cos, s_next, s_prev = rope_g_ref[0], rope_g_ref[1], rope_g_ref[2]
    for g in range(5):
        v = z[:, COL_GQ + g * LANES:COL_GQ + (g + 1) * LANES]
        sq = v * v
        sq_hi = sq.astype(BF16)
        sq_lo = (sq - sq_hi.astype(F32)).astype(BF16)
        ssq = jnp.dot(sq_hi, seg, preferred_element_type=F32) + jnp.dot(sq_lo, seg, preferred_element_type=F32)
        v = v * lax.rsqrt(ssq * (1.0 / GQA_HEAD_DIM) + EPS) * gain_ref[g:g + 1, :]
        v = _rotate_pairs(v, cos, s_next, s_prev)
        if g < 4:
            v = (v * (GQA_HEAD_DIM ** -0.5 * LOG2_E)).astype(BF16)
            q_ref[0, 2 * g] = v[:, :GQA_HEAD_DIM]
            q_ref[0, 2 * g + 1] = v[:, GQA_HEAD_DIM:]
        else:
            v = v.astype(BF16)
            k_ref[0, 0] = v[:, :GQA_HEAD_DIM]
            k_ref[0, 1] = v[:, GQA_HEAD_DIM:]
    vt = z[:, COL_GV:COL_DQ].T.astype(BF16)
    for hd in range(GQA_KV_HEADS):
        vt_ref[0, hd, 0, 0:GQA_HEAD_DIM, :] = vt[hd * GQA_HEAD_DIM:(hd + 1) * GQA_HEAD_DIM]
        vt_ref[0, hd, 0, GQA_HEAD_DIM:VT_ROWS, :] = ones_rows

    cos, s_next, s_prev = rope_d_ref[0], rope_d_ref[1], rope_d_ref[2]
    for g in range(4):
        v = _rotate_pairs(z[:, COL_DQ + g * LANES:COL_DQ + (g + 1) * LANES], cos, s_next, s_prev)
        if g < 2:
            v = v * (DIFF_QK_DIM ** -0.5 * LOG2_E)
        v = v.astype(BF16)
        dst = dq_ref if g < 2 else dk_ref
        for j in range(4):
            dst[0, (g % 2) * 4 + j] = v[:, j * DIFF_QK_DIM:(j + 1) * DIFF_QK_DIM]
    dvt = z[:, COL_DV:IN_WIDTH].T.astype(BF16)
    for hd in range(DIFF_HEADS):
        dvt_ref[0, hd, 0, 0:DIFF_V_DIM, :] = dvt[hd * DIFF_V_DIM:(hd + 1) * DIFF_V_DIM]
        dvt_ref[0, hd, 0, DIFF_V_DIM:VT_ROWS, :] = ones_rows


def _in_proj(xc, bsz, sk, mod, w_in, gains, seg, rope_g, rope_d, n_ctx_tiles):
    d = xc.shape[1]
    tm = TOKEN_TILE
    nt = sk // tm
    n_mod_rows = mod.shape[0]

    def mod_map(b, i):
        return (jnp.where(i < n_ctx_tiles, bsz, b), 0, 0)

    assert n_mod_rows > bsz
    return pl.pallas_call(
        _in_proj_kernel,
        out_shape=(
            jax.ShapeDtypeStruct((bsz, sk, COL_GQ), F32),
            jax.ShapeDtypeStruct((bsz, GQA_HEADS, sk, GQA_HEAD_DIM), BF16),
            jax.ShapeDtypeStruct((bsz, GQA_KV_HEADS, sk, GQA_HEAD_DIM), BF16),
            jax.ShapeDtypeStruct((bsz, GQA_KV_HEADS, nt, VT_ROWS, tm), BF16),
            jax.ShapeDtypeStruct((bsz, 2 * DIFF_HEADS, sk, DIFF_QK_DIM), BF16),
            jax.ShapeDtypeStruct((bsz, 2 * DIFF_HEADS, sk, DIFF_QK_DIM), BF16),
            jax.ShapeDtypeStruct((bsz, DIFF_HEADS, nt, VT_ROWS, tm), BF16),
        ),
        grid=(bsz, nt),
        in_specs=[
            pl.BlockSpec((tm, d), lambda b, i: (b * nt + i, 0)),
            pl.BlockSpec((1, N_MOD, d), mod_map),
            pl.BlockSpec((d, IN_WIDTH), lambda b, i: (0, 0)),
            pl.BlockSpec((SUBLANES, LANES), lambda b, i: (0, 0)),
            pl.BlockSpec((LANES, LANES), lambda b, i: (0, 0)),
            pl.BlockSpec((3, tm, LANES), lambda b, i: (0, i, 0)),
            pl.BlockSpec((3, tm, LANES), lambda b, i: (0, i, 0)),
        ],
        out_specs=(
            pl.BlockSpec((1, tm, COL_GQ), lambda b, i: (b, i, 0)),
            pl.BlockSpec((1, GQA_HEADS, tm, GQA_HEAD_DIM), lambda b, i: (b, 0, i, 0)),
            pl.BlockSpec((1, GQA_KV_HEADS, tm, GQA_HEAD_DIM), lambda b, i: (b, 0, i, 0)),
            pl.BlockSpec((1, GQA_KV_HEADS, 1, VT_ROWS, tm), lambda b, i: (b, 0, i, 0, 0)),
            pl.BlockSpec((1, 2 * DIFF_HEADS, tm, DIFF_QK_DIM), lambda b, i: (b, 0, i, 0)),
            pl.BlockSpec((1, 2 * DIFF_HEADS, tm, DIFF_QK_DIM), lambda b, i: (b, 0, i, 0)),
            pl.BlockSpec((1, DIFF_HEADS, 1, VT_ROWS, tm), lambda b, i: (b, 0, i, 0, 0)),
        ),
        compiler_params=_params(("arbitrary", "arbitrary")),
        name="in_proj",
    )(xc, mod, w_in, gains, seg, rope_g, rope_d)


def _expm1(x):
    u = jnp.exp(x)
    return jnp.where(u == 1.0, x, (u - 1.0) * x / jnp.log(u))


def _lru_kernel(ax_ref, prev_ref, next_ref, cw_ref, cb_ref, w_ref, b_ref, lam_ref, h_ref, carry_ref, *, nt):
    tm = ax_ref.shape[1]
    d = pl.program_id(1)
    j = pl.program_id(2)
    blk = jnp.where(d == 0, j, jnp.where(j == 0, 0, nt - j))

    @pl.when(j == 0)
    def _():
        carry_ref[...] = jnp.zeros_like(carry_ref)

    x = ax_ref[0]
    no_left = jnp.logical_or(blk == 0, blk == 1)
    no_right = jnp.logical_or(blk == 0, blk == nt - 1)
    left = jnp.where(no_left, 0.0, prev_ref[0, SUBLANES - 1:SUBLANES, :])
    right0 = jnp.where(no_right, 0.0, next_ref[0, 0:1, :])
    right1 = jnp.where(no_right, 0.0, next_ref[0, 1:2, :])
    row = lax.broadcasted_iota(I32, (tm, 1), 0)
    x_m1 = jnp.where(row == 0, left, pltpu.roll(x, 1, 0))
    x_p1 = jnp.where(row == tm - 1, right0, pltpu.roll(x, tm - 1, 0))
    x_p2 = jnp.where(row == tm - 2, right0, jnp.where(row == tm - 1, right1, pltpu.roll(x, tm - 2, 0)))
    u = cb_ref[...] + x_m1 * cw_ref[0:1, :] + x * cw_ref[1:2, :] + x_p1 * cw_ref[2:3, :] + x_p2 * cw_ref[3:4, :]

    g = jnp.dot(u.astype(BF16), w_ref[0], preferred_element_type=F32) + b_ref[0]
    r = jax.nn.sigmoid(g[:, :LRU_WIDTH])
    gate_in = jax.nn.sigmoid(g[:, LRU_WIDTH:])
    neg_lam = -lam_ref[0]
    softplus = jnp.maximum(neg_lam, 0.0) + jnp.log1p(jnp.exp(-jnp.abs(neg_lam)))
    log_a = (-RG_C * r) * softplus
    a = jnp.exp(log_a)
    b = jnp.sqrt(-_expm1(2.0 * log_a)) * (gate_in * u)

    def scan(reverse):
        aa, bb = a, b
        in_group = row % SUBLANES
        s = 1
        while s < SUBLANES:
            shift = tm - s if reverse else s
            keep = (in_group < SUBLANES - s) if reverse else (in_group >= s)
            a_sh = jnp.where(keep, pltpu.roll(aa, shift, 0), 1.0)
            b_sh = jnp.where(keep, pltpu.roll(bb, shift, 0), 0.0)
            bb = aa * b_sh + bb
            aa = aa * a_sh
            s *= 2
        state = carry_ref[...]
        n_groups = tm // SUBLANES
        for g in (range(n_groups - 1, -1, -1) if reverse else range(n_groups)):
            lo = g * SUBLANES
            h = aa[lo:lo + SUBLANES] * state + bb[lo:lo + SUBLANES]
            h_ref[0, 0, lo:lo + SUBLANES, :] = h
            state = h[0:1, :] if reverse else h[SUBLANES - 1:SUBLANES, :]
        carry_ref[...] = state

    @pl.when(d == 0)
    def _():
        scan(False)

    @pl.when(d == 1)
    def _():
        scan(True)


def _lru(zl, conv_w, conv_b, w_gates, b_gates, lam):
    bsz, sk, _ = zl.shape
    tm = TOKEN_TILE
    nt = sk // tm
    per = tm // SUBLANES

    def blk_of(d, j):
        return jnp.where(d == 0, j, jnp.where(j == 0, 0, nt - j))

    return pl.pallas_call(
        functools.partial(_lru_kernel, nt=nt),
        out_shape=jax.ShapeDtypeStruct((2, bsz, sk, LRU_WIDTH), F32),
        grid=(bsz, 2, nt),
        in_specs=[
            pl.BlockSpec((1, tm, LRU_WIDTH), lambda b, d, j: (b, blk_of(d, j), 0)),
            pl.BlockSpec((1, SUBLANES, LRU_WIDTH), lambda b, d, j: (b, jnp.maximum(blk_of(d, j) * per - 1, 0), 0)),
            pl.BlockSpec((1, SUBLANES, LRU_WIDTH),
                         lambda b, d, j: (b, jnp.minimum((blk_of(d, j) + 1) * per, nt * per - 1), 0)),
            pl.BlockSpec((CONV_W, LRU_WIDTH), lambda b, d, j: (0, 0)),
            pl.BlockSpec((1, LRU_WIDTH), lambda b, d, j: (0, 0)),
            pl.BlockSpec((1, LRU_WIDTH, 2 * LRU_WIDTH), lambda b, d, j: (d, 0, 0)),
            pl.BlockSpec((1, 1, 2 * LRU_WIDTH), lambda b, d, j: (d, 0, 0)),
            pl.BlockSpec((1, 1, LRU_WIDTH), lambda b, d, j: (d, 0, 0)),
        ],
        out_specs=pl.BlockSpec((1, 1, tm, LRU_WIDTH), lambda b, d, j: (d, b, blk_of(d, j), 0)),
        scratch_shapes=[pltpu.VMEM((1, LRU_WIDTH), F32)],
        compiler_params=_params(("arbitrary", "arbitrary", "arbitrary")),
        name="rg_lru",
    )(zl, zl, zl, conv_w, conv_b, w_gates, b_gates, lam)


def _attn_kernel(q_ref, k_ref, vt_ref, aux_ref, o_ref, m_ref, acc_ref, *, k_heads, v_heads, nt, n_ctx_tiles,
                 diff):
    tq = q_ref.shape[2]
    tk = tq
    dv = GQA_HEAD_DIM
    qi = pl.program_id(2)

    def scores(kt):
        return jnp.concatenate(
            [lax.dot_general(k_ref[0, k_heads[j], kt * tk:(kt + 1) * tk, :], q_ref[0, j], NT_DIMS,
                             preferred_element_type=F32) for j in range(4)], axis=1)

    def weighted_values(kt, p):
        return jnp.concatenate(
            [jnp.dot(vt_ref[0, v_heads[j], kt], p[:, j * tq:(j + 1) * tq], preferred_element_type=F32)
             for j in range(4)], axis=1)

    def finalize(acc_all):
        outs = []
        for j in range(4):
            acc = acc_all[:, j * tq:(j + 1) * tq]
            outs.append(acc[0:dv] / acc[dv:dv + 1])
        if diff:
            dl = aux_ref[0:4, :]
            lam_init = aux_ref[4:5, 0:1]
            lam = (jnp.exp(jnp.sum(dl[0:1] * dl[1:2], axis=1, keepdims=True))
                   - jnp.exp(jnp.sum(dl[2:3] * dl[3:4], axis=1, keepdims=True)) + lam_init)
            heads = []
            for hd in range(2):
                o = outs[2 * hd] - lam * outs[2 * hd + 1]
                heads.append(o * lax.rsqrt(jnp.mean(o * o, axis=0, keepdims=True) + EPS))
            o_ref[0] = ((jnp.concatenate(heads, axis=0).T * aux_ref[5:6, :]) * (1.0 - lam_init)).astype(o_ref.dtype)
        else:
            for pair in range(2):
                o_ref[0, :, pair * 2 * dv:(pair + 1) * 2 * dv] = (
                    jnp.concatenate(outs[2 * pair:2 * pair + 2], axis=0).T.astype(o_ref.dtype))

    s_first = scores(0)
    m_first = jnp.max(s_first, axis=0, keepdims=True)

    @pl.when(qi < n_ctx_tiles)
    def _():
        finalize(weighted_values(0, jnp.exp2(s_first - m_first).astype(BF16)))

    @pl.when(qi >= n_ctx_tiles)
    def _():
        m_ref[...] = m_first

        def attempt(again):
            m0 = m_ref[...]
            s_cur = s_first
            peak = jnp.zeros((1, 4 * tq), BF16)
            acc = None
            for u in range(nt):
                s_next = scores(u + 1) if u + 1 < nt else None
                p = jnp.exp2(s_cur - m0).astype(BF16)
                peak = jnp.maximum(peak, jnp.max(p, axis=0, keepdims=True))
                pv = weighted_values(u, p)
                acc = pv if acc is None else acc + pv
                s_cur = s_next
            acc_ref[...] = acc
            peak = peak.astype(F32)
            lagging = jnp.max(peak) > 2.0 ** MAX_SHIFT_LAG

            @pl.when(lagging)
            def _():
                m_ref[...] = m0 + jnp.clip(jnp.log(peak) * LOG2_E, 0.0, MAX_SHIFT_RAISE)

            return lagging

        lax.while_loop(lambda again: again, attempt, True)
        finalize(acc_ref[...])


def _attention(q, k, vt, aux, *, diff, n_ctx_tiles):
    bsz, n_q, sk, dh = q.shape
    n_k_heads = k.shape[1]
    n_v_heads = vt.shape[1]
    tq = TOKEN_TILE
    nt = sk // tq
    assert n_ctx_tiles == 1
    groups = n_q // 4
    kpg = n_k_heads // groups
    vpg = n_v_heads // groups
    k_heads = tuple(j * kpg // 4 for j in range(4))
    v_heads = tuple(j * vpg // 4 for j in range(4))
    width = 2 * DIFF_V_DIM if diff else 4 * GQA_HEAD_DIM
    return pl.pallas_call(
        functools.partial(_attn_kernel, k_heads=k_heads, v_heads=v_heads, nt=nt, n_ctx_tiles=n_ctx_tiles, diff=diff),
        out_shape=jax.ShapeDtypeStruct((bsz, sk, groups * width), BF16),
        grid=(bsz, groups, nt),
        in_specs=[
            pl.BlockSpec((1, 4, tq, dh), lambda b, g, i: (b, g, i, 0)),
            pl.BlockSpec((1, kpg, sk, dh), lambda b, g, i: (b, g, 0, 0)),
            pl.BlockSpec((1, vpg, nt, VT_ROWS, tq), lambda b, g, i: (b, g, 0, 0, 0)),
            pl.BlockSpec((SUBLANES, LANES), lambda b, g, i: (0, 0)),
        ],
        out_specs=pl.BlockSpec((1, tq, width), lambda b, g, i: (b, i, g)),
        scratch_shapes=[pltpu.VMEM((1, 4 * tq), F32), pltpu.VMEM((VT_ROWS, 4 * tq), F32)],
        compiler_params=_params(("arbitrary", "arbitrary", "arbitrary")),
        name="diff_attention" if diff else "gqa_attention",
    )(q, k, vt, aux)


def _layer_norm(r, g, b):
    mu = jnp.mean(r, axis=-1, keepdims=True)
    c = r - mu
    var = jnp.mean(c * c, axis=-1, keepdims=True)
    return c * lax.rsqrt(var + EPS) * g + b


def _route(logits, bias):
    mx = jnp.max(logits, axis=0, keepdims=True)
    e = jnp.exp(logits - mx)
    probs = e / jnp.sum(e, axis=0, keepdims=True)
    sel = probs + bias
    epg = EXPERTS_PER_GROUP
    in_top2, scores = [], []
    for g in range(N_GROUPS):
        v = [sel[g * epg + i:g * epg + i + 1] for i in range(epg)]
        masks = []
        for i in range(epg):
            rank = jnp.zeros_like(v[i])
            for j in range(epg):
                if j != i:
                    beats = (v[j] > v[i]) if j > i else (v[j] >= v[i])
                    rank = rank + beats.astype(F32)
            masks.append(rank < 2.0)
        in_top2.append(masks)
        scores.append(sum(jnp.where(masks[i], v[i], 0.0) for i in range(epg)))
    chosen = []
    for g in range(N_GROUPS):
        c = None
        for j in range(N_GROUPS):
            if j != g:
                t = (scores[j] < scores[g]) if j < g else (scores[j] <= scores[g])
                c = t if c is None else jnp.logical_and(c, t)
        chosen.append(c)
    picked, weight = [], []
    for i in range(epg):
        m = None
        w = jnp.zeros_like(scores[0])
        for g in range(N_GROUPS):
            t = jnp.logical_and(chosen[g], in_top2[g][i])
            m = t if m is None else jnp.logical_or(m, t)
            w = w + jnp.where(t, probs[g * epg + i:g * epg + i + 1], 0.0)
        picked.append(m)
        weight.append(w)
    total = weight[0] + weight[1] + weight[2] + weight[3]
    gate = [w / total for w in weight]
    lo = jnp.where(picked[0], 0.0, jnp.where(picked[1], 1.0, 2.0))
    hi = jnp.where(picked[3], 3.0, jnp.where(picked[2], 2.0, 1.0))
    gate_lo = jnp.where(picked[0], gate[0], jnp.where(picked[1], gate[1], gate[2]))
    gate_hi = jnp.where(picked[3], gate[3], jnp.where(picked[2], gate[2], gate[1]))
    group = sum(jnp.where(chosen[g], float(g), 0.0) for g in range(N_GROUPS))
    base = jnp.where(lo == 0.0, 0.0, jnp.where(lo == 1.0, 3.0, 5.0))
    natural = base + hi - lo - 1.0
    pair = jnp.where(natural == 3.0, 4.0, jnp.where(natural == 4.0, 3.0, natural))
    swapped = natural == 5.0
    cls = group * float(len(PAIRS_A)) + pair
    return cls.astype(I32), jnp.where(swapped, gate_hi, gate_lo), jnp.where(swapped, gate_lo, gate_hi)


def _out_proj_kernel(hf_ref, hb_ref, ag_ref, yb_ref, yc_ref, x_ref, mod_ref, w_ref, lng_ref, lnb_ref,
                     rw_ref, rb_ref, x1_ref, ri_ref, cnt_ref, run_ref, *, alpha, n_ctx_tiles, n_batch, sub_tiles):
    tm = x_ref.shape[0] // sub_tiles
    d = x_ref.shape[1]
    n_a = LRU_WIDTH
    n_b = n_a + GQA_HEADS * GQA_HEAD_DIM
    b = pl.program_id(0)
    i = pl.program_id(1)

    @pl.when(jnp.logical_and(b == 0, i == 0))
    def _():
        run_ref[...] = jnp.zeros_like(run_ref)

    before = (lax.broadcasted_iota(I32, (tm, tm), 0) < lax.broadcasted_iota(I32, (tm, tm), 1)).astype(BF16)
    run = run_ref[...]
    logit_parts, mod_rows = [], []
    for sub in range(sub_tiles):
        rows = slice(sub * tm, (sub + 1) * tm)
        mod_row = jnp.where(i * sub_tiles + sub < n_ctx_tiles, n_batch, b)
        ya = (jax.nn.gelu(ag_ref[0, rows, :]) * (hf_ref[0, 0, rows, :] + hb_ref[0, 0, rows, :])).astype(BF16)
        y = (jnp.dot(ya, w_ref[0:n_a, :], preferred_element_type=F32)
             + jnp.dot(yb_ref[0, rows, :], w_ref[n_a:n_b, :], preferred_element_type=F32)
             + jnp.dot(yc_ref[0, rows, :], w_ref[n_b:, :], preferred_element_type=F32))
        x1 = _layer_norm(alpha * x_ref[rows, :] + mod_ref[mod_row, 2:3, :] * y, lng_ref[...], lnb_ref[...])
        x1_ref[rows, 0:d] = x1

        h2 = x1 * (1.0 + mod_ref[mod_row, 4:5, :]) + mod_ref[mod_row, 3:4, :]
        h_hi = h2.astype(BF16)
        h_lo = (h2 - h_hi.astype(F32)).astype(BF16)
        both = lax.dot_general(rw_ref[...], h_hi, NT_DIMS, preferred_element_type=F32)
        logit_parts.append(both[0:N_EXPERTS] + both[N_EXPERTS:]
                           + lax.dot_general(rw_ref[0:N_EXPERTS, :], h_lo, NT_DIMS, preferred_element_type=F32))
        mod_rows.append(mod_row)

    cls, gate_a, gate_b = _route(jnp.concatenate(logit_parts, axis=1), rb_ref[...])
    member = lax.broadcasted_iota(I32, (CLASS_ROWS, sub_tiles * tm), 0) == cls
    for sub in range(sub_tiles):
        rows = slice(sub * tm, (sub + 1) * tm)
        extra = jnp.concatenate([gate_a[:, rows], gate_b[:, rows], jnp.full((1, tm), mod_rows[sub].astype(F32), F32),
                                 jnp.zeros((LANES - 3, tm), F32)], axis=0)
        x1_ref[rows, d:] = extra.T

        in_class = member[:, rows]
        earlier = jnp.dot(in_class.astype(BF16), before, preferred_element_type=F32)
        rank = jnp.sum(jnp.where(in_class, earlier + run, 0.0), axis=0, keepdims=True).astype(I32)
        run = run + jnp.sum(in_class.astype(F32), axis=1, keepdims=True)
        ri_ref[0, :, rows] = jnp.concatenate([cls[:, rows], rank, jnp.zeros((SUBLANES - 2, tm), I32)], axis=0)
    run_ref[...] = run
    cnt_ref[...] = jnp.broadcast_to(run, cnt_ref.shape).astype(I32)


def _out_proj(h_lru, zl, yb, yc, xc, mod, w_out, ln_g, ln_b, rw_split, rb, alpha, n_ctx_tiles):
    bsz, sk, _ = zl.shape
    d = xc.shape[1]
    nt = sk // TOKEN_TILE
    sub_tiles = OUT_PROJ_TILES if nt % OUT_PROJ_TILES == 0 else 1
    tm = sub_tiles * TOKEN_TILE
    nt = nt // sub_tiles

    return pl.pallas_call(
        functools.partial(_out_proj_kernel, alpha=alpha, n_ctx_tiles=n_ctx_tiles, n_batch=bsz,
                          sub_tiles=sub_tiles),
        out_shape=(
            jax.ShapeDtypeStruct((bsz * sk, d + LANES), F32),
            jax.ShapeDtypeStruct((bsz, SUBLANES, sk), I32),
            jax.ShapeDtypeStruct((CLASS_ROWS, LANES), I32),
        ),
        grid=(bsz, nt),
        in_specs=[
            pl.BlockSpec((1, 1, tm, LRU_WIDTH), lambda b, i: (0, b, i, 0)),
            pl.BlockSpec((1, 1, tm, LRU_WIDTH), lambda b, i: (1, b, i, 0)),
            pl.BlockSpec((1, tm, LRU_WIDTH), lambda b, i: (b, i, 1)),
            pl.BlockSpec((1, tm, GQA_HEADS * GQA_HEAD_DIM), lambda b, i: (b, i, 0)),
            pl.BlockSpec((1, tm, DIFF_HEADS * DIFF_V_DIM), lambda b, i: (b, i, 0)),
            pl.BlockSpec((tm, d), lambda b, i: (b * nt + i, 0)),
            pl.BlockSpec(mod.shape, lambda b, i: (0, 0, 0)),
            pl.BlockSpec((d, d), lambda b, i: (0, 0)),
            pl.BlockSpec((1, d), lambda b, i: (0, 0)),
            pl.BlockSpec((1, d), lambda b, i: (0, 0)),
            pl.BlockSpec((2 * N_EXPERTS, d), lambda b, i: (0, 0)),
            pl.BlockSpec((N_EXPERTS, 1), lambda b, i: (0, 0)),
        ],
        out_specs=(
            pl.BlockSpec((tm, d + LANES), lambda b, i: (b * nt + i, 0)),
            pl.BlockSpec((1, SUBLANES, tm), lambda b, i: (b, 0, i)),
            pl.BlockSpec((CLASS_ROWS, LANES), lambda b, i: (0, 0)),
        ),
        scratch_shapes=[pltpu.VMEM((CLASS_ROWS, 1), F32)],
        compiler_params=_params(("arbitrary", "arbitrary")),
        name="out_proj_router",
    )(h_lru, h_lru, zl, yb, yc, xc, mod, w_out, ln_g, ln_b, rw_split, rb)


def _moe_kernel(cls_ref, rank_ref, pstart_ref, ea_ref, eb_ref, nv_ref, nu_ref,
                x_hbm, mod_ref, lng_ref, lnb_ref,
                wga_ref, wua_ref, wda_ref, wgb_ref, wub_ref, wdb_ref,
                o_hbm, tok_ref, xbuf, obuf, gsem, ssem, *, alpha, n_batch, n_tok):
    rows = xbuf.shape[1]
    d = obuf.shape[2]
    i = pl.program_id(0)
    n_steps = pl.num_programs(0)
    n_used = nu_ref[0]
    slot = i % 2

    @pl.when(i == 0)
    def _():
        def clear(r, c):
            tok_ref[r] = 0
            return c
        lax.fori_loop(0, tok_ref.shape[0], clear, 0, unroll=8)

        def place(t, c):
            tok_ref[pstart_ref[cls_ref[t]] + rank_ref[t]] = t
            return c
        lax.fori_loop(0, n_tok, place, 0, unroll=8)

        obuf[...] = jnp.zeros_like(obuf)

    def start_gather(step, slot_):
        for r in range(rows):
            pltpu.make_async_copy(x_hbm.at[pl.ds(tok_ref[step * rows + r], 1)], xbuf.at[slot_, pl.ds(r, 1)],
                                  gsem.at[slot_]).start()

    def wait_gather(slot_):
        pltpu.make_async_copy(x_hbm.at[pl.ds(0, rows)], xbuf.at[slot_], gsem.at[slot_]).wait()

    def start_scatter(step, n_valid, slot_):
        for r in range(rows):
            dst = jnp.where(r < n_valid, tok_ref[step * rows + r], n_tok + slot_ * rows + r)
            pltpu.make_async_copy(obuf.at[slot_, pl.ds(r, 1)], o_hbm.at[pl.ds(dst, 1)], ssem.at[slot_]).start()

    def wait_scatter(slot_):
        pltpu.make_async_copy(obuf.at[slot_], o_hbm.at[pl.ds(0, rows)], ssem.at[slot_]).wait()

    def block_step(slot):
        wait_gather(slot)
        start_gather(jnp.minimum(i + 1, n_used - 1), 1 - slot)
        prev = jnp.maximum(i - 1, 0)
        start_scatter(prev, jnp.where(i >= 1, nv_ref[prev], 0), 1 - slot)

        x = xbuf[slot, :, 0:d]
        gate_a = xbuf[slot, :, d:d + 1]
        gate_b = xbuf[slot, :, d + 1:d + 2]
        mid = xbuf[slot, :, d + 2:d + 3]

        def pick(kk):
            v = mod_ref[n_batch, kk:kk + 1, :]
            for b in range(n_batch):
                v = jnp.where(mid == float(b), mod_ref[b, kk:kk + 1, :], v)
            return v

        h = x * (1.0 + pick(4)) + pick(3)

        def ffn(wg, wu, wd):
            a = jnp.dot(h, wg[0], preferred_element_type=F32)
            u = jnp.dot(h, wu[0], preferred_element_type=F32)
            return jnp.dot((a * jax.nn.sigmoid(a)) * u, wd[0], preferred_element_type=F32)

        f = ffn(wga_ref, wua_ref, wda_ref) * gate_a + ffn(wgb_ref, wub_ref, wdb_ref) * gate_b
        y = _layer_norm(alpha * x + pick(5) * f, lng_ref[...], lnb_ref[...])
        wait_scatter(slot)
        obuf[slot] = y

    @pl.when(i == 0)
    def _():
        pltpu.make_async_copy(obuf.at[0], o_hbm.at[pl.ds(n_tok, rows)], ssem.at[0]).start()
        start_gather(0, 0)

    for parity in range(2):
        @pl.when(jnp.logical_and(i < n_used, i % 2 == parity))
        def _(parity=parity):
            block_step(parity)

    @pl.when(i == n_steps - 1)
    def _():
        last = n_used - 1
        q = last % 2
        start_scatter(last, nv_ref[last], q)
        wait_scatter(q)
        wait_scatter(1 - q)
        wait_gather(1 - q)


def _moe(x_wide, n_tok, cls, rank, pstart, blk_ea, blk_eb, blk_nv, n_used, mod, ln_g, ln_b,
         w_gate, w_up, w_down, layer, alpha, n_batch):
    d = x_wide.shape[1] - LANES
    rows = MOE_ROWS
    n_blk = blk_ea.shape[0]
    ff = w_gate.shape[-1]
    first = layer * N_EXPERTS

    def wa(i, cl, rk, ps, ea, eb, nv, nu):
        return (first + ea[i], 0, 0)

    def wb(i, cl, rk, ps, ea, eb, nv, nu):
        return (first + eb[i], 0, 0)

    const2 = lambda i, cl, rk, ps, ea, eb, nv, nu: (0, 0)
    grid_spec = pltpu.PrefetchScalarGridSpec(
        num_scalar_prefetch=7,
        grid=(n_blk,),
        in_specs=[
            pl.BlockSpec(memory_space=pl.ANY),
            pl.BlockSpec(mod.shape, lambda i, cl, rk, ps, ea, eb, nv, nu: (0, 0, 0)),
            pl.BlockSpec((1, d), const2),
            pl.BlockSpec((1, d), const2),
            pl.BlockSpec((1, d, ff), wa),
            pl.BlockSpec((1, d, ff), wa),
            pl.BlockSpec((1, ff, d), wa),
            pl.BlockSpec((1, d, ff), wb),
            pl.BlockSpec((1, d, ff), wb),
            pl.BlockSpec((1, ff, d), wb),
        ],
        out_specs=pl.BlockSpec(memory_space=pl.ANY),
        scratch_shapes=[
            pltpu.SMEM((n_blk * rows,), I32),
            pltpu.VMEM((2, rows, d + LANES), F32),
            pltpu.VMEM((2, rows, d), F32),
            pltpu.SemaphoreType.DMA((2,)),
            pltpu.SemaphoreType.DMA((2,)),
        ],
    )
    return pl.pallas_call(
        functools.partial(_moe_kernel, alpha=alpha, n_batch=n_batch, n_tok=n_tok),
        out_shape=jax.ShapeDtypeStruct((n_tok + 2 * rows, d), F32),
        grid_spec=grid_spec,
        compiler_params=_params(("arbitrary",)),
        name="moe_experts",
    )(cls, rank, pstart, blk_ea, blk_eb, blk_nv, n_used, x_wide, mod, ln_g, ln_b,
      w_gate, w_up, w_down, w_gate, w_up, w_down)


def _block_tables(counts, n_tok):
    rows = MOE_ROWS
    n_blk = n_tok // rows + N_CLASSES
    n_pairs = len(PAIRS_A)
    padded = (counts + rows - 1) // rows * rows
    pend = jnp.cumsum(padded)
    pstart = pend - padded
    n_used = pend[-1] // rows
    blk = jnp.arange(n_blk, dtype=I32)
    blk_ids = jnp.minimum(blk, jnp.maximum(n_used - 1, 0))
    blk_cls = jnp.minimum(jnp.sum((blk_ids[:, None] * rows >= pend[None, :]).astype(I32), axis=1), N_CLASSES - 1)
    onehot = blk_cls[:, None] == jnp.arange(N_CLASSES, dtype=I32)[None, :]
    within = blk_ids * rows - jnp.sum(jnp.where(onehot, pstart[None, :], 0), axis=1)
    count_b = jnp.sum(jnp.where(onehot, counts[None, :], 0), axis=1)
    blk_nv = jnp.where(blk < n_used, jnp.clip(count_b - within, 0, rows), 0).astype(I32)
    pair = blk_cls % n_pairs
    slot_a = sum(jnp.where(pair == p, PAIRS_A[p], 0) for p in range(n_pairs))
    slot_b = sum(jnp.where(pair == p, PAIRS_B[p], 0) for p in range(n_pairs))
    first = blk_cls // n_pairs * EXPERTS_PER_GROUP
    pstart_pad = jnp.concatenate([pstart, jnp.zeros((CLASS_ROWS - N_CLASSES,), I32)]).astype(I32)
    return (pstart_pad, (first + slot_a).astype(I32), (first + slot_b).astype(I32), blk_nv,
            n_used.reshape(1).astype(I32))


def _rope_tables(n_lat, n_ctx, dim):
    rows = n_lat // GRID_W
    row = jnp.repeat(jnp.arange(rows, dtype=F32), GRID_W)
    col = jnp.tile(jnp.arange(GRID_W, dtype=F32), rows)
    n_freq = dim // 4
    inv = ROPE_THETA ** (-jnp.arange(n_freq, dtype=F32) / n_freq)
    ang = jnp.concatenate([row[:, None] * inv, col[:, None] * inv], axis=-1)
    cos = jnp.repeat(jnp.cos(ang), 2, axis=-1)
    sin = jnp.repeat(jnp.sin(ang), 2, axis=-1)
    even = (jnp.arange(dim) % 2 == 0)
    tabs = jnp.stack([cos, jnp.where(even, -sin, 0.0), jnp.where(even, 0.0, sin)])
    ident = jnp.stack([jnp.ones((n_ctx, dim), F32), jnp.zeros((n_ctx, dim), F32), jnp.zeros((n_ctx, dim), F32)])
    return jnp.tile(jnp.concatenate([ident, tabs], axis=1), (1, 1, LANES // dim))


def _block_diag(w):
    n, c, _ = w.shape
    eye = jnp.eye(n, dtype=w.dtype)
    return (eye[:, None, :, None] * w[:, :, None, :]).reshape(n * c, n * c)


def kernel(x, c, ctx, c_ctx, ada_w, ada_b, w_in, conv_w, conv_b, rg_wa, rg_ba, rg_wx, rg_bx, rg_lam, q_norm_g, k_norm_g, diff_lambda, diff_subln_g, w_out, ln1_g, ln1_b, ln2_g, ln2_b, router_w, router_b, exp_w_gate, exp_w_up, exp_w_down):
    bsz, s, d = x.shape
    n_ctx = ctx.shape[1]
    depth = w_in.shape[0]
    sk = n_ctx + s
    assert d == D_MODEL and n_ctx == TOKEN_TILE and s % TOKEN_TILE == 0 and bsz + 1 <= SUBLANES
    n_ctx_tiles = n_ctx // TOKEN_TILE
    n_tok = bsz * sk
    alpha = (2.0 * depth) ** 0.25

    c_rows = jnp.concatenate([c, c_ctx[None, :], jnp.zeros((SUBLANES - bsz - 1, d), F32)], axis=0)
    mods = _ada_modulation(c_rows, ada_w, ada_b).reshape(depth, SUBLANES, N_MOD, d)

    rope_g = _rope_tables(s, n_ctx, GQA_HEAD_DIM)
    rope_d = _rope_tables(s, n_ctx, DIFF_QK_DIM)
    lane = np.arange(LANES)
    seg = jnp.asarray(lane[:, None] // GQA_HEAD_DIM == lane[None, :] // GQA_HEAD_DIM, BF16)
    rw_t = router_w.T
    rw_hi = rw_t.astype(BF16)
    rw_split = jnp.concatenate([rw_hi, (rw_t - rw_hi.astype(F32)).astype(BF16)], axis=0)
    rb = router_b.reshape(N_EXPERTS, 1)
    zero_aux = jnp.zeros((SUBLANES, LANES), F32)

    xc = jnp.concatenate([part for b in range(bsz) for part in (ctx[b], x[b])], axis=0)
    w_gate_all = exp_w_gate.reshape(depth * N_EXPERTS, d, EXPERT_FF)
    w_up_all = exp_w_up.reshape(depth * N_EXPERTS, d, EXPERT_FF)
    w_down_all = exp_w_down.reshape(depth * N_EXPERTS, EXPERT_FF, d)
    for li in range(depth):
        lam_init = 0.8 - 0.6 * math.exp(-0.3 * li)
        mod = mods[li]
        gains = jnp.concatenate([jnp.tile(jnp.tile(q_norm_g[li], 2)[None, :], (4, 1)),
                                 jnp.tile(k_norm_g[li], 2)[None, :], jnp.zeros((3, LANES), F32)], axis=0)
        zl, q, k, vt, dq, dk, dvt = _in_proj(xc, bsz, sk, mod, w_in[li].astype(BF16), gains, seg, rope_g, rope_d,
                                             n_ctx_tiles)

        w_gates = jnp.stack([jnp.concatenate([_block_diag(rg_wa[li, dd]), _block_diag(rg_wx[li, dd])], axis=1)
                             for dd in range(2)]).astype(BF16)
        b_gates = jnp.concatenate([rg_ba[li], rg_bx[li]], axis=-1)[:, None, :]
        h_lru = _lru(zl, conv_w[li], conv_b[li][None, :], w_gates, b_gates, rg_lam[li][:, None, :])

        yb = _attention(q, k, vt, zero_aux, diff=False, n_ctx_tiles=n_ctx_tiles)
        aux = jnp.concatenate([
            jnp.pad(diff_lambda[li], ((0, 0), (0, LANES - DIFF_QK_DIM))),
            jnp.full((1, LANES), lam_init, F32),
            jnp.tile(diff_subln_g[li], 2)[None, :],
            jnp.zeros((2, LANES), F32)], axis=0)
        yc = _attention(dq, dk, dvt, aux, diff=True, n_ctx_tiles=n_ctx_tiles)

        x1, route_i, counts = _out_proj(h_lru, zl, yb, yc, xc, mod, w_out[li].astype(BF16), ln1_g[li][None, :],
                                        ln1_b[li][None, :], rw_split, rb, alpha, n_ctx_tiles)

        pstart, blk_ea, blk_eb, blk_nv, n_used = _block_tables(counts[:N_CLASSES, 0], n_tok)
        xc = _moe(x1, n_tok, route_i[:, 0, :].reshape(n_tok), route_i[:, 1, :].reshape(n_tok), pstart,
                  blk_ea, blk_eb, blk_nv, n_used, mod, ln2_g[li][None, :], ln2_b[li][None, :],
                  w_gate_all, w_up_all, w_down_all, li, alpha, bsz)
    return jnp.stack([xc[b * sk + n_ctx:(b + 1) * sk] for b in range(bsz)])
```

```python
import functools
import math

import numpy as np
import jax
import jax.numpy as jnp
from jax import lax
from jax.experimental import pallas as pl
from jax.experimental.pallas import tpu as pltpu

F32 = jnp.float32
BF16 = jnp.bfloat16
I32 = jnp.int32

D_MODEL = 1024
GRID_W = 64
LRU_WIDTH = 256
LRU_BLOCKS = 4
CONV_W = 4
RG_C = 8.0
GQA_HEADS = 8
GQA_KV_HEADS = 2
GQA_HEAD_DIM = 64
DIFF_HEADS = 4
DIFF_QK_DIM = 32
DIFF_V_DIM = 64
IN_WIDTH = 2048
N_EXPERTS = 16
N_GROUPS = 4
EXPERTS_PER_GROUP = 4
EXPERT_FF = 512
ROPE_THETA = 10000.0
EPS = 1e-6
N_MOD = 6
LOG2_E = math.log2(math.e)

COL_AX, COL_AG, COL_GQ, COL_GK, COL_GV, COL_DQ, COL_DK, COL_DV = 0, 256, 512, 1024, 1152, 1280, 1536, 1792

LANES = 128
SUBLANES = 8
VMEM_LIMIT = 56 * 1024 * 1024

TOKEN_TILE = 256
VT_ROWS = 80
PAIRS_A = (0, 0, 0, 1, 1, 3)
PAIRS_B = (1, 2, 3, 3, 2, 2)
N_CLASSES = N_GROUPS * len(PAIRS_A)
CLASS_ROWS = 32
MOE_ROWS = 128
OUT_PROJ_TILES = 3
MAX_SHIFT_LAG = 60.0
MAX_SHIFT_RAISE = 120.0

NT_DIMS = (((1,), (1,)), ((), ()))


def _params(semantics):
    return pltpu.CompilerParams(dimension_semantics=semantics, vmem_limit_bytes=VMEM_LIMIT)


def _ada_kernel(c_ref, w_ref, b_ref, o_ref):
    c = c_ref[...]
    o_ref[0] = jnp.dot(c * jax.nn.sigmoid(c), w_ref[0], preferred_element_type=F32) + b_ref[0]


def _ada_modulation(c_rows, ada_w, ada_b):
    depth, d, n = ada_w.shape
    tn = 1536
    return pl.pallas_call(
        _ada_kernel,
        out_shape=jax.ShapeDtypeStruct((depth, SUBLANES, n), F32),
        grid=(depth, n // tn),
        in_specs=[
            pl.BlockSpec((SUBLANES, d), lambda l, j: (0, 0)),
            pl.BlockSpec((1, d, tn), lambda l, j: (l, 0, j)),
            pl.BlockSpec((1, 1, tn), lambda l, j: (l, 0, j)),
        ],
        out_specs=pl.BlockSpec((1, SUBLANES, tn), lambda l, j: (l, 0, j)),
        compiler_params=_params(("arbitrary", "arbitrary")),
        name="ada_modulation",
    )(c_rows, ada_w, ada_b.reshape(depth, 1, n))


def _rotate_pairs(v, cos, sin_next, sin_prev):
    return v * cos + pltpu.roll(v, LANES - 1, 1) * sin_next + pltpu.roll(v, 1, 1) * sin_prev


def _in_proj_kernel(x_ref, mod_ref, w_ref, gain_ref, seg_ref, rope_g_ref, rope_d_ref,
                    zl_ref, q_ref, k_ref, vt_ref, dq_ref, dk_ref, dvt_ref):
    tm = x_ref.shape[0]
    x = x_ref[...]
    shift = mod_ref[0, 0:1, :]
    scale = mod_ref[0, 1:2, :]
    h = (x * (1.0 + scale) + shift).astype(BF16)
    z = jnp.dot(h, w_ref[...], preferred_element_type=F32)
    zl_ref[0] = z[:, COL_AX:COL_GQ]

    ones_rows = jnp.where(lax.broadcasted_iota(I32, (VT_ROWS - GQA_HEAD_DIM, tm), 0) == 0, 1.0, 0.0).astype(BF16)

    seg = seg_ref[...]
    cos, s_next, s_prev = rope_g_ref[0], rope_g_ref[1], rope_g_ref[2]
    for g in range(5):
        v = z[:, COL_GQ + g * LANES:COL_GQ + (g + 1) * LANES]
        sq = v * v
        sq_hi = sq.astype(BF16)
        sq_lo = (sq - sq_hi.astype(F32)).astype(BF16)
        ssq = jnp.dot(sq_hi, seg, preferred_element_type=F32) + jnp.dot(sq_lo, seg, preferred_element_type=F32)
        v = v * lax.rsqrt(ssq * (1.0 / GQA_HEAD_DIM) + EPS) * gain_ref[g:g + 1, :]
        v = _rotate_pairs(v, cos, s_next, s_prev)
        if g < 4:
            v = (v * (GQA_HEAD_DIM ** -0.5 * LOG2_E)).astype(BF16)
            q_ref[0, 2 * g] = v[:, :GQA_HEAD_DIM]
            q_ref[0, 2 * g + 1] = v[:, GQA_HEAD_DIM:]
        else:
            v = v.astype(BF16)
            k_ref[0, 0] = v[:, :GQA_HEAD_DIM]
            k_ref[0, 1] = v[:, GQA_HEAD_DIM:]
    vt = z[:, COL_GV:COL_DQ].T.astype(BF16)
    for hd in range(GQA_KV_HEADS):
        vt_ref[0, hd, 0, 0:GQA_HEAD_DIM, :] = vt[hd * GQA_HEAD_DIM:(hd + 1) * GQA_HEAD_DIM]
        vt_ref[0, hd, 0, GQA_HEAD_DIM:VT_ROWS, :] = ones_rows

    cos, s_next, s_prev = rope_d_ref[0], rope_d_ref[1], rope_d_ref[2]
    for g in range(4):
        v = _rotate_pairs(z[:, COL_DQ + g * LANES:COL_DQ + (g + 1) * LANES], cos, s_next, s_prev)
        if g < 2:
            v = v * (DIFF_QK_DIM ** -0.5 * LOG2_E)
        v = v.astype(BF16)
        dst = dq_ref if g < 2 else dk_ref
        for j in range(4):
            dst[0, (g % 2) * 4 + j] = v[:, j * DIFF_QK_DIM:(j + 1) * DIFF_QK_DIM]
    dvt = z[:, COL_DV:IN_WIDTH].T.astype(BF16)
    for hd in range(DIFF_HEADS):
        dvt_ref[0, hd, 0, 0:DIFF_V_DIM, :] = dvt[hd * DIFF_V_DIM:(hd + 1) * DIFF_V_DIM]
        dvt_ref[0, hd, 0, DIFF_V_DIM:VT_ROWS, :] = ones_rows


def _in_proj(xc, bsz, sk, mod, w_in, gains, seg, rope_g, rope_d, n_ctx_tiles):
    d = xc.shape[1]
    tm = TOKEN_TILE
    nt = sk // tm
    n_mod_rows = mod.shape[0]

    def mod_map(b, i):
        return (jnp.where(i < n_ctx_tiles, bsz, b), 0, 0)

    assert n_mod_rows > bsz
    return pl.pallas_call(
        _in_proj_kernel,
        out_shape=(
            jax.ShapeDtypeStruct((bsz, sk, COL_GQ), F32),
            jax.ShapeDtypeStruct((bsz, GQA_HEADS, sk, GQA_HEAD_DIM), BF16),
            jax.ShapeDtypeStruct((bsz, GQA_KV_HEADS, sk, GQA_HEAD_DIM), BF16),
            jax.ShapeDtypeStruct((bsz, GQA_KV_HEADS, nt, VT_ROWS, tm), BF16),
            jax.ShapeDtypeStruct((bsz, 2 * DIFF_HEADS, sk, DIFF_QK_DIM), BF16),
            jax.ShapeDtypeStruct((bsz, 2 * DIFF_HEADS, sk, DIFF_QK_DIM), BF16),
            jax.ShapeDtypeStruct((bsz, DIFF_HEADS, nt, VT_ROWS, tm), BF16),
        ),
        grid=(bsz, nt),
        in_specs=[
            pl.BlockSpec((tm, d), lambda b, i: (b * nt + i, 0)),
            pl.BlockSpec((1, N_MOD, d), mod_map),
            pl.BlockSpec((d, IN_WIDTH), lambda b, i: (0, 0)),
            pl.BlockSpec((SUBLANES, LANES), lambda b, i: (0, 0)),
            pl.BlockSpec((LANES, LANES), lambda b, i: (0, 0)),
            pl.BlockSpec((3, tm, LANES), lambda b, i: (0, i, 0)),
            pl.BlockSpec((3, tm, LANES), lambda b, i: (0, i, 0)),
        ],
        out_specs=(
            pl.BlockSpec((1, tm, COL_GQ), lambda b, i: (b, i, 0)),
            pl.BlockSpec((1, GQA_HEADS, tm, GQA_HEAD_DIM), lambda b, i: (b, 0, i, 0)),
            pl.BlockSpec((1, GQA_KV_HEADS, tm, GQA_HEAD_DIM), lambda b, i: (b, 0, i, 0)),
            pl.BlockSpec((1, GQA_KV_HEADS, 1, VT_ROWS, tm), lambda b, i: (b, 0, i, 0, 0)),
            pl.BlockSpec((1, 2 * DIFF_HEADS, tm, DIFF_QK_DIM), lambda b, i: (b, 0, i, 0)),
            pl.BlockSpec((1, 2 * DIFF_HEADS, tm, DIFF_QK_DIM), lambda b, i: (b, 0, i, 0)),
            pl.BlockSpec((1, DIFF_HEADS, 1, VT_ROWS, tm), lambda b, i: (b, 0, i, 0, 0)),
        ),
        compiler_params=_params(("arbitrary", "arbitrary")),
        name="in_proj",
    )(xc, mod, w_in, gains, seg, rope_g, rope_d)


def _expm1(x):
    u = jnp.exp(x)
    return jnp.where(u == 1.0, x, (u - 1.0) * x / jnp.log(u))


def _lru_kernel(ax_ref, prev_ref, next_ref, cw_ref, cb_ref, w_ref, b_ref, lam_ref, h_ref, carry_ref, *, nt):
    tm = ax_ref.shape[1]
    d = pl.program_id(1)
    j = pl.program_id(2)
    blk = jnp.where(d == 0, j, jnp.where(j == 0, 0, nt - j))

    @pl.when(j == 0)
    def _():
        carry_ref[...] = jnp.zeros_like(carry_ref)

    x = ax_ref[0]
    no_left = jnp.logical_or(blk == 0, blk == 1)
    no_right = jnp.logical_or(blk == 0, blk == nt - 1)
    left = jnp.where(no_left, 0.0, prev_ref[0, SUBLANES - 1:SUBLANES, :])
    right0 = jnp.where(no_right, 0.0, next_ref[0, 0:1, :])
    right1 = jnp.where(no_right, 0.0, next_ref[0, 1:2, :])
    row = lax.broadcasted_iota(I32, (tm, 1), 0)
    x_m1 = jnp.where(row == 0, left, pltpu.roll(x, 1, 0))
    x_p1 = jnp.where(row == tm - 1, right0, pltpu.roll(x, tm - 1, 0))
    x_p2 = jnp.where(row == tm - 2, right0, jnp.where(row == tm - 1, right1, pltpu.roll(x, tm - 2, 0)))
    u = cb_ref[...] + x_m1 * cw_ref[0:1, :] + x * cw_ref[1:2, :] + x_p1 * cw_ref[2:3, :] + x_p2 * cw_ref[3:4, :]

    g = jnp.dot(u.astype(BF16), w_ref[0], preferred_element_type=F32) + b_ref[0]
    r = jax.nn.sigmoid(g[:, :LRU_WIDTH])
    gate_in = jax.nn.sigmoid(g[:, LRU_WIDTH:])
    neg_lam = -lam_ref[0]
    softplus = jnp.maximum(neg_lam, 0.0) + jnp.log1p(jnp.exp(-jnp.abs(neg_lam)))
    log_a = (-RG_C * r) * softplus
    a = jnp.exp(log_a)
    b = jnp.sqrt(-_expm1(2.0 * log_a)) * (gate_in * u)

    def scan(reverse):
        aa, bb = a, b
        in_group = row % SUBLANES
        s = 1
        while s < SUBLANES:
            shift = tm - s if reverse else s
            keep = (in_group < SUBLANES - s) if reverse else (in_group >= s)
            a_sh = jnp.where(keep, pltpu.roll(aa, shift, 0), 1.0)
            b_sh = jnp.where(keep, pltpu.roll(bb, shift, 0), 0.0)
            bb = aa * b_sh + bb
            aa = aa * a_sh
            s *= 2
        state = carry_ref[...]
        n_groups = tm // SUBLANES
        for g in (range(n_groups - 1, -1, -1) if reverse else range(n_groups)):
            lo = g * SUBLANES
            h = aa[lo:lo + SUBLANES] * state + bb[lo:lo + SUBLANES]
            h_ref[0, 0, lo:lo + SUBLANES, :] = h
            state = h[0:1, :] if reverse else h[SUBLANES - 1:SUBLANES, :]
        carry_ref[...] = state

    @pl.when(d == 0)
    def _():
        scan(False)

    @pl.when(d == 1)
    def _():
        scan(True)


def _lru(zl, conv_w, conv_b, w_gates, b_gates, lam):
    bsz, sk, _ = zl.shape
    tm = TOKEN_TILE
    nt = sk // tm
    per = tm // SUBLANES

    def blk_of(d, j):
        return jnp.where(d == 0, j, jnp.where(j == 0, 0, nt - j))

    return pl.pallas_call(
        functools.partial(_lru_kernel, nt=nt),
        out_shape=jax.ShapeDtypeStruct((2, bsz, sk, LRU_WIDTH), F32),
        grid=(bsz, 2, nt),
        in_specs=[
            pl.BlockSpec((1, tm, LRU_WIDTH), lambda b, d, j: (b, blk_of(d, j), 0)),
            pl.BlockSpec((1, SUBLANES, LRU_WIDTH), lambda b, d, j: (b, jnp.maximum(blk_of(d, j) * per - 1, 0), 0)),
            pl.BlockSpec((1, SUBLANES, LRU_WIDTH),
                         lambda b, d, j: (b, jnp.minimum((blk_of(d, j) + 1) * per, nt * per - 1), 0)),
            pl.BlockSpec((CONV_W, LRU_WIDTH), lambda b, d, j: (0, 0)),
            pl.BlockSpec((1, LRU_WIDTH), lambda b, d, j: (0, 0)),
            pl.BlockSpec((1, LRU_WIDTH, 2 * LRU_WIDTH), lambda b, d, j: (d, 0, 0)),
            pl.BlockSpec((1, 1, 2 * LRU_WIDTH), lambda b, d, j: (d, 0, 0)),
            pl.BlockSpec((1, 1, LRU_WIDTH), lambda b, d, j: (d, 0, 0)),
        ],
        out_specs=pl.BlockSpec((1, 1, tm, LRU_WIDTH), lambda b, d, j: (d, b, blk_of(d, j), 0)),
        scratch_shapes=[pltpu.VMEM((1, LRU_WIDTH), F32)],
        compiler_params=_params(("arbitrary", "arbitrary", "arbitrary")),
        name="rg_lru",
    )(zl, zl, zl, conv_w, conv_b, w_gates, b_gates, lam)


def _attn_kernel(q_ref, k_ref, vt_ref, aux_ref, o_ref, m_ref, acc_ref, *, k_heads, v_heads, nt, n_ctx_tiles,
                 diff):
    tq = q_ref.shape[2]
    tk = tq
    dv = GQA_HEAD_DIM
    qi = pl.program_id(2)

    def scores(kt):
        return jnp.concatenate(
            [lax.dot_general(k_ref[0, k_heads[j], kt * tk:(kt + 1) * tk, :], q_ref[0, j], NT_DIMS,
                             preferred_element_type=F32) for j in range(4)], axis=1)

    def weighted_values(kt, p):
        return jnp.concatenate(
            [jnp.dot(vt_ref[0, v_heads[j], kt], p[:, j * tq:(j + 1) * tq], preferred_element_type=F32)
             for j in range(4)], axis=1)

    def finalize(acc_all):
        outs = []
        for j in range(4):
            acc = acc_all[:, j * tq:(j + 1) * tq]
            outs.append(acc[0:dv] / acc[dv:dv + 1])
        if diff:
            dl = aux_ref[0:4, :]
            lam_init = aux_ref[4:5, 0:1]
            lam = (jnp.exp(jnp.sum(dl[0:1] * dl[1:2], axis=1, keepdims=True))
                   - jnp.exp(jnp.sum(dl[2:3] * dl[3:4], axis=1, keepdims=True)) + lam_init)
            heads = []
            for hd in range(2):
                o = outs[2 * hd] - lam * outs[2 * hd + 1]
                heads.append(o * lax.rsqrt(jnp.mean(o * o, axis=0, keepdims=True) + EPS))
            o_ref[0] = ((jnp.concatenate(heads, axis=0).T * aux_ref[5:6, :]) * (1.0 - lam_init)).astype(o_ref.dtype)
        else:
            for pair in range(2):
                o_ref[0, :, pair * 2 * dv:(pair + 1) * 2 * dv] = (
                    jnp.concatenate(outs[2 * pair:2 * pair + 2], axis=0).T.astype(o_ref.dtype))

    s_first = scores(0)
    m_first = jnp.max(s_first, axis=0, keepdims=True)

    @pl.when(qi < n_ctx_tiles)
    def _():
        finalize(weighted_values(0, jnp.exp2(s_first - m_first).astype(BF16)))

    @pl.when(qi >= n_ctx_tiles)
    def _():
        m_ref[...] = m_first

        def attempt(again):
            m0 = m_ref[...]
            s_cur = s_first
            peak = jnp.zeros((1, 4 * tq), BF16)
            acc = None
            for u in range(nt):
                s_next = scores(u + 1) if u + 1 < nt else None
                p = jnp.exp2(s_cur - m0).astype(BF16)
                peak = jnp.maximum(peak, jnp.max(p, axis=0, keepdims=True))
                pv = weighted_values(u, p)
                acc = pv if acc is None else acc + pv
                s_cur = s_next
            acc_ref[...] = acc
            peak = peak.astype(F32)
            lagging = jnp.max(peak) > 2.0 ** MAX_SHIFT_LAG

            @pl.when(lagging)
            def _():
                m_ref[...] = m0 + jnp.clip(jnp.log(peak) * LOG2_E, 0.0, MAX_SHIFT_RAISE)

            return lagging

        lax.while_loop(lambda again: again, attempt, True)
        finalize(acc_ref[...])


def _attention(q, k, vt, aux, *, diff, n_ctx_tiles):
    bsz, n_q, sk, dh = q.shape
    n_k_heads = k.shape[1]
    n_v_heads = vt.shape[1]
    tq = TOKEN_TILE
    nt = sk // tq
    assert n_ctx_tiles == 1
    groups = n_q // 4
    kpg = n_k_heads // groups
    vpg = n_v_heads // groups
    k_heads = tuple(j * kpg // 4 for j in range(4))
    v_heads = tuple(j * vpg // 4 for j in range(4))
    width = 2 * DIFF_V_DIM if diff else 4 * GQA_HEAD_DIM
    return pl.pallas_call(
        functools.partial(_attn_kernel, k_heads=k_heads, v_heads=v_heads, nt=nt, n_ctx_tiles=n_ctx_tiles, diff=diff),
        out_shape=jax.ShapeDtypeStruct((bsz, sk, groups * width), BF16),
        grid=(bsz, groups, nt),
        in_specs=[
            pl.BlockSpec((1, 4, tq, dh), lambda b, g, i: (b, g, i, 0)),
            pl.BlockSpec((1, kpg, sk, dh), lambda b, g, i: (b, g, 0, 0)),
            pl.BlockSpec((1, vpg, nt, VT_ROWS, tq), lambda b, g, i: (b, g, 0, 0, 0)),
            pl.BlockSpec((SUBLANES, LANES), lambda b, g, i: (0, 0)),
        ],
        out_specs=pl.BlockSpec((1, tq, width), lambda b, g, i: (b, i, g)),
        scratch_shapes=[pltpu.VMEM((1, 4 * tq), F32), pltpu.VMEM((VT_ROWS, 4 * tq), F32)],
        compiler_params=_params(("arbitrary", "arbitrary", "arbitrary")),
        name="diff_attention" if diff else "gqa_attention",
    )(q, k, vt, aux)


def _layer_norm(r, g, b):
    mu = jnp.mean(r, axis=-1, keepdims=True)
    c = r - mu
    var = jnp.mean(c * c, axis=-1, keepdims=True)
    return c * lax.rsqrt(var + EPS) * g + b


def _route(logits, bias):
    mx = jnp.max(logits, axis=0, keepdims=True)
    e = jnp.exp(logits - mx)
    probs = e / jnp.sum(e, axis=0, keepdims=True)
    sel = probs + bias
    epg = EXPERTS_PER_GROUP
    in_top2, scores = [], []
    for g in range(N_GROUPS):
        v = [sel[g * epg + i:g * epg + i + 1] for i in range(epg)]
        masks = []
        for i in range(epg):
            rank = jnp.zeros_like(v[i])
            for j in range(epg):
                if j != i:
                    beats = (v[j] > v[i]) if j > i else (v[j] >= v[i])
                    rank = rank + beats.astype(F32)
            masks.append(rank < 2.0)
        in_top2.append(masks)
        scores.append(sum(jnp.where(masks[i], v[i], 0.0) for i in range(epg)))
    chosen = []
    for g in range(N_GROUPS):
        c = None
        for j in range(N_GROUPS):
            if j != g:
                t = (scores[j] < scores[g]) if j < g else (scores[j] <= scores[g])
                c = t if c is None else jnp.logical_and(c, t)
        chosen.append(c)
    picked, weight = [], []
    for i in range(epg):
        m = None
        w = jnp.zeros_like(scores[0])
        for g in range(N_GROUPS):
            t = jnp.logical_and(chosen[g], in_top2[g][i])
            m = t if m is None else jnp.logical_or(m, t)
            w = w + jnp.where(t, probs[g * epg + i:g * epg + i + 1], 0.0)
        picked.append(m)
        weight.append(w)
    total = weight[0] + weight[1] + weight[2] + weight[3]
    gate = [w / total for w in weight]
    lo = jnp.where(picked[0], 0.0, jnp.where(picked[1], 1.0, 2.0))
    hi = jnp.where(picked[3], 3.0, jnp.where(picked[2], 2.0, 1.0))
    gate_lo = jnp.where(picked[0], gate[0], jnp.where(picked[1], gate[1], gate[2]))
    gate_hi = jnp.where(picked[3], gate[3], jnp.where(picked[2], gate[2], gate[1]))
    group = sum(jnp.where(chosen[g], float(g), 0.0) for g in range(N_GROUPS))
    base = jnp.where(lo == 0.0, 0.0, jnp.where(lo == 1.0, 3.0, 5.0))
    natural = base + hi - lo - 1.0
    pair = jnp.where(natural == 3.0, 4.0, jnp.where(natural == 4.0, 3.0, natural))
    swapped = natural == 5.0
    cls = group * float(len(PAIRS_A)) + pair
    return cls.astype(I32), jnp.where(swapped, gate_hi, gate_lo), jnp.where(swapped, gate_lo, gate_hi)


def _out_proj_kernel(hf_ref, hb_ref, ag_ref, yb_ref, yc_ref, x_ref, mod_ref, w_ref, lng_ref, lnb_ref,
                     rw_ref, rb_ref, x1_ref, ri_ref, cnt_ref, run_ref, *, alpha, n_ctx_tiles, n_batch, sub_tiles):
    tm = x_ref.shape[0] // sub_tiles
    d = x_ref.shape[1]
    n_a = LRU_WIDTH
    n_b = n_a + GQA_HEADS * GQA_HEAD_DIM
    b = pl.program_id(0)
    i = pl.program_id(1)

    @pl.when(jnp.logical_and(b == 0, i == 0))
    def _():
        run_ref[...] = jnp.zeros_like(run_ref)

    before = (lax.broadcasted_iota(I32, (tm, tm), 0) < lax.broadcasted_iota(I32, (tm, tm), 1)).astype(BF16)
    run = run_ref[...]
    logit_parts, mod_rows = [], []
    for sub in range(sub_tiles):
        rows = slice(sub * tm, (sub + 1) * tm)
        mod_row = jnp.where(i * sub_tiles + sub < n_ctx_tiles, n_batch, b)
        ya = (jax.nn.gelu(ag_ref[0, rows, :]) * (hf_ref[0, 0, rows, :] + hb_ref[0, 0, rows, :])).astype(BF16)
        y = (jnp.dot(ya, w_ref[0:n_a, :], preferred_element_type=F32)
             + jnp.dot(yb_ref[0, rows, :], w_ref[n_a:n_b, :], preferred_element_type=F32)
             + jnp.dot(yc_ref[0, rows, :], w_ref[n_b:, :], preferred_element_type=F32))
        x1 = _layer_norm(alpha * x_ref[rows, :] + mod_ref[mod_row, 2:3, :] * y, lng_ref[...], lnb_ref[...])
        x1_ref[rows, 0:d] = x1

        h2 = x1 * (1.0 + mod_ref[mod_row, 4:5, :]) + mod_ref[mod_row, 3:4, :]
        h_hi = h2.astype(BF16)
        h_lo = (h2 - h_hi.astype(F32)).astype(BF16)
        both = lax.dot_general(rw_ref[...], h_hi, NT_DIMS, preferred_element_type=F32)
        logit_parts.append(both[0:N_EXPERTS] + both[N_EXPERTS:]
                           + lax.dot_general(rw_ref[0:N_EXPERTS, :], h_lo, NT_DIMS, preferred_element_type=F32))
        mod_rows.append(mod_row)

    cls, gate_a, gate_b = _route(jnp.concatenate(logit_parts, axis=1), rb_ref[...])
    member = lax.broadcasted_iota(I32, (CLASS_ROWS, sub_tiles * tm), 0) == cls
    for sub in range(sub_tiles):
        rows = slice(sub * tm, (sub + 1) * tm)
        extra = jnp.concatenate([gate_a[:, rows], gate_b[:, rows], jnp.full((1, tm), mod_rows[sub].astype(F32), F32),
                                 jnp.zeros((LANES - 3, tm), F32)], axis=0)
        x1_ref[rows, d:] = extra.T

        in_class = member[:, rows]
        earlier = jnp.dot(in_class.astype(BF16), before, preferred_element_type=F32)
        rank = jnp.sum(jnp.where(in_class, earlier + run, 0.0), axis=0, keepdims=True).astype(I32)
        run = run + jnp.sum(in_class.astype(F32), axis=1, keepdims=True)
        ri_ref[0, :, rows] = jnp.concatenate([cls[:, rows], rank, jnp.zeros((SUBLANES - 2, tm), I32)], axis=0)
    run_ref[...] = run
    cnt_ref[...] = jnp.broadcast_to(run, cnt_ref.shape).astype(I32)


def _out_proj(h_lru, zl, yb, yc, xc, mod, w_out, ln_g, ln_b, rw_split, rb, alpha, n_ctx_tiles):
    bsz, sk, _ = zl.shape
    d = xc.shape[1]
    nt = sk // TOKEN_TILE
    sub_tiles = OUT_PROJ_TILES if nt % OUT_PROJ_TILES == 0 else 1
    tm = sub_tiles * TOKEN_TILE
    nt = nt // sub_tiles

    return pl.pallas_call(
        functools.partial(_out_proj_kernel, alpha=alpha, n_ctx_tiles=n_ctx_tiles, n_batch=bsz,
                          sub_tiles=sub_tiles),
        out_shape=(
            jax.ShapeDtypeStruct((bsz * sk, d + LANES), F32),
            jax.ShapeDtypeStruct((bsz, SUBLANES, sk), I32),
            jax.ShapeDtypeStruct((CLASS_ROWS, LANES), I32),
        ),
        grid=(bsz, nt),
        in_specs=[
            pl.BlockSpec((1, 1, tm, LRU_WIDTH), lambda b, i: (0, b, i, 0)),
            pl.BlockSpec((1, 1, tm, LRU_WIDTH), lambda b, i: (1, b, i, 0)),
            pl.BlockSpec((1, tm, LRU_WIDTH), lambda b, i: (b, i, 1)),
            pl.BlockSpec((1, tm, GQA_HEADS * GQA_HEAD_DIM), lambda b, i: (b, i, 0)),
            pl.BlockSpec((1, tm, DIFF_HEADS * DIFF_V_DIM), lambda b, i: (b, i, 0)),
            pl.BlockSpec((tm, d), lambda b, i: (b * nt + i, 0)),
            pl.BlockSpec(mod.shape, lambda b, i: (0, 0, 0)),
            pl.BlockSpec((d, d), lambda b, i: (0, 0)),
            pl.BlockSpec((1, d), lambda b, i: (0, 0)),
            pl.BlockSpec((1, d), lambda b, i: (0, 0)),
            pl.BlockSpec((2 * N_EXPERTS, d), lambda b, i: (0, 0)),
            pl.BlockSpec((N_EXPERTS, 1), lambda b, i: (0, 0)),
        ],
        out_specs=(
            pl.BlockSpec((tm, d + LANES), lambda b, i: (b * nt + i, 0)),
            pl.BlockSpec((1, SUBLANES, tm), lambda b, i: (b, 0, i)),
            pl.BlockSpec((CLASS_ROWS, LANES), lambda b, i: (0, 0)),
        ),
        scratch_shapes=[pltpu.VMEM((CLASS_ROWS, 1), F32)],
        compiler_params=_params(("arbitrary", "arbitrary")),
        name="out_proj_router",
    )(h_lru, h_lru, zl, yb, yc, xc, mod, w_out, ln_g, ln_b, rw_split, rb)


def _moe_kernel(cls_ref, rank_ref, pstart_ref, ea_ref, eb_ref, nv_ref, nu_ref,
                x_hbm, mod_ref, lng_ref, lnb_ref,
                wga_ref, wua_ref, wda_ref, wgb_ref, wub_ref, wdb_ref,
                o_hbm, tok_ref, xbuf, obuf, gsem, ssem, *, alpha, n_batch, n_tok):
    rows = xbuf.shape[1]
    d = obuf.shape[2]
    i = pl.program_id(0)
    n_steps = pl.num_programs(0)
    n_used = nu_ref[0]
    slot = i % 2

    @pl.when(i == 0)
    def _():
        def clear(r, c):
            tok_ref[r] = 0
            return c
        lax.fori_loop(0, tok_ref.shape[0], clear, 0, unroll=8)

        def place(t, c):
            tok_ref[pstart_ref[cls_ref[t]] + rank_ref[t]] = t
            return c
        lax.fori_loop(0, n_tok, place, 0, unroll=8)

        obuf[...] = jnp.zeros_like(obuf)

    def start_gather(step, slot_):
        for r in range(rows):
            pltpu.make_async_copy(x_hbm.at[pl.ds(tok_ref[step * rows + r], 1)], xbuf.at[slot_, pl.ds(r, 1)],
                                  gsem.at[slot_]).start()

    def wait_gather(slot_):
        pltpu.make_async_copy(x_hbm.at[pl.ds(0, rows)], xbuf.at[slot_], gsem.at[slot_]).wait()

    def start_scatter(step, n_valid, slot_):
        for r in range(rows):
            dst = jnp.where(r < n_valid, tok_ref[step * rows + r], n_tok + slot_ * rows + r)
            pltpu.make_async_copy(obuf.at[slot_, pl.ds(r, 1)], o_hbm.at[pl.ds(dst, 1)],
                                  ssem.at[slot_]).start(priority=r % 2)

    def wait_scatter(slot_):
        pltpu.make_async_copy(obuf.at[slot_], o_hbm.at[pl.ds(0, rows)], ssem.at[slot_]).wait()

    def block_step(slot):
        wait_gather(slot)
        start_gather(jnp.minimum(i + 1, n_used - 1), 1 - slot)
        prev = jnp.maximum(i - 1, 0)
        start_scatter(prev, jnp.where(i >= 1, nv_ref[prev], 0), 1 - slot)

        x = xbuf[slot, :, 0:d]
        gate_a = xbuf[slot, :, d:d + 1]
        gate_b = xbuf[slot, :, d + 1:d + 2]
        mid = xbuf[slot, :, d + 2:d + 3]

        def pick(kk):
            v = mod_ref[n_batch, kk:kk + 1, :]
            for b in range(n_batch):
                v = jnp.where(mid == float(b), mod_ref[b, kk:kk + 1, :], v)
            return v

        h = x * (1.0 + pick(4)) + pick(3)

        def ffn(wg, wu, wd):
            a = jnp.dot(h, wg[0], preferred_element_type=F32)
            u = jnp.dot(h, wu[0], preferred_element_type=F32)
            return jnp.dot((a * jax.nn.sigmoid(a)) * u, wd[0], preferred_element_type=F32)

        f = ffn(wga_ref, wua_ref, wda_ref) * gate_a + ffn(wgb_ref, wub_ref, wdb_ref) * gate_b
        y = _layer_norm(alpha * x + pick(5) * f, lng_ref[...], lnb_ref[...])
        wait_scatter(slot)
        obuf[slot] = y

    @pl.when(i == 0)
    def _():
        pltpu.make_async_copy(obuf.at[0], o_hbm.at[pl.ds(n_tok, rows)], ssem.at[0]).start()
        start_gather(0, 0)

    for parity in range(2):
        @pl.when(jnp.logical_and(i < n_used, i % 2 == parity))
        def _(parity=parity):
            block_step(parity)

    @pl.when(i == n_steps - 1)
    def _():
        last = n_used - 1
        q = last % 2
        start_scatter(last, nv_ref[last], q)
        wait_scatter(q)
        wait_scatter(1 - q)
        wait_gather(1 - q)


def _moe(x_wide, n_tok, cls, rank, pstart, blk_ea, blk_eb, blk_nv, n_used, mod, ln_g, ln_b,
         w_gate, w_up, w_down, layer, alpha, n_batch):
    d = x_wide.shape[1] - LANES
    rows = MOE_ROWS
    n_blk = blk_ea.shape[0]
    ff = w_gate.shape[-1]
    first = layer * N_EXPERTS

    def wa(i, cl, rk, ps, ea, eb, nv, nu):
        return (first + ea[i], 0, 0)

    def wb(i, cl, rk, ps, ea, eb, nv, nu):
        return (first + eb[i], 0, 0)

    const2 = lambda i, cl, rk, ps, ea, eb, nv, nu: (0, 0)
    grid_spec = pltpu.PrefetchScalarGridSpec(
        num_scalar_prefetch=7,
        grid=(n_blk,),
        in_specs=[
            pl.BlockSpec(memory_space=pl.ANY),
            pl.BlockSpec(mod.shape, lambda i, cl, rk, ps, ea, eb, nv, nu: (0, 0, 0)),
            pl.BlockSpec((1, d), const2),
            pl.BlockSpec((1, d), const2),
            pl.BlockSpec((1, d, ff), wa),
            pl.BlockSpec((1, d, ff), wa),
            pl.BlockSpec((1, ff, d), wa),
            pl.BlockSpec((1, d, ff), wb),
            pl.BlockSpec((1, d, ff), wb),
            pl.BlockSpec((1, ff, d), wb),
        ],
        out_specs=pl.BlockSpec(memory_space=pl.ANY),
        scratch_shapes=[
            pltpu.SMEM((n_blk * rows,), I32),
            pltpu.VMEM((2, rows, d + LANES), F32),
            pltpu.VMEM((2, rows, d), F32),
            pltpu.SemaphoreType.DMA((2,)),
            pltpu.SemaphoreType.DMA((2,)),
        ],
    )
    return pl.pallas_call(
        functools.partial(_moe_kernel, alpha=alpha, n_batch=n_batch, n_tok=n_tok),
        out_shape=jax.ShapeDtypeStruct((n_tok + 2 * rows, d), F32),
        grid_spec=grid_spec,
        compiler_params=_params(("arbitrary",)),
        name="moe_experts",
    )(cls, rank, pstart, blk_ea, blk_eb, blk_nv, n_used, x_wide, mod, ln_g, ln_b,
      w_gate, w_up, w_down, w_gate, w_up, w_down)


def _block_tables(counts, n_tok):
    rows = MOE_ROWS
    n_blk = n_tok // rows + N_CLASSES
    n_pairs = len(PAIRS_A)
    padded = (counts + rows - 1) // rows * rows
    pend = jnp.cumsum(padded)
    pstart = pend - padded
    n_used = pend[-1] // rows
    blk = jnp.arange(n_blk, dtype=I32)
    blk_ids = jnp.minimum(blk, jnp.maximum(n_used - 1, 0))
    blk_cls = jnp.minimum(jnp.sum((blk_ids[:, None] * rows >= pend[None, :]).astype(I32), axis=1), N_CLASSES - 1)
    onehot = blk_cls[:, None] == jnp.arange(N_CLASSES, dtype=I32)[None, :]
    within = blk_ids * rows - jnp.sum(jnp.where(onehot, pstart[None, :], 0), axis=1)
    count_b = jnp.sum(jnp.where(onehot, counts[None, :], 0), axis=1)
    blk_nv = jnp.where(blk < n_used, jnp.clip(count_b - within, 0, rows), 0).astype(I32)
    pair = blk_cls % n_pairs
    slot_a = sum(jnp.where(pair == p, PAIRS_A[p], 0) for p in range(n_pairs))
    slot_b = sum(jnp.where(pair == p, PAIRS_B[p], 0) for p in range(n_pairs))
    first = blk_cls // n_pairs * EXPERTS_PER_GROUP
    pstart_pad = jnp.concatenate([pstart, jnp.zeros((CLASS_ROWS - N_CLASSES,), I32)]).astype(I32)
    return (pstart_pad, (first + slot_a).astype(I32), (first + slot_b).astype(I32), blk_nv,
            n_used.reshape(1).astype(I32))


def _rope_tables(n_lat, n_ctx, dim):
    rows = n_lat // GRID_W
    row = jnp.repeat(jnp.arange(rows, dtype=F32), GRID_W)
    col = jnp.tile(jnp.arange(GRID_W, dtype=F32), rows)
    n_freq = dim // 4
    inv = ROPE_THETA ** (-jnp.arange(n_freq, dtype=F32) / n_freq)
    ang = jnp.concatenate([row[:, None] * inv, col[:, None] * inv], axis=-1)
    cos = jnp.repeat(jnp.cos(ang), 2, axis=-1)
    sin = jnp.repeat(jnp.sin(ang), 2, axis=-1)
    even = (jnp.arange(dim) % 2 == 0)
    tabs = jnp.stack([cos, jnp.where(even, -sin, 0.0), jnp.where(even, 0.0, sin)])
    ident = jnp.stack([jnp.ones((n_ctx, dim), F32), jnp.zeros((n_ctx, dim), F32), jnp.zeros((n_ctx, dim), F32)])
    return jnp.tile(jnp.concatenate([ident, tabs], axis=1), (1, 1, LANES // dim))


def _block_diag(w):
    n, c, _ = w.shape
    eye = jnp.eye(n, dtype=w.dtype)
    return (eye[:, None, :, None] * w[:, :, None, :]).reshape(n * c, n * c)


def kernel(x, c, ctx, c_ctx, ada_w, ada_b, w_in, conv_w, conv_b, rg_wa, rg_ba, rg_wx, rg_bx, rg_lam, q_norm_g, k_norm_g, diff_lambda, diff_subln_g, w_out, ln1_g, ln1_b, ln2_g, ln2_b, router_w, router_b, exp_w_gate, exp_w_up, exp_w_down):
    bsz, s, d = x.shape
    n_ctx = ctx.shape[1]
    depth = w_in.shape[0]
    sk = n_ctx + s
    assert d == D_MODEL and n_ctx == TOKEN_TILE and s % TOKEN_TILE == 0 and bsz + 1 <= SUBLANES
    n_ctx_tiles = n_ctx // TOKEN_TILE
    n_tok = bsz * sk
    alpha = (2.0 * depth) ** 0.25

    c_rows = jnp.concatenate([c, c_ctx[None, :], jnp.zeros((SUBLANES - bsz - 1, d), F32)], axis=0)
    mods = _ada_modulation(c_rows, ada_w, ada_b).reshape(depth, SUBLANES, N_MOD, d)

    rope_g = _rope_tables(s, n_ctx, GQA_HEAD_DIM)
    rope_d = _rope_tables(s, n_ctx, DIFF_QK_DIM)
    lane = np.arange(LANES)
    seg = jnp.asarray(lane[:, None] // GQA_HEAD_DIM == lane[None, :] // GQA_HEAD_DIM, BF16)
    rw_t = router_w.T
    rw_hi = rw_t.astype(BF16)
    rw_split = jnp.concatenate([rw_hi, (rw_t - rw_hi.astype(F32)).astype(BF16)], axis=0)
    rb = router_b.reshape(N_EXPERTS, 1)
    zero_aux = jnp.zeros((SUBLANES, LANES), F32)

    xc = jnp.concatenate([part for b in range(bsz) for part in (ctx[b], x[b])], axis=0)
    w_gate_all = exp_w_gate.reshape(depth * N_EXPERTS, d, EXPERT_FF)
    w_up_all = exp_w_up.reshape(depth * N_EXPERTS, d, EXPERT_FF)
    w_down_all = exp_w_down.reshape(depth * N_EXPERTS, EXPERT_FF, d)
    for li in range(depth):
        lam_init = 0.8 - 0.6 * math.exp(-0.3 * li)
        mod = mods[li]
        gains = jnp.concatenate([jnp.tile(jnp.tile(q_norm_g[li], 2)[None, :], (4, 1)),
                                 jnp.tile(k_norm_g[li], 2)[None, :], jnp.zeros((3, LANES), F32)], axis=0)
        zl, q, k, vt, dq, dk, dvt = _in_proj(xc, bsz, sk, mod, w_in[li].astype(BF16), gains, seg, rope_g, rope_d,
                                             n_ctx_tiles)

        w_gates = jnp.stack([jnp.concatenate([_block_diag(rg_wa[li, dd]), _block_diag(rg_wx[li, dd])], axis=1)
                             for dd in range(2)]).astype(BF16)
        b_gates = jnp.concatenate([rg_ba[li], rg_bx[li]], axis=-1)[:, None, :]
        h_lru = _lru(zl, conv_w[li], conv_b[li][None, :], w_gates, b_gates, rg_lam[li][:, None, :])

        yb = _attention(q, k, vt, zero_aux, diff=False, n_ctx_tiles=n_ctx_tiles)
        aux = jnp.concatenate([
            jnp.pad(diff_lambda[li], ((0, 0), (0, LANES - DIFF_QK_DIM))),
            jnp.full((1, LANES), lam_init, F32),
            jnp.tile(diff_subln_g[li], 2)[None, :],
            jnp.zeros((2, LANES), F32)], axis=0)
        yc = _attention(dq, dk, dvt, aux, diff=True, n_ctx_tiles=n_ctx_tiles)

        x1, route_i, counts = _out_proj(h_lru, zl, yb, yc, xc, mod, w_out[li].astype(BF16), ln1_g[li][None, :],
                                        ln1_b[li][None, :], rw_split, rb, alpha, n_ctx_tiles)

        pstart, blk_ea, blk_eb, blk_nv, n_used = _block_tables(counts[:N_CLASSES, 0], n_tok)
        xc = _moe(x1, n_tok, route_i[:, 0, :].reshape(n_tok), route_i[:, 1, :].reshape(n_tok), pstart,
                  blk_ea, blk_eb, blk_nv, n_used, mod, ln2_g[li][None, :], ln2_b[li][None, :],
                  w_gate_all, w_up_all, w_down_all, li, alpha, bsz)
    return jnp.stack([xc[b * sk + n_ctx:(b + 1) * sk] for b in range(bsz)])
```
